```python
import jax, jax.numpy as jnp
from jax import lax
import numpy as np

D_MODEL = 2048
BATCH = 4
SEQ = 2048
DEPTH = 2

HEAD_DIM = 128
FOX_HEADS = 8
DSA_HEADS = 8
IDX_HEADS = 16
IDX_DIM = 64
IDX_TOPK_MAX = 256
SWA_HEADS = 32
SWA_KV_HEADS = 4
SWA_HEAD_DIM = 64
SWA_WINDOW = 128
Q_BLOCK = 128
DSA_Q_BLOCK = 64
ROPE_THETA = 10000.0
EPS = 1e-6
NEG = -1e30

FOX_W = FOX_HEADS * HEAD_DIM
DSA_W = DSA_HEADS * HEAD_DIM
SWA_W = SWA_HEADS * SWA_HEAD_DIM
SWA_KV_W = SWA_KV_HEADS * SWA_HEAD_DIM
EVEN_SPLITS = [FOX_W, FOX_W, FOX_W, FOX_W, FOX_HEADS,
               DSA_W, DSA_W, DSA_W, DSA_W,
               IDX_HEADS * IDX_DIM, IDX_DIM, IDX_HEADS]
EVEN_IN = sum(EVEN_SPLITS)
ODD_SPLITS = [SWA_W, SWA_KV_W, SWA_KV_W, SWA_W]
ODD_IN = sum(ODD_SPLITS)
N_EVEN = (DEPTH + 1) // 2
N_ODD = DEPTH // 2

kernel_name = "hybrid_fox_dsa_swa_gated"


def rmsnorm(x, g):
    xf = x.astype(jnp.float32)
    y = xf * lax.rsqrt(jnp.mean(xf * xf, axis=-1, keepdims=True) + EPS)
    return (y * g.astype(jnp.float32)).astype(x.dtype)


def split_cols(h, sizes):
    return jnp.split(h, np.cumsum(sizes)[:-1].tolist(), axis=-1)


def rope_tables(seq, dim):
    inv = 1.0 / (ROPE_THETA ** (jnp.arange(0, dim, 2, dtype=jnp.float32) / dim))
    ang = jnp.arange(seq, dtype=jnp.float32)[:, None] * inv[None, :]
    return jnp.cos(ang), jnp.sin(ang)


def apply_rope(x, cos, sin):
    x1, x2 = jnp.split(x.astype(jnp.float32), 2, axis=-1)
    c = cos[None, :, None, :]
    s = sin[None, :, None, :]
    return jnp.concatenate([x1 * c - x2 * s, x1 * s + x2 * c], axis=-1).astype(x.dtype)


def fox_attention(q, k, v, log_f):
    B, S, H, D = q.shape
    nb = S // Q_BLOCK
    c = jnp.cumsum(log_f, axis=1).transpose(0, 2, 1)
    qb = q.reshape(B, nb, Q_BLOCK, H, D).transpose(1, 0, 2, 3, 4)
    cb = c.reshape(B, H, nb, Q_BLOCK).transpose(2, 0, 1, 3)
    kpos = jnp.arange(S)
    scale = D ** -0.5

    def block(args):
        i, qi, ci = args
        qpos = i * Q_BLOCK + jnp.arange(Q_BLOCK)
        s = jnp.einsum('bqhd,bkhd->bhqk', qi, k).astype(jnp.float32) * scale
        s = s + ci[..., None] - c[:, :, None, :]
        s = jnp.where((kpos[None, :] <= qpos[:, None])[None, None], s, NEG)
        p = jax.nn.softmax(s, axis=-1)
        return jnp.einsum('bhqk,bkhd->bqhd', p.astype(v.dtype), v)

    out = lax.map(block, (jnp.arange(nb), qb, cb))
    return out.transpose(1, 0, 2, 3, 4).reshape(B, S, H, D)


def dsa_attention(q, k, v, iq, ik, iw, topk):
    B, S, H, D = q.shape
    nb = S // DSA_Q_BLOCK
    qb = q.reshape(B, nb, DSA_Q_BLOCK, H, D).transpose(1, 0, 2, 3, 4)
    iqb = iq.reshape(B, nb, DSA_Q_BLOCK, IDX_HEADS, IDX_DIM).transpose(1, 0, 2, 3, 4)
    iwb = iw.reshape(B, nb, DSA_Q_BLOCK, IDX_HEADS).transpose(1, 0, 2, 3)
    kpos = jnp.arange(S)
    gather = jax.vmap(lambda xb, ib: xb[ib])

    def block(args):
        i, qi, iqi, iwi = args
        qpos = i * DSA_Q_BLOCK + jnp.arange(DSA_Q_BLOCK)
        logits = jnp.einsum('bqhd,bkd->bqhk', iqi, ik).astype(jnp.float32) * IDX_DIM ** -0.5
        score = jnp.einsum('bqh,bqhk->bqk', iwi.astype(jnp.float32), jax.nn.relu(logits))
        score = jnp.where((kpos[None, :] <= qpos[:, None])[None], score, NEG)
        _, idx = lax.top_k(score, topk)
        valid = idx <= qpos[None, :, None]
        ks = gather(k, idx)
        vs = gather(v, idx)
        s = jnp.einsum('bqhd,bqkhd->bhqk', qi, ks).astype(jnp.float32) * D ** -0.5
        s = jnp.where(valid[:, None], s, NEG)
        p = jax.nn.softmax(s, axis=-1)
        return jnp.einsum('bhqk,bqkhd->bqhd', p.astype(v.dtype), vs)

    out = lax.map(block, (jnp.arange(nb), qb, iqb, iwb))
    return out.transpose(1, 0, 2, 3, 4).reshape(B, S, H, D)


def swa_attention(q, k, v, sinks):
    B, S, HQ, D = q.shape
    HKV = k.shape[2]
    G = HQ // HKV
    W = SWA_WINDOW
    nb = S // W
    qb = q.reshape(B, nb, W, HKV, G, D)

    def band(t):
        tb = t.reshape(B, nb, W, HKV, D)
        prev = jnp.pad(tb, ((0, 0), (1, 0), (0, 0), (0, 0), (0, 0)))[:, :-1]
        return jnp.concatenate([prev, tb], axis=2)

    kb, vb = band(k), band(v)
    s = jnp.einsum('bnqhgd,bnkhd->bnhgqk', qb, kb).astype(jnp.float32) * D ** -0.5
    kpos = jnp.arange(2 * W)[None, :]
    rel = (jnp.arange(W)[:, None] + W) - kpos
    mask = (rel >= 0) & (rel < W)
    mask = mask[None] & ((jnp.arange(nb)[:, None, None] > 0) | (kpos >= W)[None])
    s = jnp.where(mask[None, :, None, None], s, NEG)
    sink = jnp.broadcast_to(sinks.reshape(HKV, G).astype(jnp.float32)[None, None, :, :, None, None],
                            s.shape[:-1] + (1,))
    p = jax.nn.softmax(jnp.concatenate([s, sink], axis=-1), axis=-1)[..., :-1]
    o = jnp.einsum('bnhgqk,bnkhd->bnqhgd', p.astype(v.dtype), vb)
    return o.reshape(B, S, HQ, D)


def even_layer(x, g_norm, w_in, b_f, g_qk_fox, g_qk_dsa, g_kidx, w_out, rope128, rope64, topk):
    B, S, _ = x.shape
    h = rmsnorm(x, g_norm) @ w_in
    aq, ak, av, ag, af, bq, bk, bv, bg, iq, ik, iw = split_cols(h, EVEN_SPLITS)
    aq = rmsnorm(aq.reshape(B, S, FOX_HEADS, HEAD_DIM), g_qk_fox[0])
    ak = rmsnorm(ak.reshape(B, S, FOX_HEADS, HEAD_DIM), g_qk_fox[1])
    log_f = jax.nn.log_sigmoid((af + b_f).astype(jnp.float32))
    ya = fox_attention(aq, ak, av.reshape(B, S, FOX_HEADS, HEAD_DIM), log_f)
    ya = ya.reshape(B, S, FOX_W) * jax.nn.silu(ag)
    bq = apply_rope(rmsnorm(bq.reshape(B, S, DSA_HEADS, HEAD_DIM), g_qk_dsa[0]), *rope128)
    bk = apply_rope(rmsnorm(bk.reshape(B, S, DSA_HEADS, HEAD_DIM), g_qk_dsa[1]), *rope128)
    iq = apply_rope(iq.reshape(B, S, IDX_HEADS, IDX_DIM), *rope64)
    ik = apply_rope(rmsnorm(ik, g_kidx)[:, :, None, :], *rope64)[:, :, 0]
    iw = iw * IDX_HEADS ** -0.5
    yb = dsa_attention(bq, bk, bv.reshape(B, S, DSA_HEADS, HEAD_DIM), iq, ik, iw, topk)
    yb = yb.reshape(B, S, DSA_W) * jax.nn.silu(bg)
    return x + jnp.concatenate([ya, yb], axis=-1) @ w_out


def odd_layer(x, g_norm, w_in, g_qk, sinks, w_out, rope64):
    B, S, _ = x.shape
    h = rmsnorm(x, g_norm) @ w_in
    q, k, v, g = split_cols(h, ODD_SPLITS)
    q = apply_rope(rmsnorm(q.reshape(B, S, SWA_HEADS, SWA_HEAD_DIM), g_qk[0]), *rope64)
    k = apply_rope(rmsnorm(k.reshape(B, S, SWA_KV_HEADS, SWA_HEAD_DIM), g_qk[1]), *rope64)
    v = v.reshape(B, S, SWA_KV_HEADS, SWA_HEAD_DIM)
    y = swa_attention(q, k, v, sinks).reshape(B, S, SWA_W) * jax.nn.silu(g)
    return x + y @ w_out


def setup_inputs(seed: int = 0) -> dict:
    key = jax.random.key(seed)
    ks = jax.random.split(key, 13)
    nrm = lambda k, shape, s: jax.random.normal(k, shape, jnp.float32) * s
    gain = lambda k, shape: 1.0 + 0.02 * jax.random.normal(k, shape, jnp.float32)
    return {
        "x": nrm(ks[0], (BATCH, SEQ, D_MODEL), 1.0),
        "norm_even": gain(ks[1], (N_EVEN, D_MODEL)),
        "w_in_even": nrm(ks[2], (N_EVEN, D_MODEL, EVEN_IN), D_MODEL ** -0.5),
        "b_f_even": nrm(ks[3], (N_EVEN, FOX_HEADS), 0.1),
        "g_qk_fox": gain(ks[4], (N_EVEN, 2, HEAD_DIM)),
        "g_qk_dsa": gain(ks[5], (N_EVEN, 2, HEAD_DIM)),
        "g_kidx": gain(ks[6], (N_EVEN, IDX_DIM)),
        "w_out_even": nrm(ks[7], (N_EVEN, FOX_W + DSA_W, D_MODEL), (FOX_W + DSA_W) ** -0.5),
        "norm_odd": gain(ks[8], (N_ODD, D_MODEL)),
        "w_in_odd": nrm(ks[9], (N_ODD, D_MODEL, ODD_IN), D_MODEL ** -0.5),
        "g_qk_swa": gain(ks[10], (N_ODD, 2, SWA_HEAD_DIM)),
        "sinks": nrm(ks[11], (N_ODD, SWA_HEADS), 0.5),
        "w_out_odd": nrm(ks[12], (N_ODD, SWA_W, D_MODEL), SWA_W ** -0.5),
    }


def reference(x, norm_even, w_in_even, b_f_even, g_qk_fox, g_qk_dsa, g_kidx, w_out_even,
              norm_odd, w_in_odd, g_qk_swa, sinks, w_out_odd):
    S = x.shape[1]
    topk = min(IDX_TOPK_MAX, S // 4)
    rope128 = rope_tables(S, HEAD_DIM)
    rope64 = rope_tables(S, IDX_DIM)
    for layer in range(DEPTH):
        j = layer // 2
        if layer % 2 == 0:
            x = even_layer(x, norm_even[j], w_in_even[j], b_f_even[j], g_qk_fox[j], g_qk_dsa[j],
                           g_kidx[j], w_out_even[j], rope128, rope64, topk)
        else:
            x = odd_layer(x, norm_odd[j], w_in_odd[j], g_qk_swa[j], sinks[j], w_out_odd[j], rope64)
    return x
```

```python
import functools

import jax
import jax.numpy as jnp
from jax import lax
from jax.experimental import pallas as pl
from jax.experimental.pallas import tpu as pltpu

F32 = jnp.float32
BF16 = jnp.bfloat16
I32 = jnp.int32

D_MODEL = 2048
HEAD_DIM = 128
FOX_HEADS = 8
DSA_HEADS = 8
IDX_HEADS = 16
IDX_DIM = 64
IDX_TOPK_MAX = 256
SWA_HEADS = 32
SWA_KV_HEADS = 4
SWA_HEAD_DIM = 64
SWA_WINDOW = 128
ROPE_THETA = 10000.0
EPS = 1e-6
NEG = -1e30

FOX_W = FOX_HEADS * HEAD_DIM
DSA_W = DSA_HEADS * HEAD_DIM
IDX_W = IDX_HEADS * IDX_DIM
SWA_W = SWA_HEADS * SWA_HEAD_DIM
SWA_KV_W = SWA_KV_HEADS * SWA_HEAD_DIM

LANES = 128
VMEM_LIMIT = 56 * 2 ** 20

SM_IK = 0
SM_IW = IDX_DIM
SM_CF = IDX_DIM + IDX_HEADS

INT_MIN = -2 ** 31
IDX_BITS = 12


def _params(sem):
    return pltpu.CompilerParams(dimension_semantics=sem, vmem_limit_bytes=VMEM_LIMIT)


def _dot(a, b):
    return jnp.dot(a, b, preferred_element_type=F32)


def _dot_nt(a, b):
    return lax.dot_general(a, b, (((1,), (1,)), ((), ())), preferred_element_type=F32)


def _lane_iota(shape):
    return lax.broadcasted_iota(I32, shape, 1)


def _rope(y, cos, sin, dim):
    if dim == LANES:
        rot = pltpu.roll(y, LANES // 2, 1)
    else:
        half = dim // 2
        first = (_lane_iota(y.shape) & half) == 0
        rot = jnp.where(first, pltpu.roll(y, LANES - half, 1), pltpu.roll(y, half, 1))
    return y * cos + rot * sin


def _head_rms(a, gain):
    return a * lax.rsqrt(jnp.mean(a * a, axis=-1, keepdims=True) + EPS) * gain


def _half_rms(a, gain):
    lo = _lane_iota(a.shape) < 64
    sq = a * a
    ms_lo = jnp.sum(jnp.where(lo, sq, 0.0), axis=-1, keepdims=True) * (1.0 / 64)
    ms_hi = jnp.sum(jnp.where(lo, 0.0, sq), axis=-1, keepdims=True) * (1.0 / 64)
    return a * lax.rsqrt(jnp.where(lo, ms_lo, ms_hi) + EPS) * gain


def _rmsnorm_kernel(x_ref, g_ref, o_ref):
    x = x_ref[...]
    y = x * lax.rsqrt(jnp.mean(x * x, axis=-1, keepdims=True) + EPS)
    o_ref[...] = (y * g_ref[...]).astype(o_ref.dtype)


def _rmsnorm(x2, g, tm=512):
    m, d = x2.shape
    return pl.pallas_call(
        _rmsnorm_kernel,
        grid=(m // tm,),
        in_specs=[pl.BlockSpec((tm, d), lambda i: (i, 0)), pl.BlockSpec((1, d), lambda i: (0, 0))],
        out_specs=pl.BlockSpec((tm, d), lambda i: (i, 0)),
        out_shape=jax.ShapeDtypeStruct((m, d), BF16),
        compiler_params=_params(("parallel",)),
        name="rmsnorm",
    )(x2, g.reshape(1, d))


EV_TN = 1024
EV_FQ, EV_FK, EV_DQ, EV_DK, EV_FV, EV_DV, EV_FG, EV_DG, EV_IQ = range(9)


def _even_proj_kernel(xn_ref, w_ref, gain_ref, c128_ref, s128_ref, c64_ref, s64_ref, o_ref, acc_ref):
    j = pl.program_id(1)
    acc_ref[...] = _dot(xn_ref[...], w_ref[...])

    def per_head(fn):
        for h in range(EV_TN // LANES):
            sl = slice(h * LANES, (h + 1) * LANES)
            o_ref[:, sl] = fn(acc_ref[:, sl]).astype(o_ref.dtype)

    @pl.when(j <= EV_FK)
    def _():
        gain = gain_ref[jnp.minimum(j, 3)]
        per_head(lambda a: _head_rms(a, gain))

    @pl.when((j == EV_DQ) | (j == EV_DK))
    def _():
        gain = gain_ref[jnp.clip(j, 0, 3)]
        cos, sin = c128_ref[...], s128_ref[...]
        per_head(lambda a: _rope(_head_rms(a, gain), cos, sin, HEAD_DIM))

    @pl.when((j == EV_FV) | (j == EV_DV))
    def _():
        per_head(lambda a: a)

    @pl.when((j == EV_FG) | (j == EV_DG))
    def _():
        per_head(lambda a: a * jax.nn.sigmoid(a))

    @pl.when(j == EV_IQ)
    def _():
        cos, sin = c64_ref[...], s64_ref[...]
        per_head(lambda a: _rope(a, cos, sin, IDX_DIM))


def _even_proj(xn, w, gains, c128, s128, c64, s64, seq, tm=1024):
    m, d = xn.shape
    n = w.shape[1]
    nrow = seq // tm
    tab = pl.BlockSpec((tm, LANES), lambda i, j: (i % nrow, 0))
    return pl.pallas_call(
        _even_proj_kernel,
        grid=(m // tm, n // EV_TN),
        in_specs=[pl.BlockSpec((tm, d), lambda i, j: (i, 0)),
                  pl.BlockSpec((d, EV_TN), lambda i, j: (0, j)),
                  pl.BlockSpec((4, 1, LANES), lambda i, j: (0, 0, 0)),
                  tab, tab, tab, tab],
        out_specs=pl.BlockSpec((tm, EV_TN), lambda i, j: (i, j)),
        out_shape=jax.ShapeDtypeStruct((m, n), BF16),
        scratch_shapes=[pltpu.VMEM((tm, EV_TN), F32)],
        compiler_params=_params(("parallel", "arbitrary")),
        name="even_proj",
    )(xn, w, gains, c128, s128, c64, s64)


def _even_small_kernel(xn_ref, w_ref, gk_ref, bf_ref, c64_ref, s64_ref, o_ref, ot_ref):
    h = _dot(xn_ref[...], w_ref[...])
    lane = _lane_iota(h.shape)
    is_ik = lane < SM_IW
    ms = jnp.sum(jnp.where(is_ik, h * h, 0.0), axis=-1, keepdims=True) * (1.0 / IDX_DIM)
    ik = _rope(h * lax.rsqrt(ms + EPS) * gk_ref[...], c64_ref[...], s64_ref[...], IDX_DIM)
    iw = h * (IDX_HEADS ** -0.5 * IDX_DIM ** -0.5)
    c = jax.nn.log_sigmoid(h + bf_ref[...])
    row = lax.broadcasted_iota(I32, h.shape, 0)
    d = 1
    while d < h.shape[0]:
        c = c + jnp.where(row >= d, pltpu.roll(c, d, 0), 0.0)
        d *= 2
    out = jnp.where(is_ik, ik, jnp.where(lane < SM_CF, iw, c))
    o_ref[...] = out
    ot_ref[0] = out.T


def _even_small(xn, w, gk, bf, c64, s64, batch, seq):
    d = xn.shape[1]
    vec = pl.BlockSpec((1, LANES), lambda b: (0, 0))
    tab = pl.BlockSpec((seq, LANES), lambda b: (0, 0))
    return pl.pallas_call(
        _even_small_kernel,
        grid=(batch,),
        in_specs=[pl.BlockSpec((seq, d), lambda b: (b, 0)),
                  pl.BlockSpec((d, LANES), lambda b: (0, 0)),
                  vec, vec, tab, tab],
        out_specs=[pl.BlockSpec((seq, LANES), lambda b: (b, 0)),
                   pl.BlockSpec((1, LANES, seq), lambda b: (b, 0, 0))],
        out_shape=[jax.ShapeDtypeStruct((batch * seq, LANES), F32),
                   jax.ShapeDtypeStruct((batch, LANES, seq), F32)],
        compiler_params=_params(("parallel",)),
        name="even_small",
    )(xn, w, gk, bf, c64, s64)


FOX_TQ = 256


def _fox_kernel(q_ref, k_ref, v_ref, g_ref, ccol_ref, crow_ref, o_ref):
    h = pl.program_id(1)
    i = pl.program_id(2)
    tq = FOX_TQ
    q = q_ref[...]
    lane = _lane_iota((tq, LANES))
    cq = jnp.sum(jnp.where(lane == SM_CF + h, ccol_ref[...], 0.0), axis=1, keepdims=True)
    scale = HEAD_DIM ** -0.5

    def step(j, carry, diagonal):
        m, l, acc = carry
        off = pl.multiple_of(j * tq, tq)
        kb = k_ref[pl.ds(off, tq), :]
        vb = v_ref[pl.ds(off, tq), :]
        ck = crow_ref[0, pl.ds(h, 1), pl.ds(off, tq)]
        s = _dot_nt(q, kb) * scale + (cq - ck)
        if diagonal:
            r = lax.broadcasted_iota(I32, (tq, tq), 0)
            c = lax.broadcasted_iota(I32, (tq, tq), 1)
            s = jnp.where(c <= r, s, NEG)
        m_new = jnp.maximum(m, jnp.max(s, axis=1, keepdims=True))
        p = jnp.exp(s - m_new)
        alpha = jnp.exp(m - m_new)
        l = alpha * l + jnp.sum(p, axis=1, keepdims=True)
        acc = alpha * acc + _dot(p.astype(BF16), vb)
        return m_new, l, acc

    init = (jnp.full((tq, 1), -jnp.inf, F32), jnp.zeros((tq, 1), F32), jnp.zeros((tq, LANES), F32))
    carry = lax.fori_loop(0, i, lambda j, c: step(j, c, False), init)
    _, l, acc = step(i, carry, True)
    o_ref[...] = (acc / l * g_ref[...].astype(F32)).astype(o_ref.dtype)


def _fox_attention(hmain, small, small_t, batch, seq):
    tq = FOX_TQ
    nq = seq // tq
    nh = FOX_W // LANES
    return pl.pallas_call(
        _fox_kernel,
        grid=(batch, FOX_HEADS, nq),
        in_specs=[pl.BlockSpec((tq, LANES), lambda b, h, i: (b * nq + i, EV_FQ * nh + h)),
                  pl.BlockSpec((seq, LANES), lambda b, h, i: (b, EV_FK * nh + h)),
                  pl.BlockSpec((seq, LANES), lambda b, h, i: (b, EV_FV * nh + h)),
                  pl.BlockSpec((tq, LANES), lambda b, h, i: (b * nq + i, EV_FG * nh + h)),
                  pl.BlockSpec((tq, LANES), lambda b, h, i: (b * nq + i, 0)),
                  pl.BlockSpec((1, 8, seq), lambda b, h, i: (b, SM_CF // 8, 0))],
        out_specs=pl.BlockSpec((tq, LANES), lambda b, h, i: (b * nq + i, h)),
        out_shape=jax.ShapeDtypeStruct((batch * seq, FOX_W), BF16),
        compiler_params=_params(("parallel", "parallel", "arbitrary")),
        name="fox_attention",
    )(hmain, hmain, hmain, hmain, small, small_t)


DSA_TQ = 256


def _sort_key(x):
    b = pltpu.bitcast(x, I32)
    return b ^ ((b >> 31) & 0x7FFFFFFF)


def _dsa_kernel(iq_ref, small_ref, ikt_ref, q_ref, k_ref, v_ref, g_ref, o_ref,
                key_ref, lim_ref, m_ref, l_ref, acc_ref, *, topk):
    i = pl.program_id(1)
    tq = DSA_TQ
    nchunk = i + 1
    row = i * tq + lax.broadcasted_iota(I32, (tq, tq), 0)
    col0 = lax.broadcasted_iota(I32, (tq, tq), 1)

    def chunk_off(c):
        return pl.multiple_of(c * tq, tq)

    def score_chunk(c, _):
        off = chunk_off(c)
        kt = ikt_ref[0, SM_IK:SM_IK + IDX_DIM, pl.ds(off, tq)].astype(BF16)
        acc = jnp.zeros((tq, tq), F32)
        for hh in range(IDX_HEADS):
            logit = _dot(iq_ref[:, hh * IDX_DIM:(hh + 1) * IDX_DIM], kt)
            acc = acc + small_ref[:, SM_IW + hh:SM_IW + hh + 1] * jnp.maximum(logit, 0.0)
        acc = jnp.where(off + col0 <= row, acc, NEG)
        key_ref[:, pl.ds(off, tq)] = _sort_key(acc)
        return 0

    lax.fori_loop(0, nchunk, score_chunk, 0)

    def count(pred):
        def body(c, part):
            off = chunk_off(c)
            for u in range(tq // LANES):
                kk = key_ref[:, pl.ds(off + u * LANES, LANES)]
                cc = off + u * LANES + _lane_iota((tq, LANES))
                part = part + jnp.where(pred(kk, cc), 1.0, 0.0)
            return part
        part = lax.fori_loop(0, nchunk, body, jnp.zeros((tq, LANES), F32))
        return jnp.sum(part, axis=1, keepdims=True)

    def thr_step(it, res):
        cand = res + jnp.left_shift(jnp.int32(1), 31 - it)
        cnt = count(lambda kk, cc: kk >= cand)
        return jnp.where(cnt >= topk, cand, res)

    thr = lax.fori_loop(0, 32, thr_step, jnp.full((tq, 1), INT_MIN, I32))
    n_gt = count(lambda kk, cc: kk > thr)
    n_ge = count(lambda kk, cc: kk >= thr)
    need = topk - n_gt

    lim_ref[...] = jnp.full((tq, 1), 2 ** IDX_BITS, I32)

    @pl.when(jnp.max(n_ge) > topk)
    def _():
        def lim_step(it, res):
            cand = res + jnp.left_shift(jnp.int32(1), IDX_BITS - 1 - it)
            cnt = count(lambda kk, cc: (kk == thr) & (cc < cand))
            return jnp.where(cnt <= need, cand, res)
        lim_ref[...] = lax.fori_loop(0, IDX_BITS, lim_step, jnp.zeros((tq, 1), I32))

    lim = lim_ref[...]

    m_ref[...] = jnp.full(m_ref.shape, -jnp.inf, F32)
    l_ref[...] = jnp.zeros(l_ref.shape, F32)
    acc_ref[...] = jnp.zeros(acc_ref.shape, F32)
    scale = HEAD_DIM ** -0.5

    def attn_chunk(c, _):
        off = chunk_off(c)
        kk = key_ref[:, pl.ds(off, tq)]
        cc = off + col0
        sel = ((kk > thr) | ((kk == thr) & (cc < lim))) & (cc <= row)
        for hh in range(DSA_HEADS):
            sl = slice(hh * HEAD_DIM, (hh + 1) * HEAD_DIM)
            s = jnp.where(sel, _dot_nt(q_ref[:, sl], k_ref[pl.ds(off, tq), sl]) * scale, NEG)
            m = m_ref[hh]
            m_new = jnp.maximum(m, jnp.max(s, axis=1, keepdims=True))
            p = jnp.exp(s - m_new)
            alpha = jnp.exp(m - m_new)
            l_ref[hh] = alpha * l_ref[hh] + jnp.sum(p, axis=1, keepdims=True)
            acc_ref[hh] = alpha * acc_ref[hh] + _dot(p.astype(BF16), v_ref[pl.ds(off, tq), sl])
            m_ref[hh] = m_new
        return 0

    lax.fori_loop(0, nchunk, attn_chunk, 0)
    for hh in range(DSA_HEADS):
        sl = slice(hh * HEAD_DIM, (hh + 1) * HEAD_DIM)
        o_ref[:, sl] = (acc_ref[hh] / l_ref[hh] * g_ref[:, sl].astype(F32)).astype(o_ref.dtype)


def _dsa_attention(hmain, small, small_t, batch, seq, topk):
    tq = DSA_TQ
    nq = seq // tq
    wide = lambda t: pl.BlockSpec((tq, DSA_W), lambda b, i: (b * nq + i, t))
    full = lambda t: pl.BlockSpec((seq, DSA_W), lambda b, i: (b, t))
    return pl.pallas_call(
        functools.partial(_dsa_kernel, topk=topk),
        grid=(batch, nq),
        in_specs=[wide(EV_IQ),
                  pl.BlockSpec((tq, LANES), lambda b, i: (b * nq + i, 0)),
                  pl.BlockSpec((1, LANES, seq), lambda b, i: (b, 0, 0)),
                  wide(EV_DQ), full(EV_DK), full(EV_DV), wide(EV_DG)],
        out_specs=pl.BlockSpec((tq, DSA_W), lambda b, i: (b * nq + i, 0)),
        out_shape=jax.ShapeDtypeStruct((batch * seq, DSA_W), BF16),
        scratch_shapes=[pltpu.VMEM((tq, seq), I32),
                        pltpu.VMEM((tq, 1), I32),
                        pltpu.VMEM((DSA_HEADS, tq, 1), F32),
                        pltpu.VMEM((DSA_HEADS, tq, 1), F32),
                        pltpu.VMEM((DSA_HEADS, tq, HEAD_DIM), F32)],
        compiler_params=_params(("parallel", "arbitrary")),
        name="dsa_attention",
    )(hmain, small, small_t, hmain, hmain, hmain, hmain)


def _out_proj_kernel(*refs):
    *y_refs, w_ref, x_ref, o_ref = refs
    acc = x_ref[...]
    off = 0
    for y_ref in y_refs:
        kdim = y_ref.shape[1]
        acc = acc + _dot(y_ref[...], w_ref[off:off + kdim, :])
        off += kdim
    o_ref[...] = acc


def _out_proj(ys, w, x2, tm=512):
    m, d = x2.shape
    return pl.pallas_call(
        _out_proj_kernel,
        grid=(m // tm,),
        in_specs=[pl.BlockSpec((tm, y.shape[1]), lambda i: (i, 0)) for y in ys]
                 + [pl.BlockSpec(w.shape, lambda i: (0, 0)), pl.BlockSpec((tm, d), lambda i: (i, 0))],
        out_specs=pl.BlockSpec((tm, d), lambda i: (i, 0)),
        out_shape=jax.ShapeDtypeStruct((m, d), F32),
        compiler_params=_params(("parallel",)),
        name="out_proj",
    )(*ys, w, x2)


OD_TN = 512
OD_NQ = SWA_W // OD_TN


def _odd_proj_kernel(xn_ref, w_ref, gain_ref, c64_ref, s64_ref, o_ref, acc_ref):
    j = pl.program_id(1)
    acc_ref[...] = _dot(xn_ref[...], w_ref[...])
    cos, sin = c64_ref[...], s64_ref[...]

    def per_slab(fn, lo=0, hi=OD_TN // LANES):
        for h in range(lo, hi):
            sl = slice(h * LANES, (h + 1) * LANES)
            o_ref[:, sl] = fn(acc_ref[:, sl]).astype(o_ref.dtype)

    @pl.when(j < OD_NQ)
    def _():
        gain = gain_ref[0]
        per_slab(lambda a: _rope(_half_rms(a, gain), cos, sin, SWA_HEAD_DIM))

    @pl.when((j >= OD_NQ) & (j < 2 * OD_NQ))
    def _():
        per_slab(lambda a: a * jax.nn.sigmoid(a))

    @pl.when(j == 2 * OD_NQ)
    def _():
        gain = gain_ref[1]
        nk = SWA_KV_W // LANES
        per_slab(lambda a: _rope(_half_rms(a, gain), cos, sin, SWA_HEAD_DIM), 0, nk)
        per_slab(lambda a: a, nk, OD_TN // LANES)


def _odd_proj(xn, w, gains, c64, s64, seq, tm=1024):
    m, d = xn.shape
    n = w.shape[1]
    nrow = seq // tm
    tab = pl.BlockSpec((tm, LANES), lambda i, j: (i % nrow, 0))
    return pl.pallas_call(
        _odd_proj_kernel,
        grid=(m // tm, n // OD_TN),
        in_specs=[pl.BlockSpec((tm, d), lambda i, j: (i, 0)),
                  pl.BlockSpec((d, OD_TN), lambda i, j: (0, j)),
                  pl.BlockSpec((2, 1, LANES), lambda i, j: (0, 0, 0)),
                  tab, tab],
        out_specs=pl.BlockSpec((tm, OD_TN), lambda i, j: (i, j)),
        out_shape=jax.ShapeDtypeStruct((m, n), BF16),
        scratch_shapes=[pltpu.VMEM((tm, OD_TN), F32)],
        compiler_params=_params(("parallel", "arbitrary")),
        name="odd_proj",
    )(xn, w, gains, c64, s64)


def _swa_kernel(sink_ref, q_ref, kvp_ref, kvc_ref, g_ref, o_ref):
    n = pl.program_id(1)
    w = SWA_WINDOW
    dh = SWA_HEAD_DIM
    group = SWA_HEADS // SWA_KV_HEADS
    r = lax.broadcasted_iota(I32, (w, w), 0)
    c = lax.broadcasted_iota(I32, (w, w), 1)
    mask_prev = (c > r) & (n > 0)
    mask_cur = c <= r
    scale = dh ** -0.5

    def head(h):
        kv = h // group
        ksl = slice(kv * dh, (kv + 1) * dh)
        vsl = slice(SWA_KV_W + kv * dh, SWA_KV_W + (kv + 1) * dh)
        q = q_ref[:, h * dh:(h + 1) * dh]
        sp = jnp.where(mask_prev, _dot_nt(q, kvp_ref[:, ksl]) * scale, NEG)
        sc = jnp.where(mask_cur, _dot_nt(q, kvc_ref[:, ksl]) * scale, NEG)
        sink = sink_ref[h]
        m = jnp.maximum(jnp.maximum(jnp.max(sp, axis=1, keepdims=True),
                                    jnp.max(sc, axis=1, keepdims=True)), sink)
        pp = jnp.exp(sp - m)
        pc = jnp.exp(sc - m)
        den = jnp.sum(pp, axis=1, keepdims=True) + jnp.sum(pc, axis=1, keepdims=True) + jnp.exp(sink - m)
        o = _dot(pp.astype(BF16), kvp_ref[:, vsl]) + _dot(pc.astype(BF16), kvc_ref[:, vsl])
        return o / den

    for hp in range(SWA_HEADS // 2):
        sl = slice(hp * LANES, (hp + 1) * LANES)
        o = jnp.concatenate([head(2 * hp), head(2 * hp + 1)], axis=1)
        o_ref[:, sl] = (o * g_ref[:, sl].astype(F32)).astype(o_ref.dtype)


def _swa_attention(hodd, sinks, batch, seq):
    w = SWA_WINDOW
    nb = seq // w
    kv_col = 2 * SWA_W // (2 * SWA_KV_W)
    return pl.pallas_call(
        _swa_kernel,
        grid=(batch, nb),
        in_specs=[pl.BlockSpec(memory_space=pltpu.SMEM),
                  pl.BlockSpec((w, SWA_W), lambda b, n: (b * nb + n, 0)),
                  pl.BlockSpec((w, 2 * SWA_KV_W), lambda b, n: (b * nb + jnp.maximum(n - 1, 0), kv_col)),
                  pl.BlockSpec((w, 2 * SWA_KV_W), lambda b, n: (b * nb + n, kv_col)),
                  pl.BlockSpec((w, SWA_W), lambda b, n: (b * nb + n, 1))],
        out_specs=pl.BlockSpec((w, SWA_W), lambda b, n: (b * nb + n, 0)),
        out_shape=jax.ShapeDtypeStruct((batch * seq, SWA_W), BF16),
        compiler_params=_params(("parallel", "arbitrary")),
        name="swa_attention",
    )(sinks, hodd, hodd, hodd, hodd)


def _rope_tables(seq, dim):
    inv = 1.0 / (ROPE_THETA ** (jnp.arange(0, dim, 2, dtype=F32) / dim))
    ang = jnp.arange(seq, dtype=F32)[:, None] * inv[None, :]
    cos, sin = jnp.cos(ang), jnp.sin(ang)
    reps = LANES // dim
    return (jnp.tile(jnp.concatenate([cos, cos], -1), (1, reps)),
            jnp.tile(jnp.concatenate([-sin, sin], -1), (1, reps)))


def _cols(w, start, size):
    return lax.slice_in_dim(w, start, start + size, axis=1)


def _even_layer(x2, batch, seq, g_norm, w_in, b_f, g_fox, g_dsa, g_kidx, w_out, tabs, topk):
    c128, s128, c64, s64 = tabs
    o = 0
    offs = {}
    for name, size in (("fq", FOX_W), ("fk", FOX_W), ("fv", FOX_W), ("fg", FOX_W), ("af", FOX_HEADS),
                       ("dq", DSA_W), ("dk", DSA_W), ("dv", DSA_W), ("dg", DSA_W),
                       ("iq", IDX_W), ("ik", IDX_DIM), ("iw", IDX_HEADS)):
        offs[name] = (o, size)
        o += size
    w_main = jnp.concatenate([_cols(w_in, *offs[n]) for n in
                              ("fq", "fk", "dq", "dk", "fv", "dv", "fg", "dg", "iq")], axis=1).astype(BF16)
    pad = LANES - (IDX_DIM + IDX_HEADS + FOX_HEADS)
    w_small = jnp.concatenate([_cols(w_in, *offs["ik"]), _cols(w_in, *offs["iw"]), _cols(w_in, *offs["af"]),
                               jnp.zeros((w_in.shape[0], pad), F32)], axis=1).astype(BF16)
    gains = jnp.stack([g_fox[0], g_fox[1], g_dsa[0], g_dsa[1]]).reshape(4, 1, LANES)
    gk = jnp.concatenate([g_kidx, jnp.zeros((LANES - IDX_DIM,), F32)]).reshape(1, LANES)
    bf = jnp.concatenate([jnp.zeros((SM_CF,), F32), b_f, jnp.zeros((pad,), F32)]).reshape(1, LANES)

    xn = _rmsnorm(x2, g_norm)
    hmain = _even_proj(xn, w_main, gains, c128, s128, c64, s64, seq)
    small, small_t = _even_small(xn, w_small, gk, bf, c64, s64, batch, seq)
    ya = _fox_attention(hmain, small, small_t, batch, seq)
    yb = _dsa_attention(hmain, small, small_t, batch, seq, topk)
    return _out_proj([ya, yb], w_out.astype(BF16), x2)


def _odd_layer(x2, batch, seq, g_norm, w_in, g_qk, sinks, w_out, tabs):
    _, _, c64, s64 = tabs
    q_w, k_w, v_w, g_w = (_cols(w_in, 0, SWA_W), _cols(w_in, SWA_W, SWA_KV_W),
                          _cols(w_in, SWA_W + SWA_KV_W, SWA_KV_W), _cols(w_in, SWA_W + 2 * SWA_KV_W, SWA_W))
    w_all = jnp.concatenate([q_w, g_w, k_w, v_w], axis=1).astype(BF16)
    gains = jnp.tile(g_qk, (1, LANES // SWA_HEAD_DIM)).reshape(2, 1, LANES)
    xn = _rmsnorm(x2, g_norm)
    hodd = _odd_proj(xn, w_all, gains, c64, s64, seq)
    y = _swa_attention(hodd, sinks, batch, seq)
    return _out_proj([y], w_out.astype(BF16), x2)


def kernel(x, norm_even, w_in_even, b_f_even, g_qk_fox, g_qk_dsa, g_kidx, w_out_even,
           norm_odd, w_in_odd, g_qk_swa, sinks, w_out_odd):
    batch, seq, d = x.shape
    depth = norm_even.shape[0] + norm_odd.shape[0]
    topk = min(IDX_TOPK_MAX, seq // 4)
    tabs = _rope_tables(seq, HEAD_DIM) + _rope_tables(seq, IDX_DIM)
    x2 = x.reshape(batch * seq, d)
    for layer in range(depth):
        j = layer // 2
        if layer % 2 == 0:
            x2 = _even_layer(x2, batch, seq, norm_even[j], w_in_even[j], b_f_even[j], g_qk_fox[j],
                             g_qk_dsa[j], g_kidx[j], w_out_even[j], tabs, topk)
        else:
            x2 = _odd_layer(x2, batch, seq, norm_odd[j], w_in_odd[j], g_qk_swa[j], sinks[j],
                            w_out_odd[j], tabs)
    return x2.reshape(batch, seq, d)
```

```python
import functools

import jax
import jax.numpy as jnp
from jax import lax
from jax.experimental import pallas as pl
from jax.experimental.pallas import tpu as pltpu

F32 = jnp.float32
BF16 = jnp.bfloat16
I32 = jnp.int32

D_MODEL = 2048
HEAD_DIM = 128
FOX_HEADS = 8
DSA_HEADS = 8
IDX_HEADS = 16
IDX_DIM = 64
IDX_TOPK_MAX = 256
SWA_HEADS = 32
SWA_KV_HEADS = 4
SWA_HEAD_DIM = 64
SWA_WINDOW = 128
ROPE_THETA = 10000.0
EPS = 1e-6
NEG = -1e30

FOX_W = FOX_HEADS * HEAD_DIM
DSA_W = DSA_HEADS * HEAD_DIM
IDX_W = IDX_HEADS * IDX_DIM
SWA_W = SWA_HEADS * SWA_HEAD_DIM
SWA_KV_W = SWA_KV_HEADS * SWA_HEAD_DIM

LANES = 128
VMEM_LIMIT = 56 * 2 ** 20

SM_IK = 0
SM_IW = IDX_DIM
SM_CF = IDX_DIM + IDX_HEADS

LOG2E = 1.4426950408889634
INT_MIN = -2 ** 31
IDX_BITS = 12


def _params(sem):
    return pltpu.CompilerParams(dimension_semantics=sem, vmem_limit_bytes=VMEM_LIMIT)


def _dot(a, b):
    return jnp.dot(a, b, preferred_element_type=F32)


def _dot_nt(a, b):
    return lax.dot_general(a, b, (((1,), (1,)), ((), ())), preferred_element_type=F32)


def _lane_iota(shape):
    return lax.broadcasted_iota(I32, shape, 1)


def _rope(y, cos, sin, dim):
    if dim == LANES:
        rot = pltpu.roll(y, LANES // 2, 1)
    else:
        half = dim // 2
        first = (_lane_iota(y.shape) & half) == 0
        rot = jnp.where(first, pltpu.roll(y, LANES - half, 1), pltpu.roll(y, half, 1))
    return y * cos + rot * sin


def _head_rms(a, gain):
    return a * lax.rsqrt(jnp.mean(a * a, axis=-1, keepdims=True) + EPS) * gain


def _half_rms(a, gain):
    lo = _lane_iota(a.shape) < 64
    sq = a * a
    ms_lo = jnp.sum(jnp.where(lo, sq, 0.0), axis=-1, keepdims=True) * (1.0 / 64)
    ms_hi = jnp.sum(jnp.where(lo, 0.0, sq), axis=-1, keepdims=True) * (1.0 / 64)
    return a * lax.rsqrt(jnp.where(lo, ms_lo, ms_hi) + EPS) * gain


def _rmsnorm_kernel(x_ref, g_ref, o_ref):
    x = x_ref[...]
    y = x * lax.rsqrt(jnp.mean(x * x, axis=-1, keepdims=True) + EPS)
    o_ref[...] = (y * g_ref[...]).astype(o_ref.dtype)


def _rmsnorm(x2, g, tm=512):
    m, d = x2.shape
    return pl.pallas_call(
        _rmsnorm_kernel,
        grid=(m // tm,),
        in_specs=[pl.BlockSpec((tm, d), lambda i: (i, 0)), pl.BlockSpec((1, d), lambda i: (0, 0))],
        out_specs=pl.BlockSpec((tm, d), lambda i: (i, 0)),
        out_shape=jax.ShapeDtypeStruct((m, d), BF16),
        compiler_params=_params(("parallel",)),
        name="rmsnorm",
    )(x2, g.reshape(1, d))


EV_TN = 1024
EV_FQ, EV_FK, EV_DQ, EV_DK, EV_FV, EV_DV, EV_FG, EV_DG, EV_IQ = range(9)


def _even_proj_kernel(xn_ref, w_ref, gain_ref, c128_ref, s128_ref, c64_ref, s64_ref, o_ref, acc_ref):
    j = pl.program_id(1)
    acc_ref[...] = _dot(xn_ref[...], w_ref[...])

    def per_head(fn):
        for h in range(EV_TN // LANES):
            sl = slice(h * LANES, (h + 1) * LANES)
            o_ref[:, sl] = fn(acc_ref[:, sl]).astype(o_ref.dtype)

    @pl.when(j <= EV_FK)
    def _():
        gain = gain_ref[jnp.minimum(j, 3)]
        per_head(lambda a: _head_rms(a, gain))

    @pl.when((j == EV_DQ) | (j == EV_DK))
    def _():
        gain = gain_ref[jnp.clip(j, 0, 3)]
        cos, sin = c128_ref[...], s128_ref[...]
        per_head(lambda a: _rope(_head_rms(a, gain), cos, sin, HEAD_DIM))

    @pl.when((j == EV_FV) | (j == EV_DV))
    def _():
        per_head(lambda a: a)

    @pl.when((j == EV_FG) | (j == EV_DG))
    def _():
        per_head(lambda a: a * jax.nn.sigmoid(a))

    @pl.when(j == EV_IQ)
    def _():
        cos, sin = c64_ref[...], s64_ref[...]
        per_head(lambda a: _rope(a, cos, sin, IDX_DIM))


def _even_proj(xn, w, gains, c128, s128, c64, s64, seq, tm=1024):
    m, d = xn.shape
    n = w.shape[1]
    nrow = seq // tm
    tab = pl.BlockSpec((tm, LANES), lambda i, j: (i % nrow, 0))
    return pl.pallas_call(
        _even_proj_kernel,
        grid=(m // tm, n // EV_TN),
        in_specs=[pl.BlockSpec((tm, d), lambda i, j: (i, 0)),
                  pl.BlockSpec((d, EV_TN), lambda i, j: (0, j)),
                  pl.BlockSpec((4, 1, LANES), lambda i, j: (0, 0, 0)),
                  tab, tab, tab, tab],
        out_specs=pl.BlockSpec((tm, EV_TN), lambda i, j: (i, j)),
        out_shape=jax.ShapeDtypeStruct((m, n), BF16),
        scratch_shapes=[pltpu.VMEM((tm, EV_TN), F32)],
        compiler_params=_params(("parallel", "arbitrary")),
        name="even_proj",
    )(xn, w, gains, c128, s128, c64, s64)


def _even_small_kernel(xn_ref, w_ref, gk_ref, bf_ref, c64_ref, s64_ref, o_ref, ot_ref):
    h = _dot(xn_ref[...], w_ref[...])
    lane = _lane_iota(h.shape)
    is_ik = lane < SM_IW
    ms = jnp.sum(jnp.where(is_ik, h * h, 0.0), axis=-1, keepdims=True) * (1.0 / IDX_DIM)
    ik = _rope(h * lax.rsqrt(ms + EPS) * gk_ref[...], c64_ref[...], s64_ref[...], IDX_DIM)
    iw = h * (IDX_HEADS ** -0.5 * IDX_DIM ** -0.5)
    c = jax.nn.log_sigmoid(h + bf_ref[...])
    row = lax.broadcasted_iota(I32, h.shape, 0)
    d = 1
    while d < h.shape[0]:
        c = c + jnp.where(row >= d, pltpu.roll(c, d, 0), 0.0)
        d *= 2
    out = jnp.where(is_ik, ik, jnp.where(lane < SM_CF, iw, c))
    o_ref[...] = out
    ot_ref[0] = out.T


def _even_small(xn, w, gk, bf, c64, s64, batch, seq):
    d = xn.shape[1]
    vec = pl.BlockSpec((1, LANES), lambda b: (0, 0))
    tab = pl.BlockSpec((seq, LANES), lambda b: (0, 0))
    return pl.pallas_call(
        _even_small_kernel,
        grid=(batch,),
        in_specs=[pl.BlockSpec((seq, d), lambda b: (b, 0)),
                  pl.BlockSpec((d, LANES), lambda b: (0, 0)),
                  vec, vec, tab, tab],
        out_specs=[pl.BlockSpec((seq, LANES), lambda b: (b, 0)),
                   pl.BlockSpec((1, LANES, seq), lambda b: (b, 0, 0))],
        out_shape=[jax.ShapeDtypeStruct((batch * seq, LANES), F32),
                   jax.ShapeDtypeStruct((batch, LANES, seq), F32)],
        compiler_params=_params(("parallel",)),
        name="even_small",
    )(xn, w, gk, bf, c64, s64)


FOX_TQ = 256


def _fox_kernel(q_ref, k_ref, v_ref, g_ref, small_ref, o_ref, vt_ref, ckb_ref, *, nq):
    h = pl.program_id(1)
    i = pl.program_id(2)
    tq = FOX_TQ

    @pl.when(i == 0)
    def _():
        vt_ref[...] = v_ref[...].astype(F32).T.astype(BF16)
        sm = small_ref[...]
        ck = jnp.sum(jnp.where(_lane_iota(sm.shape) == SM_CF + h, sm, 0.0), axis=1, keepdims=True)
        ckb_ref[...] = jnp.broadcast_to(ck * LOG2E, ckb_ref.shape)

    q = q_ref[...]
    c1 = HEAD_DIM ** -0.5 * LOG2E

    def qk(j):
        return _dot_nt(k_ref[pl.ds(j * tq, tq), :], q)

    def step(j, s, carry, diagonal):
        m, l, acc = carry
        off = j * tq
        ck = ckb_ref[pl.ds(off, tq), :]
        z = s * c1 - jnp.concatenate([ck, ck], axis=1)
        if diagonal:
            key = lax.broadcasted_iota(I32, (tq, tq), 0)
            qry = lax.broadcasted_iota(I32, (tq, tq), 1)
            z = jnp.where(key <= qry, z, NEG)
        m_new = jnp.maximum(m, jnp.max(z, axis=0, keepdims=True))
        p = jnp.exp2(z - m_new)
        alpha = jnp.exp2(m - m_new)
        l = alpha * l + jnp.sum(p, axis=0, keepdims=True)
        acc = alpha * acc + _dot(vt_ref[:, pl.ds(off, tq)], p.astype(BF16))
        return m_new, l, acc

    init = (jnp.full((1, tq), -jnp.inf, F32), jnp.zeros((1, tq), F32), jnp.zeros((HEAD_DIM, tq), F32))
    for i_static in range(nq):
        @pl.when(i == i_static)
        def _(i_static=i_static):
            carry = init
            s_next = qk(0)
            for j in range(i_static + 1):
                s = s_next
                if j < i_static:
                    s_next = qk(j + 1)
                carry = step(j, s, carry, j == i_static)
            _, l, acc = carry
            o_ref[...] = ((acc * (1.0 / l)).T * g_ref[...].astype(F32)).astype(o_ref.dtype)


def _fox_attention(hmain, small, batch, seq):
    tq = FOX_TQ
    nq = seq // tq
    nh = FOX_W // LANES
    return pl.pallas_call(
        functools.partial(_fox_kernel, nq=nq),
        grid=(batch, FOX_HEADS, nq),
        in_specs=[pl.BlockSpec((tq, LANES), lambda b, h, i: (b * nq + i, EV_FQ * nh + h)),
                  pl.BlockSpec((seq, LANES), lambda b, h, i: (b, EV_FK * nh + h)),
                  pl.BlockSpec((seq, LANES), lambda b, h, i: (b, EV_FV * nh + h)),
                  pl.BlockSpec((tq, LANES), lambda b, h, i: (b * nq + i, EV_FG * nh + h)),
                  pl.BlockSpec((seq, LANES), lambda b, h, i: (b, 0))],
        out_specs=pl.BlockSpec((tq, LANES), lambda b, h, i: (b * nq + i, h)),
        out_shape=jax.ShapeDtypeStruct((batch * seq, FOX_W), BF16),
        scratch_shapes=[pltpu.VMEM((HEAD_DIM, seq), BF16), pltpu.VMEM((seq, LANES), F32)],
        compiler_params=_params(("parallel", "parallel", "arbitrary")),
        name="fox_attention",
    )(hmain, hmain, hmain, hmain, small)


DSA_TQ = 256


def _sort_key(x):
    b = pltpu.bitcast(x, I32)
    return b ^ ((b >> 31) & 0x7FFFFFFF)


def _dsa_kernel(iq_ref, small_ref, smallt_ref, q_ref, k_ref, v_ref, g_ref, o_ref,
                key_ref, vt_ref, lim_ref, m_ref, l_ref, acc_ref, *, topk):
    i = pl.program_id(1)
    tq = DSA_TQ
    nchunk = i + 1
    half = tq // 2
    key0 = lax.broadcasted_iota(I32, (tq, tq), 0)
    qry = i * tq + lax.broadcasted_iota(I32, (tq, tq), 1)

    @pl.when(i == 0)
    def _():
        for hh in range(DSA_HEADS):
            sl = slice(hh * HEAD_DIM, (hh + 1) * HEAD_DIM)
            vt_ref[sl, :] = v_ref[:, sl].astype(F32).T.astype(BF16)

    def chunk_off(c):
        return pl.multiple_of(c * tq, tq)

    def score_chunk(c, _):
        off = chunk_off(c)
        ikc = small_ref[pl.ds(off, tq), SM_IK:SM_IK + IDX_DIM].astype(BF16)

        def logits(hh):
            return _dot_nt(ikc, iq_ref[:, hh * IDX_DIM:(hh + 1) * IDX_DIM])

        acc = jnp.zeros((tq, tq), F32)
        nxt = logits(0)
        for hh in range(IDX_HEADS):
            cur = nxt
            if hh + 1 < IDX_HEADS:
                nxt = logits(hh + 1)
            w = smallt_ref[0, SM_IW + hh:SM_IW + hh + 1, pl.ds(pl.multiple_of(i * tq, tq), tq)]
            acc = acc + w * jnp.maximum(cur, 0.0)
        key_ref[pl.ds(off, tq), :] = _sort_key(jnp.where(off + key0 <= qry, acc, NEG))
        return 0

    lax.fori_loop(0, nchunk, score_chunk, 0)

    def count(pred):
        def body(c, tot):
            off = chunk_off(c)
            hit = jnp.where(pred(key_ref[pl.ds(off, tq), :], off + key0), 1.0, 0.0)
            return tot + jnp.sum(hit, axis=0, keepdims=True)
        return lax.fori_loop(0, nchunk, body, jnp.zeros((1, tq), F32))

    def thr_step(it, res):
        cand = res + jnp.left_shift(jnp.int32(1), 31 - it)
        cnt = count(lambda kk, ki: kk >= cand)
        return jnp.where(cnt >= topk, cand, res)

    thr = lax.fori_loop(0, 32, thr_step, jnp.full((1, tq), INT_MIN, I32))
    n_gt = count(lambda kk, ki: kk > thr)
    n_ge = count(lambda kk, ki: kk >= thr)
    need = topk - n_gt

    lim_ref[...] = jnp.full((1, tq), 2 ** IDX_BITS, I32)

    @pl.when(jnp.max(n_ge) > topk)
    def _():
        def lim_step(it, res):
            cand = res + jnp.left_shift(jnp.int32(1), IDX_BITS - 1 - it)
            cnt = count(lambda kk, ki: (kk == thr) & (ki < cand))
            return jnp.where(cnt <= need, cand, res)
        lim_ref[...] = lax.fori_loop(0, IDX_BITS, lim_step, jnp.zeros((1, tq), I32))

    lim = lim_ref[...]

    m_ref[...] = jnp.full(m_ref.shape, -jnp.inf, F32)
    l_ref[...] = jnp.zeros(l_ref.shape, F32)
    acc_ref[...] = jnp.zeros(acc_ref.shape, F32)
    c1 = HEAD_DIM ** -0.5 * LOG2E

    def attn_chunk(c, _):
        off = chunk_off(c)
        kk = key_ref[pl.ds(off, tq), :]
        ki = off + key0
        sel = ((kk > thr) | ((kk == thr) & (ki < lim))) & (ki <= qry)
        bias = jnp.where(sel, 0.0, NEG)

        def qk(hh):
            sl = slice(hh * HEAD_DIM, (hh + 1) * HEAD_DIM)
            return _dot_nt(k_ref[pl.ds(off, tq), sl], q_ref[:, sl])

        nxt = qk(0)
        for hh in range(DSA_HEADS):
            cur = nxt
            if hh + 1 < DSA_HEADS:
                nxt = qk(hh + 1)
            z = cur * c1 + bias
            m = m_ref[hh:hh + 1, :]
            m_new = jnp.maximum(m, jnp.max(z, axis=0, keepdims=True))
            p = jnp.exp2(z - m_new)
            alpha = jnp.exp2(m - m_new)
            l_ref[hh:hh + 1, :] = alpha * l_ref[hh:hh + 1, :] + jnp.sum(p, axis=0, keepdims=True)
            pv = _dot(vt_ref[hh * HEAD_DIM:(hh + 1) * HEAD_DIM, pl.ds(off, tq)], p.astype(BF16))
            acc_ref[hh] = alpha * acc_ref[hh] + pv
            m_ref[hh:hh + 1, :] = m_new
        return 0

    lax.fori_loop(0, nchunk, attn_chunk, 0)
    for hh in range(DSA_HEADS):
        sl = slice(hh * HEAD_DIM, (hh + 1) * HEAD_DIM)
        out = (acc_ref[hh] * (1.0 / l_ref[hh:hh + 1, :])).T
        o_ref[:, sl] = (out * g_ref[:, sl].astype(F32)).astype(o_ref.dtype)


def _dsa_attention(hmain, small, small_t, batch, seq, topk):
    tq = DSA_TQ
    nq = seq // tq
    wide = lambda t: pl.BlockSpec((tq, DSA_W), lambda b, i: (b * nq + i, t))
    full = lambda t: pl.BlockSpec((seq, DSA_W), lambda b, i: (b, t))
    return pl.pallas_call(
        functools.partial(_dsa_kernel, topk=topk),
        grid=(batch, nq),
        in_specs=[wide(EV_IQ),
                  pl.BlockSpec((seq, LANES), lambda b, i: (b, 0)),
                  pl.BlockSpec((1, LANES, seq), lambda b, i: (b, 0, 0)),
                  wide(EV_DQ), full(EV_DK), full(EV_DV), wide(EV_DG)],
        out_specs=pl.BlockSpec((tq, DSA_W), lambda b, i: (b * nq + i, 0)),
        out_shape=jax.ShapeDtypeStruct((batch * seq, DSA_W), BF16),
        scratch_shapes=[pltpu.VMEM((seq, tq), I32),
                        pltpu.VMEM((DSA_W, seq), BF16),
                        pltpu.VMEM((1, tq), I32),
                        pltpu.VMEM((DSA_HEADS, tq), F32),
                        pltpu.VMEM((DSA_HEADS, tq), F32),
                        pltpu.VMEM((DSA_HEADS, HEAD_DIM, tq), F32)],
        compiler_params=_params(("parallel", "arbitrary")),
        name="dsa_attention",
    )(hmain, small, small_t, hmain, hmain, hmain, hmain)


def _out_proj_kernel(*refs):
    *y_refs, w_ref, x_ref, o_ref = refs
    acc = x_ref[...]
    off = 0
    for y_ref in y_refs:
        kdim = y_ref.shape[1]
        acc = acc + _dot(y_ref[...], w_ref[off:off + kdim, :])
        off += kdim
    o_ref[...] = acc


def _out_proj(ys, w, x2, tm=512):
    m, d = x2.shape
    return pl.pallas_call(
        _out_proj_kernel,
        grid=(m // tm,),
        in_specs=[pl.BlockSpec((tm, y.shape[1]), lambda i: (i, 0)) for y in ys]
                 + [pl.BlockSpec(w.shape, lambda i: (0, 0)), pl.BlockSpec((tm, d), lambda i: (i, 0))],
        out_specs=pl.BlockSpec((tm, d), lambda i: (i, 0)),
        out_shape=jax.ShapeDtypeStruct((m, d), F32),
        compiler_params=_params(("parallel",)),
        name="out_proj",
    )(*ys, w, x2)


OD_TN = 512
OD_NQ = SWA_W // OD_TN


def _odd_proj_kernel(xn_ref, w_ref, gain_ref, c64_ref, s64_ref, o_ref, acc_ref):
    j = pl.program_id(1)
    acc_ref[...] = _dot(xn_ref[...], w_ref[...])
    cos, sin = c64_ref[...], s64_ref[...]

    def per_slab(fn, lo=0, hi=OD_TN // LANES):
        for h in range(lo, hi):
            sl = slice(h * LANES, (h + 1) * LANES)
            o_ref[:, sl] = fn(acc_ref[:, sl]).astype(o_ref.dtype)

    @pl.when(j < OD_NQ)
    def _():
        gain = gain_ref[0]
        per_slab(lambda a: _rope(_half_rms(a, gain), cos, sin, SWA_HEAD_DIM))

    @pl.when((j >= OD_NQ) & (j < 2 * OD_NQ))
    def _():
        per_slab(lambda a: a * jax.nn.sigmoid(a))

    @pl.when(j == 2 * OD_NQ)
    def _():
        gain = gain_ref[1]
        nk = SWA_KV_W // LANES
        per_slab(lambda a: _rope(_half_rms(a, gain), cos, sin, SWA_HEAD_DIM), 0, nk)
        per_slab(lambda a: a, nk, OD_TN // LANES)


def _odd_proj(xn, w, gains, c64, s64, seq, tm=1024):
    m, d = xn.shape
    n = w.shape[1]
    nrow = seq // tm
    tab = pl.BlockSpec((tm, LANES), lambda i, j: (i % nrow, 0))
    return pl.pallas_call(
        _odd_proj_kernel,
        grid=(m // tm, n // OD_TN),
        in_specs=[pl.BlockSpec((tm, d), lambda i, j: (i, 0)),
                  pl.BlockSpec((d, OD_TN), lambda i, j: (0, j)),
                  pl.BlockSpec((2, 1, LANES), lambda i, j: (0, 0, 0)),
                  tab, tab],
        out_specs=pl.BlockSpec((tm, OD_TN), lambda i, j: (i, j)),
        out_shape=jax.ShapeDtypeStruct((m, n), BF16),
        scratch_shapes=[pltpu.VMEM((tm, OD_TN), F32)],
        compiler_params=_params(("parallel", "arbitrary")),
        name="odd_proj",
    )(xn, w, gains, c64, s64)


def _swa_kernel(sink_ref, q_ref, kvp_ref, kvc_ref, g_ref, o_ref):
    n = pl.program_id(1)
    w = SWA_WINDOW
    dh = SWA_HEAD_DIM
    group = SWA_HEADS // SWA_KV_HEADS
    kb = lax.broadcasted_iota(I32, (2 * w, 2 * w), 0)
    qi = lax.broadcasted_iota(I32, (2 * w, 2 * w), 1) & (w - 1)
    valid = (kb > qi) & (kb <= qi + w) & ((n > 0) | (kb >= w))
    bias = jnp.where(valid, 0.0, NEG)
    first = lax.broadcasted_iota(I32, (1, 2 * w), 1) < w
    c1 = dh ** -0.5 * LOG2E

    def band(col):
        sl = slice(col, col + dh)
        return jnp.concatenate([kvp_ref[:, sl], kvc_ref[:, sl]], axis=0)

    def qk(hp, kband):
        qpair = jnp.concatenate([q_ref[:, (2 * hp) * dh:(2 * hp + 1) * dh],
                                 q_ref[:, (2 * hp + 1) * dh:(2 * hp + 2) * dh]], axis=0)
        return _dot_nt(kband, qpair)

    pairs = group // 2
    for kv in range(SWA_KV_HEADS):
        kband = band(kv * dh)
        vt = band(SWA_KV_W + kv * dh).astype(F32).T.astype(BF16)
        nxt = qk(kv * pairs, kband)
        for t in range(pairs):
            hp = kv * pairs + t
            cur = nxt
            if t + 1 < pairs:
                nxt = qk(hp + 1, kband)
            z = cur * c1 + bias
            sink = jnp.where(first, sink_ref[2 * hp], sink_ref[2 * hp + 1]) * LOG2E
            m = jnp.maximum(jnp.max(z, axis=0, keepdims=True), sink)
            p = jnp.exp2(z - m)
            den = jnp.sum(p, axis=0, keepdims=True) + jnp.exp2(sink - m)
            ot = (_dot(vt, p.astype(BF16)) * (1.0 / den)).T
            o = jnp.concatenate([ot[:w], ot[w:]], axis=1)
            sl = slice(hp * LANES, (hp + 1) * LANES)
            o_ref[:, sl] = (o * g_ref[:, sl].astype(F32)).astype(o_ref.dtype)


def _swa_attention(hodd, sinks, batch, seq):
    w = SWA_WINDOW
    nb = seq // w
    kv_col = 2 * SWA_W // (2 * SWA_KV_W)
    return pl.pallas_call(
        _swa_kernel,
        grid=(batch, nb),
        in_specs=[pl.BlockSpec(memory_space=pltpu.SMEM),
                  pl.BlockSpec((w, SWA_W), lambda b, n: (b * nb + n, 0)),
                  pl.BlockSpec((w, 2 * SWA_KV_W), lambda b, n: (b * nb + jnp.maximum(n - 1, 0), kv_col)),
                  pl.BlockSpec((w, 2 * SWA_KV_W), lambda b, n: (b * nb + n, kv_col)),
                  pl.BlockSpec((w, SWA_W), lambda b, n: (b * nb + n, 1))],
        out_specs=pl.BlockSpec((w, SWA_W), lambda b, n: (b * nb + n, 0)),
        out_shape=jax.ShapeDtypeStruct((batch * seq, SWA_W), BF16),
        compiler_params=_params(("parallel", "arbitrary")),
        name="swa_attention",
    )(sinks, hodd, hodd, hodd, hodd)


def _rope_tables(seq, dim):
    inv = 1.0 / (ROPE_THETA ** (jnp.arange(0, dim, 2, dtype=F32) / dim))
    ang = jnp.arange(seq, dtype=F32)[:, None] * inv[None, :]
    cos, sin = jnp.cos(ang), jnp.sin(ang)
    reps = LANES // dim
    return (jnp.tile(jnp.concatenate([cos, cos], -1), (1, reps)),
            jnp.tile(jnp.concatenate([-sin, sin], -1), (1, reps)))


def _cols(w, start, size):
    return lax.slice_in_dim(w, start, start + size, axis=1)


def _even_layer(x2, batch, seq, g_norm, w_in, b_f, g_fox, g_dsa, g_kidx, w_out, tabs, topk):
    c128, s128, c64, s64 = tabs
    o = 0
    offs = {}
    for name, size in (("fq", FOX_W), ("fk", FOX_W), ("fv", FOX_W), ("fg", FOX_W), ("af", FOX_HEADS),
                       ("dq", DSA_W), ("dk", DSA_W), ("dv", DSA_W), ("dg", DSA_W),
                       ("iq", IDX_W), ("ik", IDX_DIM), ("iw", IDX_HEADS)):
        offs[name] = (o, size)
        o += size
    w_main = jnp.concatenate([_cols(w_in, *offs[n]) for n in
                              ("fq", "fk", "dq", "dk", "fv", "dv", "fg", "dg", "iq")], axis=1).astype(BF16)
    pad = LANES - (IDX_DIM + IDX_HEADS + FOX_HEADS)
    w_small = jnp.concatenate([_cols(w_in, *offs["ik"]), _cols(w_in, *offs["iw"]), _cols(w_in, *offs["af"]),
                               jnp.zeros((w_in.shape[0], pad), F32)], axis=1).astype(BF16)
    gains = jnp.stack([g_fox[0], g_fox[1], g_dsa[0], g_dsa[1]]).reshape(4, 1, LANES)
    gk = jnp.concatenate([g_kidx, jnp.zeros((LANES - IDX_DIM,), F32)]).reshape(1, LANES)
    bf = jnp.concatenate([jnp.zeros((SM_CF,), F32), b_f, jnp.zeros((pad,), F32)]).reshape(1, LANES)

    xn = _rmsnorm(x2, g_norm)
    hmain = _even_proj(xn, w_main, gains, c128, s128, c64, s64, seq)
    small, small_t = _even_small(xn, w_small, gk, bf, c64, s64, batch, seq)
    ya = _fox_attention(hmain, small, batch, seq)
    yb = _dsa_attention(hmain, small, small_t, batch, seq, topk)
    return _out_proj([ya, yb], w_out.astype(BF16), x2)


def _odd_layer(x2, batch, seq, g_norm, w_in, g_qk, sinks, w_out, tabs):
    _, _, c64, s64 = tabs
    q_w, k_w, v_w, g_w = (_cols(w_in, 0, SWA_W), _cols(w_in, SWA_W, SWA_KV_W),
                          _cols(w_in, SWA_W + SWA_KV_W, SWA_KV_W), _cols(w_in, SWA_W + 2 * SWA_KV_W, SWA_W))
    w_all = jnp.concatenate([q_w, g_w, k_w, v_w], axis=1).astype(BF16)
    gains = jnp.tile(g_qk, (1, LANES // SWA_HEAD_DIM)).reshape(2, 1, LANES)
    xn = _rmsnorm(x2, g_norm)
    hodd = _odd_proj(xn, w_all, gains, c64, s64, seq)
    y = _swa_attention(hodd, sinks, batch, seq)
    return _out_proj([y], w_out.astype(BF16), x2)


def kernel(x, norm_even, w_in_even, b_f_even, g_qk_fox, g_qk_dsa, g_kidx, w_out_even,
           norm_odd, w_in_odd, g_qk_swa, sinks, w_out_odd):
    batch, seq, d = x.shape
    depth = norm_even.shape[0] + norm_odd.shape[0]
    topk = min(IDX_TOPK_MAX, seq // 4)
    tabs = _rope_tables(seq, HEAD_DIM) + _rope_tables(seq, IDX_DIM)
    x2 = x.reshape(batch * seq, d)
    for layer in range(depth):
        j = layer // 2
        if layer % 2 == 0:
            x2 = _even_layer(x2, batch, seq, norm_even[j], w_in_even[j], b_f_even[j], g_qk_fox[j],
                             g_qk_dsa[j], g_kidx[j], w_out_even[j], tabs, topk)
        else:
            x2 = _odd_layer(x2, batch, seq, norm_odd[j], w_in_odd[j], g_qk_swa[j], sinks[j],
                            w_out_odd[j], tabs)
    return x2.reshape(batch, seq, d)
```

```python
import functools

import jax
import jax.numpy as jnp
from jax import lax
from jax.experimental import pallas as pl
from jax.experimental.pallas import tpu as pltpu

F32 = jnp.float32
BF16 = jnp.bfloat16
I32 = jnp.int32

D_MODEL = 2048
HEAD_DIM = 128
FOX_HEADS = 8
DSA_HEADS = 8
IDX_HEADS = 16
IDX_DIM = 64
IDX_TOPK_MAX = 256
SWA_HEADS = 32
SWA_KV_HEADS = 4
SWA_HEAD_DIM = 64
SWA_WINDOW = 128
ROPE_THETA = 10000.0
EPS = 1e-6
NEG = -1e30

FOX_W = FOX_HEADS * HEAD_DIM
DSA_W = DSA_HEADS * HEAD_DIM
IDX_W = IDX_HEADS * IDX_DIM
SWA_W = SWA_HEADS * SWA_HEAD_DIM
SWA_KV_W = SWA_KV_HEADS * SWA_HEAD_DIM

LANES = 128
VMEM_LIMIT = 56 * 2 ** 20

SM_IK = 0
SM_IW = IDX_DIM
SM_CF = IDX_DIM + IDX_HEADS

LOG2E = 1.4426950408889634
INT_MIN = -2 ** 31
IDX_BITS = 12


def _params(sem):
    return pltpu.CompilerParams(dimension_semantics=sem, vmem_limit_bytes=VMEM_LIMIT)


def _dot(a, b):
    return jnp.dot(a, b, preferred_element_type=F32)


def _dot_nt(a, b):
    return lax.dot_general(a, b, (((1,), (1,)), ((), ())), preferred_element_type=F32)


def _lane_iota(shape):
    return lax.broadcasted_iota(I32, shape, 1)


def _rope(y, cos, sin, dim):
    if dim == LANES:
        rot = pltpu.roll(y, LANES // 2, 1)
    else:
        half = dim // 2
        first = (_lane_iota(y.shape) & half) == 0
        rot = jnp.where(first, pltpu.roll(y, LANES - half, 1), pltpu.roll(y, half, 1))
    return y * cos + rot * sin


def _head_rms(a, gain):
    return a * lax.rsqrt(jnp.mean(a * a, axis=-1, keepdims=True) + EPS) * gain


def _half_rms(a, gain):
    lo = _lane_iota(a.shape) < 64
    sq = a * a
    ms_lo = jnp.sum(jnp.where(lo, sq, 0.0), axis=-1, keepdims=True) * (1.0 / 64)
    ms_hi = jnp.sum(jnp.where(lo, 0.0, sq), axis=-1, keepdims=True) * (1.0 / 64)
    return a * lax.rsqrt(jnp.where(lo, ms_lo, ms_hi) + EPS) * gain


def _rmsnorm_kernel(x_ref, g_ref, o_ref):
    x = x_ref[...]
    y = x * lax.rsqrt(jnp.mean(x * x, axis=-1, keepdims=True) + EPS)
    o_ref[...] = (y * g_ref[...]).astype(o_ref.dtype)


def _rmsnorm(x2, g, tm=512):
    m, d = x2.shape
    return pl.pallas_call(
        _rmsnorm_kernel,
        grid=(m // tm,),
        in_specs=[pl.BlockSpec((tm, d), lambda i: (i, 0)), pl.BlockSpec((1, d), lambda i: (0, 0))],
        out_specs=pl.BlockSpec((tm, d), lambda i: (i, 0)),
        out_shape=jax.ShapeDtypeStruct((m, d), BF16),
        compiler_params=_params(("parallel",)),
        name="rmsnorm",
    )(x2, g.reshape(1, d))


def _skewed_proj_kernel(xn_ref, w_ref, *rest, kinds, nj, tn):
    *aux, o_ref, acc_ref = rest
    t = pl.program_id(0)
    last = pl.num_programs(0) - 1
    j = (t + nj - 1) % nj

    def finish(parts):
        for fn, lo, hi in parts:
            for h in range(lo, hi):
                sl = slice(h * LANES, (h + 1) * LANES)
                o_ref[:, sl] = fn(acc_ref[:, sl]).astype(o_ref.dtype)

    def multiply():
        acc_ref[...] = _dot(xn_ref[...], w_ref[...])

    @pl.when(t == 0)
    def _():
        multiply()

    for cond, parts in kinds(j, aux, tn // LANES):
        @pl.when((t > 0) & (t < last) & cond)
        def _(parts=parts):
            finish(parts())
            multiply()

        @pl.when((t == last) & cond)
        def _(parts=parts):
            finish(parts())


def _skewed_proj(xn, w, aux, aux_specs, kinds, seq, tn, name, tm=1024):
    m, d = xn.shape
    n = w.shape[1]
    nj = n // tn
    ntiles = (m // tm) * nj
    nrow = seq // tm
    cur = lambda t: jnp.minimum(t, ntiles - 1)
    prev = lambda t: jnp.maximum(t - 1, 0)
    tab = pl.BlockSpec((tm, LANES), lambda t: ((prev(t) // nj) % nrow, 0))
    specs = [tab if s is None else s for s in aux_specs]
    return pl.pallas_call(
        functools.partial(_skewed_proj_kernel, kinds=kinds, nj=nj, tn=tn),
        grid=(ntiles + 1,),
        in_specs=[pl.BlockSpec((tm, d), lambda t: (cur(t) // nj, 0)),
                  pl.BlockSpec((d, tn), lambda t: (0, cur(t) % nj))] + specs,
        out_specs=pl.BlockSpec((tm, tn), lambda t: (prev(t) // nj, prev(t) % nj)),
        out_shape=jax.ShapeDtypeStruct((m, n), BF16),
        scratch_shapes=[pltpu.VMEM((tm, tn), F32)],
        compiler_params=_params(("arbitrary",)),
        name=name,
    )(xn, w, *aux)


EV_TN = 1024
EV_FQ, EV_FK, EV_DQ, EV_DK, EV_FV, EV_DV, EV_FG, EV_DG, EV_IQ = range(9)


def _even_kinds(j, aux, nslab):
    gain_ref, c128_ref, s128_ref, c64_ref, s64_ref = aux
    gain = lambda: gain_ref[jnp.minimum(j, 3)]
    return [
        (j <= EV_FK, lambda: [(lambda a: _head_rms(a, gain()), 0, nslab)]),
        ((j == EV_DQ) | (j == EV_DK),
         lambda: [(lambda a: _rope(_head_rms(a, gain()), c128_ref[...], s128_ref[...], HEAD_DIM), 0, nslab)]),
        ((j == EV_FV) | (j == EV_DV), lambda: [(lambda a: a, 0, nslab)]),
        ((j == EV_FG) | (j == EV_DG), lambda: [(lambda a: a * jax.nn.sigmoid(a), 0, nslab)]),
        (j == EV_IQ, lambda: [(lambda a: _rope(a, c64_ref[...], s64_ref[...], IDX_DIM), 0, nslab)]),
    ]


def _even_proj(xn, w, gains, c128, s128, c64, s64, seq):
    gspec = pl.BlockSpec((4, 1, LANES), lambda t: (0, 0, 0))
    return _skewed_proj(xn, w, (gains, c128, s128, c64, s64), [gspec, None, None, None, None],
                        _even_kinds, seq, EV_TN, "even_proj")


def _even_small_kernel(xn_ref, w_ref, gk_ref, bf_ref, c64_ref, s64_ref, o_ref, ot_ref):
    h = _dot(xn_ref[...], w_ref[...])
    lane = _lane_iota(h.shape)
    is_ik = lane < SM_IW
    ms = jnp.sum(jnp.where(is_ik, h * h, 0.0), axis=-1, keepdims=True) * (1.0 / IDX_DIM)
    ik = _rope(h * lax.rsqrt(ms + EPS) * gk_ref[...], c64_ref[...], s64_ref[...], IDX_DIM)
    iw = h * (IDX_HEADS ** -0.5 * IDX_DIM ** -0.5)
    c = jax.nn.log_sigmoid(h + bf_ref[...])
    row = lax.broadcasted_iota(I32, h.shape, 0)
    d = 1
    while d < h.shape[0]:
        c = c + jnp.where(row >= d, pltpu.roll(c, d, 0), 0.0)
        d *= 2
    out = jnp.where(is_ik, ik, jnp.where(lane < SM_CF, iw, c))
    o_ref[...] = out
    ot_ref[0] = out.T


def _even_small(xn, w, gk, bf, c64, s64, batch, seq):
    d = xn.shape[1]
    vec = pl.BlockSpec((1, LANES), lambda b: (0, 0))
    tab = pl.BlockSpec((seq, LANES), lambda b: (0, 0))
    return pl.pallas_call(
        _even_small_kernel,
        grid=(batch,),
        in_specs=[pl.BlockSpec((seq, d), lambda b: (b, 0)),
                  pl.BlockSpec((d, LANES), lambda b: (0, 0)),
                  vec, vec, tab, tab],
        out_specs=[pl.BlockSpec((seq, LANES), lambda b: (b, 0)),
                   pl.BlockSpec((1, LANES, seq), lambda b: (b, 0, 0))],
        out_shape=[jax.ShapeDtypeStruct((batch * seq, LANES), F32),
                   jax.ShapeDtypeStruct((batch, LANES, seq), F32)],
        compiler_params=_params(("parallel",)),
        name="even_small",
    )(xn, w, gk, bf, c64, s64)


FOX_TQ = 256
FOX_HPS = 2


def _fox_kernel(q_ref, k_ref, v_ref, g_ref, small_ref, o_ref, vt_ref, ckb_ref, *, nq):
    hg = pl.program_id(1)
    i = pl.program_id(2)
    tq = FOX_TQ
    heads = [slice(u * HEAD_DIM, (u + 1) * HEAD_DIM) for u in range(FOX_HPS)]

    @pl.when(i == 0)
    def _():
        sm = small_ref[...]
        lane = _lane_iota(sm.shape)
        for u, sl in enumerate(heads):
            vt_ref[sl, :] = v_ref[:, sl].astype(F32).T.astype(BF16)
            ck = jnp.sum(jnp.where(lane == SM_CF + hg * FOX_HPS + u, sm, 0.0), axis=1, keepdims=True)
            ckb_ref[:, sl] = jnp.broadcast_to(ck * LOG2E, (sm.shape[0], LANES))

    c1 = HEAD_DIM ** -0.5 * LOG2E

    def qk(u, j):
        return _dot_nt(k_ref[pl.ds(j * tq, tq), heads[u]], q_ref[:, heads[u]])

    def step(u, j, s, carry, diagonal):
        m, l, acc = carry
        off = j * tq
        ck = ckb_ref[pl.ds(off, tq), heads[u]]
        z = s * c1 - jnp.concatenate([ck, ck], axis=1)
        if diagonal:
            key = lax.broadcasted_iota(I32, (tq, tq), 0)
            qry = lax.broadcasted_iota(I32, (tq, tq), 1)
            z = jnp.where(key <= qry, z, NEG)
        m_new = jnp.maximum(m, jnp.max(z, axis=0, keepdims=True))
        p = jnp.exp2(z - m_new)
        alpha = jnp.exp2(m - m_new)
        l = alpha * l + jnp.sum(p, axis=0, keepdims=True)
        acc = alpha * acc + _dot(vt_ref[heads[u], pl.ds(off, tq)], p.astype(BF16))
        return m_new, l, acc

    init = (jnp.full((1, tq), -jnp.inf, F32), jnp.zeros((1, tq), F32), jnp.zeros((HEAD_DIM, tq), F32))
    for i_static in range(nq):
        @pl.when(i == i_static)
        def _(i_static=i_static):
            carry = [init] * FOX_HPS
            s_next = [qk(u, 0) for u in range(FOX_HPS)]
            for j in range(i_static + 1):
                for u in range(FOX_HPS):
                    s = s_next[u]
                    if j < i_static:
                        s_next[u] = qk(u, j + 1)
                    carry[u] = step(u, j, s, carry[u], j == i_static)
            for u, sl in enumerate(heads):
                _, l, acc = carry[u]
                o_ref[:, sl] = ((acc * (1.0 / l)).T * g_ref[:, sl].astype(F32)).astype(o_ref.dtype)


def _fox_attention(hmain, small, batch, seq):
    tq = FOX_TQ
    nq = seq // tq
    width = FOX_HPS * HEAD_DIM
    ng = FOX_W // width
    return pl.pallas_call(
        functools.partial(_fox_kernel, nq=nq),
        grid=(batch, ng, nq),
        in_specs=[pl.BlockSpec((tq, width), lambda b, h, i: (b * nq + i, EV_FQ * ng + h)),
                  pl.BlockSpec((seq, width), lambda b, h, i: (b, EV_FK * ng + h)),
                  pl.BlockSpec((seq, width), lambda b, h, i: (b, EV_FV * ng + h)),
                  pl.BlockSpec((tq, width), lambda b, h, i: (b * nq + i, EV_FG * ng + h)),
                  pl.BlockSpec((seq, LANES), lambda b, h, i: (b, 0))],
        out_specs=pl.BlockSpec((tq, width), lambda b, h, i: (b * nq + i, h)),
        out_shape=jax.ShapeDtypeStruct((batch * seq, FOX_W), BF16),
        scratch_shapes=[pltpu.VMEM((width, seq), BF16), pltpu.VMEM((seq, width), F32)],
        compiler_params=_params(("parallel", "parallel", "arbitrary")),
        name="fox_attention",
    )(hmain, hmain, hmain, hmain, small)


DSA_TQ = 256


def _sort_key(x):
    b = pltpu.bitcast(x, I32)
    return b ^ ((b >> 31) & 0x7FFFFFFF)


def _dsa_kernel(iq_ref, small_ref, smallt_ref, q_ref, k_ref, v_ref, g_ref, o_ref,
                key_ref, vt_ref, lim_ref, m_ref, l_ref, acc_ref, *, topk):
    i = pl.program_id(1)
    tq = DSA_TQ
    nchunk = i + 1
    half = tq // 2
    key0 = lax.broadcasted_iota(I32, (tq, tq), 0)
    qry = i * tq + lax.broadcasted_iota(I32, (tq, tq), 1)

    @pl.when(i == 0)
    def _():
        for hh in range(DSA_HEADS):
            sl = slice(hh * HEAD_DIM, (hh + 1) * HEAD_DIM)
            vt_ref[sl, :] = v_ref[:, sl].astype(F32).T.astype(BF16)

    def chunk_off(c):
        return pl.multiple_of(c * tq, tq)

    def score_chunk(c, _):
        off = chunk_off(c)
        ikc = small_ref[pl.ds(off, tq), SM_IK:SM_IK + IDX_DIM].astype(BF16)

        def logits(hh):
            return _dot_nt(ikc, iq_ref[:, hh * IDX_DIM:(hh + 1) * IDX_DIM])

        acc = jnp.zeros((tq, tq), F32)
        nxt = logits(0)
        for hh in range(IDX_HEADS):
            cur = nxt
            if hh + 1 < IDX_HEADS:
                nxt = logits(hh + 1)
            w = smallt_ref[0, SM_IW + hh:SM_IW + hh + 1, pl.ds(pl.multiple_of(i * tq, tq), tq)]
            acc = acc + w * jnp.maximum(cur, 0.0)
        key_ref[pl.ds(off, tq), :] = _sort_key(jnp.where(off + key0 <= qry, acc, NEG))
        return 0

    lax.fori_loop(0, nchunk, score_chunk, 0)

    def count(pred):
        def body(c, tot):
            off = chunk_off(c)
            hit = jnp.where(pred(key_ref[pl.ds(off, tq), :], off + key0), 1.0, 0.0)
            return tot + jnp.sum(hit, axis=0, keepdims=True)
        return lax.fori_loop(0, nchunk, body, jnp.zeros((1, tq), F32))

    def thr_step(it, res):
        cand = res + jnp.left_shift(jnp.int32(1), 31 - it)
        cnt = count(lambda kk, ki: kk >= cand)
        return jnp.where(cnt >= topk, cand, res)

    thr = lax.fori_loop(0, 32, thr_step, jnp.full((1, tq), INT_MIN, I32))
    n_gt = count(lambda kk, ki: kk > thr)
    n_ge = count(lambda kk, ki: kk >= thr)
    need = topk - n_gt

    lim_ref[...] = jnp.full((1, tq), 2 ** IDX_BITS, I32)

    @pl.when(jnp.max(n_ge) > topk)
    def _():
        def lim_step(it, res):
            cand = res + jnp.left_shift(jnp.int32(1), IDX_BITS - 1 - it)
            cnt = count(lambda kk, ki: (kk == thr) & (ki < cand))
            return jnp.where(cnt <= need, cand, res)
        lim_ref[...] = lax.fori_loop(0, IDX_BITS, lim_step, jnp.zeros((1, tq), I32))

    lim = lim_ref[...]

    m_ref[...] = jnp.full(m_ref.shape, -jnp.inf, F32)
    l_ref[...] = jnp.zeros(l_ref.shape, F32)
    acc_ref[...] = jnp.zeros(acc_ref.shape, F32)
    c1 = HEAD_DIM ** -0.5 * LOG2E

    def attn_chunk(c, _):
        off = chunk_off(c)
        kk = key_ref[pl.ds(off, tq), :]
        ki = off + key0
        sel = ((kk > thr) | ((kk == thr) & (ki < lim))) & (ki <= qry)
        bias = jnp.where(sel, 0.0, NEG)

        def qk(hh):
            sl = slice(hh * HEAD_DIM, (hh + 1) * HEAD_DIM)
            return _dot_nt(k_ref[pl.ds(off, tq), sl], q_ref[:, sl])

        nxt = qk(0)
        for hh in range(DSA_HEADS):
            cur = nxt
            if hh + 1 < DSA_HEADS:
                nxt = qk(hh + 1)
            z = cur * c1 + bias
            m = m_ref[hh:hh + 1, :]
            m_new = jnp.maximum(m, jnp.max(z, axis=0, keepdims=True))
            p = jnp.exp2(z - m_new)
            alpha = jnp.exp2(m - m_new)
            l_ref[hh:hh + 1, :] = alpha * l_ref[hh:hh + 1, :] + jnp.sum(p, axis=0, keepdims=True)
            pv = _dot(vt_ref[hh * HEAD_DIM:(hh + 1) * HEAD_DIM, pl.ds(off, tq)], p.astype(BF16))
            acc_ref[hh] = alpha * acc_ref[hh] + pv
            m_ref[hh:hh + 1, :] = m_new
        return 0

    lax.fori_loop(0, nchunk, attn_chunk, 0)
    for hh in range(DSA_HEADS):
        sl = slice(hh * HEAD_DIM, (hh + 1) * HEAD_DIM)
        out = (acc_ref[hh] * (1.0 / l_ref[hh:hh + 1, :])).T
        o_ref[:, sl] = (out * g_ref[:, sl].astype(F32)).astype(o_ref.dtype)


def _dsa_attention(hmain, small, small_t, batch, seq, topk):
    tq = DSA_TQ
    nq = seq // tq
    wide = lambda t: pl.BlockSpec((tq, DSA_W), lambda b, i: (b * nq + i, t))
    full = lambda t: pl.BlockSpec((seq, DSA_W), lambda b, i: (b, t))
    return pl.pallas_call(
        functools.partial(_dsa_kernel, topk=topk),
        grid=(batch, nq),
        in_specs=[wide(EV_IQ),
                  pl.BlockSpec((seq, LANES), lambda b, i: (b, 0)),
                  pl.BlockSpec((1, LANES, seq), lambda b, i: (b, 0, 0)),
                  wide(EV_DQ), full(EV_DK), full(EV_DV), wide(EV_DG)],
        out_specs=pl.BlockSpec((tq, DSA_W), lambda b, i: (b * nq + i, 0)),
        out_shape=jax.ShapeDtypeStruct((batch * seq, DSA_W), BF16),
        scratch_shapes=[pltpu.VMEM((seq, tq), I32),
                        pltpu.VMEM((DSA_W, seq), BF16),
                        pltpu.VMEM((1, tq), I32),
                        pltpu.VMEM((DSA_HEADS, tq), F32),
                        pltpu.VMEM((DSA_HEADS, tq), F32),
                        pltpu.VMEM((DSA_HEADS, HEAD_DIM, tq), F32)],
        compiler_params=_params(("parallel", "arbitrary")),
        name="dsa_attention",
    )(hmain, small, small_t, hmain, hmain, hmain, hmain)


def _out_proj_kernel(*refs):
    *y_refs, w_ref, x_ref, o_ref = refs
    acc = x_ref[...]
    off = 0
    for y_ref in y_refs:
        kdim = y_ref.shape[1]
        acc = acc + _dot(y_ref[...], w_ref[off:off + kdim, :])
        off += kdim
    o_ref[...] = acc


def _out_proj(ys, w, x2, tm=512):
    m, d = x2.shape
    return pl.pallas_call(
        _out_proj_kernel,
        grid=(m // tm,),
        in_specs=[pl.BlockSpec((tm, y.shape[1]), lambda i: (i, 0)) for y in ys]
                 + [pl.BlockSpec(w.shape, lambda i: (0, 0)), pl.BlockSpec((tm, d), lambda i: (i, 0))],
        out_specs=pl.BlockSpec((tm, d), lambda i: (i, 0)),
        out_shape=jax.ShapeDtypeStruct((m, d), F32),
        compiler_params=_params(("parallel",)),
        name="out_proj",
    )(*ys, w, x2)


OD_TN = 512
OD_NQ = SWA_W // OD_TN


def _odd_kinds(j, aux, nslab):
    gain_ref, c64_ref, s64_ref = aux
    qk = lambda idx: (lambda a: _rope(_half_rms(a, gain_ref[idx]), c64_ref[...], s64_ref[...], SWA_HEAD_DIM))
    nk = SWA_KV_W // LANES
    return [
        (j < OD_NQ, lambda: [(qk(0), 0, nslab)]),
        ((j >= OD_NQ) & (j < 2 * OD_NQ), lambda: [(lambda a: a * jax.nn.sigmoid(a), 0, nslab)]),
        (j == 2 * OD_NQ, lambda: [(qk(1), 0, nk), (lambda a: a, nk, nslab)]),
    ]


def _odd_proj(xn, w, gains, c64, s64, seq):
    gspec = pl.BlockSpec((2, 1, LANES), lambda t: (0, 0, 0))
    return _skewed_proj(xn, w, (gains, c64, s64), [gspec, None, None], _odd_kinds, seq, OD_TN, "odd_proj")


def _swa_kernel(sink_ref, q_ref, kvp_ref, kvc_ref, g_ref, o_ref):
    n = pl.program_id(1)
    w = SWA_WINDOW
    dh = SWA_HEAD_DIM
    group = SWA_HEADS // SWA_KV_HEADS
    kb = lax.broadcasted_iota(I32, (2 * w, 2 * w), 0)
    qi = lax.broadcasted_iota(I32, (2 * w, 2 * w), 1) & (w - 1)
    valid = (kb > qi) & (kb <= qi + w) & ((n > 0) | (kb >= w))
    bias = jnp.where(valid, 0.0, NEG)
    first = lax.broadcasted_iota(I32, (1, 2 * w), 1) < w
    c1 = dh ** -0.5 * LOG2E

    def band(col):
        sl = slice(col, col + dh)
        return jnp.concatenate([kvp_ref[:, sl], kvc_ref[:, sl]], axis=0)

    def qk(hp, kband):
        qpair = jnp.concatenate([q_ref[:, (2 * hp) * dh:(2 * hp + 1) * dh],
                                 q_ref[:, (2 * hp + 1) * dh:(2 * hp + 2) * dh]], axis=0)
        return _dot_nt(kband, qpair)

    pairs = group // 2
    for kv in range(SWA_KV_HEADS):
        kband = band(kv * dh)
        vt = band(SWA_KV_W + kv * dh).astype(F32).T.astype(BF16)
        nxt = qk(kv * pairs, kband)
        for t in range(pairs):
            hp = kv * pairs + t
            cur = nxt
            if t + 1 < pairs:
                nxt = qk(hp + 1, kband)
            z = cur * c1 + bias
            sink = jnp.where(first, sink_ref[2 * hp], sink_ref[2 * hp + 1]) * LOG2E
            m = jnp.maximum(jnp.max(z, axis=0, keepdims=True), sink)
            p = jnp.exp2(z - m)
            den = jnp.sum(p, axis=0, keepdims=True) + jnp.exp2(sink - m)
            ot = (_dot(vt, p.astype(BF16)) * (1.0 / den)).T
            o = jnp.concatenate([ot[:w], ot[w:]], axis=1)
            sl = slice(hp * LANES, (hp + 1) * LANES)
            o_ref[:, sl] = (o * g_ref[:, sl].astype(F32)).astype(o_ref.dtype)


def _swa_attention(hodd, sinks, batch, seq):
    w = SWA_WINDOW
    nb = seq // w
    kv_col = 2 * SWA_W // (2 * SWA_KV_W)
    return pl.pallas_call(
        _swa_kernel,
        grid=(batch, nb),
        in_specs=[pl.BlockSpec(memory_space=pltpu.SMEM),
                  pl.BlockSpec((w, SWA_W), lambda b, n: (b * nb + n, 0)),
                  pl.BlockSpec((w, 2 * SWA_KV_W), lambda b, n: (b * nb + jnp.maximum(n - 1, 0), kv_col)),
                  pl.BlockSpec((w, 2 * SWA_KV_W), lambda b, n: (b * nb + n, kv_col)),
                  pl.BlockSpec((w, SWA_W), lambda b, n: (b * nb + n, 1))],
        out_specs=pl.BlockSpec((w, SWA_W), lambda b, n: (b * nb + n, 0)),
        out_shape=jax.ShapeDtypeStruct((batch * seq, SWA_W), BF16),
        compiler_params=_params(("parallel", "arbitrary")),
        name="swa_attention",
    )(sinks, hodd, hodd, hodd, hodd)


def _rope_tables(seq, dim):
    inv = 1.0 / (ROPE_THETA ** (jnp.arange(0, dim, 2, dtype=F32) / dim))
    ang = jnp.arange(seq, dtype=F32)[:, None] * inv[None, :]
    cos, sin = jnp.cos(ang), jnp.sin(ang)
    reps = LANES // dim
    return (jnp.tile(jnp.concatenate([cos, cos], -1), (1, reps)),
            jnp.tile(jnp.concatenate([-sin, sin], -1), (1, reps)))


def _cols(w, start, size):
    return lax.slice_in_dim(w, start, start + size, axis=1)


def _even_layer(x2, batch, seq, g_norm, w_in, b_f, g_fox, g_dsa, g_kidx, w_out, tabs, topk):
    c128, s128, c64, s64 = tabs
    o = 0
    offs = {}
    for name, size in (("fq", FOX_W), ("fk", FOX_W), ("fv", FOX_W), ("fg", FOX_W), ("af", FOX_HEADS),
                       ("dq", DSA_W), ("dk", DSA_W), ("dv", DSA_W), ("dg", DSA_W),
                       ("iq", IDX_W), ("ik", IDX_DIM), ("iw", IDX_HEADS)):
        offs[name] = (o, size)
        o += size
    w_main = jnp.concatenate([_cols(w_in, *offs[n]) for n in
                              ("fq", "fk", "dq", "dk", "fv", "dv", "fg", "dg", "iq")], axis=1).astype(BF16)
    pad = LANES - (IDX_DIM + IDX_HEADS + FOX_HEADS)
    w_small = jnp.concatenate([_cols(w_in, *offs["ik"]), _cols(w_in, *offs["iw"]), _cols(w_in, *offs["af"]),
                               jnp.zeros((w_in.shape[0], pad), F32)], axis=1).astype(BF16)
    gains = jnp.stack([g_fox[0], g_fox[1], g_dsa[0], g_dsa[1]]).reshape(4, 1, LANES)
    gk = jnp.concatenate([g_kidx, jnp.zeros((LANES - IDX_DIM,), F32)]).reshape(1, LANES)
    bf = jnp.concatenate([jnp.zeros((SM_CF,), F32), b_f, jnp.zeros((pad,), F32)]).reshape(1, LANES)

    xn = _rmsnorm(x2, g_norm)
    hmain = _even_proj(xn, w_main, gains, c128, s128, c64, s64, seq)
    small, small_t = _even_small(xn, w_small, gk, bf, c64, s64, batch, seq)
    ya = _fox_attention(hmain, small, batch, seq)
    yb = _dsa_attention(hmain, small, small_t, batch, seq, topk)
    return _out_proj([ya, yb], w_out.astype(BF16), x2)


def _odd_layer(x2, batch, seq, g_norm, w_in, g_qk, sinks, w_out, tabs):
    _, _, c64, s64 = tabs
    q_w, k_w, v_w, g_w = (_cols(w_in, 0, SWA_W), _cols(w_in, SWA_W, SWA_KV_W),
                          _cols(w_in, SWA_W + SWA_KV_W, SWA_KV_W), _cols(w_in, SWA_W + 2 * SWA_KV_W, SWA_W))
    w_all = jnp.concatenate([q_w, g_w, k_w, v_w], axis=1).astype(BF16)
    gains = jnp.tile(g_qk, (1, LANES // SWA_HEAD_DIM)).reshape(2, 1, LANES)
    xn = _rmsnorm(x2, g_norm)
    hodd = _odd_proj(xn, w_all, gains, c64, s64, seq)
    y = _swa_attention(hodd, sinks, batch, seq)
    return _out_proj([y], w_out.astype(BF16), x2)


def kernel(x, norm_even, w_in_even, b_f_even, g_qk_fox, g_qk_dsa, g_kidx, w_out_even,
           norm_odd, w_in_odd, g_qk_swa, sinks, w_out_odd):
    batch, seq, d = x.shape
    depth = norm_even.shape[0] + norm_odd.shape[0]
    topk = min(IDX_TOPK_MAX, seq // 4)
    tabs = _rope_tables(seq, HEAD_DIM) + _rope_tables(seq, IDX_DIM)
    x2 = x.reshape(batch * seq, d)
    for layer in range(depth):
        j = layer // 2
        if layer % 2 == 0:
            x2 = _even_layer(x2, batch, seq, norm_even[j], w_in_even[j], b_f_even[j], g_qk_fox[j],
                             g_qk_dsa[j], g_kidx[j], w_out_even[j], tabs, topk)
        else:
            x2 = _odd_layer(x2, batch, seq, norm_odd[j], w_in_odd[j], g_qk_swa[j], sinks[j],
                            w_out_odd[j], tabs)
    return x2.reshape(batch, seq, d)
```

```python
import functools

import jax
import jax.numpy as jnp
from jax import lax
from jax.experimental import pallas as pl
from jax.experimental.pallas import tpu as pltpu

F32 = jnp.float32
BF16 = jnp.bfloat16
I32 = jnp.int32
I16 = jnp.int16

D_MODEL = 2048
HEAD_DIM = 128
FOX_HEADS = 8
DSA_HEADS = 8
IDX_HEADS = 16
IDX_DIM = 64
IDX_TOPK_MAX = 256
SWA_HEADS = 32
SWA_KV_HEADS = 4
SWA_HEAD_DIM = 64
SWA_WINDOW = 128
ROPE_THETA = 10000.0
EPS = 1e-6
NEG = -1e30

FOX_W = FOX_HEADS * HEAD_DIM
DSA_W = DSA_HEADS * HEAD_DIM
IDX_W = IDX_HEADS * IDX_DIM
SWA_W = SWA_HEADS * SWA_HEAD_DIM
SWA_KV_W = SWA_KV_HEADS * SWA_HEAD_DIM

LANES = 128
VMEM_LIMIT = 56 * 2 ** 20

SM_IK = 0
SM_IW = IDX_DIM
SM_CF = IDX_DIM + IDX_HEADS

LOG2E = 1.4426950408889634
INT_MIN = -2 ** 31
I16_MIN = -2 ** 15
IDX_BITS = 12


def _params(sem):
    return pltpu.CompilerParams(dimension_semantics=sem, vmem_limit_bytes=VMEM_LIMIT)


def _dot(a, b):
    return jnp.dot(a, b, preferred_element_type=F32)


def _dot_nt(a, b):
    return lax.dot_general(a, b, (((1,), (1,)), ((), ())), preferred_element_type=F32)


def _lane_iota(shape):
    return lax.broadcasted_iota(I32, shape, 1)


def _rope(y, cos, sin, dim):
    if dim == LANES:
        rot = pltpu.roll(y, LANES // 2, 1)
    else:
        half = dim // 2
        first = (_lane_iota(y.shape) & half) == 0
        rot = jnp.where(first, pltpu.roll(y, LANES - half, 1), pltpu.roll(y, half, 1))
    return y * cos + rot * sin


def _head_rms(a, gain):
    return a * lax.rsqrt(jnp.mean(a * a, axis=-1, keepdims=True) + EPS) * gain


def _half_rms(a, gain):
    lo = _lane_iota(a.shape) < 64
    sq = a * a
    ms_lo = jnp.sum(jnp.where(lo, sq, 0.0), axis=-1, keepdims=True) * (1.0 / 64)
    ms_hi = jnp.sum(jnp.where(lo, 0.0, sq), axis=-1, keepdims=True) * (1.0 / 64)
    return a * lax.rsqrt(jnp.where(lo, ms_lo, ms_hi) + EPS) * gain


def _rmsnorm_kernel(x_ref, g_ref, o_ref):
    x = x_ref[...]
    y = x * lax.rsqrt(jnp.mean(x * x, axis=-1, keepdims=True) + EPS)
    o_ref[...] = (y * g_ref[...]).astype(o_ref.dtype)


def _rmsnorm(x2, g, tm=512):
    m, d = x2.shape
    return pl.pallas_call(
        _rmsnorm_kernel,
        grid=(m // tm,),
        in_specs=[pl.BlockSpec((tm, d), lambda i: (i, 0)), pl.BlockSpec((1, d), lambda i: (0, 0))],
        out_specs=pl.BlockSpec((tm, d), lambda i: (i, 0)),
        out_shape=jax.ShapeDtypeStruct((m, d), BF16),
        compiler_params=_params(("parallel",)),
        name="rmsnorm",
    )(x2, g.reshape(1, d))


def _skewed_proj_kernel(xn_ref, w_ref, *rest, kinds, nj, tn):
    *aux, o_ref, acc_ref = rest
    t = pl.program_id(0)
    last = pl.num_programs(0) - 1
    j = (t + nj - 1) % nj

    def finish(parts):
        for fn, lo, hi in parts:
            for h in range(lo, hi):
                sl = slice(h * LANES, (h + 1) * LANES)
                o_ref[:, sl] = fn(acc_ref[:, sl]).astype(o_ref.dtype)

    def multiply():
        acc_ref[...] = _dot(xn_ref[...], w_ref[...])

    @pl.when(t == 0)
    def _():
        multiply()

    for cond, parts in kinds(j, aux, tn // LANES):
        @pl.when((t > 0) & (t < last) & cond)
        def _(parts=parts):
            finish(parts())
            multiply()

        @pl.when((t == last) & cond)
        def _(parts=parts):
            finish(parts())


def _skewed_proj(xn, w, aux, aux_specs, kinds, seq, tn, name, tm=1024):
    m, d = xn.shape
    n = w.shape[1]
    nj = n // tn
    ntiles = (m // tm) * nj
    nrow = seq // tm
    cur = lambda t: jnp.minimum(t, ntiles - 1)
    prev = lambda t: jnp.maximum(t - 1, 0)
    tab = pl.BlockSpec((tm, LANES), lambda t: ((prev(t) // nj) % nrow, 0))
    specs = [tab if s is None else s for s in aux_specs]
    return pl.pallas_call(
        functools.partial(_skewed_proj_kernel, kinds=kinds, nj=nj, tn=tn),
        grid=(ntiles + 1,),
        in_specs=[pl.BlockSpec((tm, d), lambda t: (cur(t) // nj, 0)),
                  pl.BlockSpec((d, tn), lambda t: (0, cur(t) % nj))] + specs,
        out_specs=pl.BlockSpec((tm, tn), lambda t: (prev(t) // nj, prev(t) % nj)),
        out_shape=jax.ShapeDtypeStruct((m, n), BF16),
        scratch_shapes=[pltpu.VMEM((tm, tn), F32)],
        compiler_params=_params(("arbitrary",)),
        name=name,
    )(xn, w, *aux)


EV_TN = 1024
EV_FQ, EV_FK, EV_DQ, EV_DK, EV_FV, EV_DV, EV_FG, EV_DG, EV_IQ = range(9)


def _even_kinds(j, aux, nslab):
    gain_ref, c128_ref, s128_ref, c64_ref, s64_ref = aux
    gain = lambda: gain_ref[jnp.minimum(j, 3)]
    return [
        (j <= EV_FK, lambda: [(lambda a: _head_rms(a, gain()), 0, nslab)]),
        ((j == EV_DQ) | (j == EV_DK),
         lambda: [(lambda a: _rope(_head_rms(a, gain()), c128_ref[...], s128_ref[...], HEAD_DIM), 0, nslab)]),
        ((j == EV_FV) | (j == EV_DV), lambda: [(lambda a: a, 0, nslab)]),
        ((j == EV_FG) | (j == EV_DG), lambda: [(lambda a: a * jax.nn.sigmoid(a), 0, nslab)]),
        (j == EV_IQ, lambda: [(lambda a: _rope(a, c64_ref[...], s64_ref[...], IDX_DIM), 0, nslab)]),
    ]


def _even_proj(xn, w, gains, c128, s128, c64, s64, seq):
    gspec = pl.BlockSpec((4, 1, LANES), lambda t: (0, 0, 0))
    return _skewed_proj(xn, w, (gains, c128, s128, c64, s64), [gspec, None, None, None, None],
                        _even_kinds, seq, EV_TN, "even_proj")


def _even_small_kernel(xn_ref, w_ref, gk_ref, bf_ref, c64_ref, s64_ref, o_ref, ot_ref):
    h = _dot(xn_ref[...], w_ref[...])
    lane = _lane_iota(h.shape)
    is_ik = lane < SM_IW
    ms = jnp.sum(jnp.where(is_ik, h * h, 0.0), axis=-1, keepdims=True) * (1.0 / IDX_DIM)
    ik = _rope(h * lax.rsqrt(ms + EPS) * gk_ref[...], c64_ref[...], s64_ref[...], IDX_DIM)
    iw = h * (IDX_HEADS ** -0.5 * IDX_DIM ** -0.5)
    c = jax.nn.log_sigmoid(h + bf_ref[...])
    row = lax.broadcasted_iota(I32, h.shape, 0)
    d = 1
    while d < h.shape[0]:
        c = c + jnp.where(row >= d, pltpu.roll(c, d, 0), 0.0)
        d *= 2
    out = jnp.where(is_ik, ik, jnp.where(lane < SM_CF, iw, c))
    o_ref[...] = out
    ot_ref[0] = out.T


def _even_small(xn, w, gk, bf, c64, s64, batch, seq):
    d = xn.shape[1]
    vec = pl.BlockSpec((1, LANES), lambda b: (0, 0))
    tab = pl.BlockSpec((seq, LANES), lambda b: (0, 0))
    return pl.pallas_call(
        _even_small_kernel,
        grid=(batch,),
        in_specs=[pl.BlockSpec((seq, d), lambda b: (b, 0)),
                  pl.BlockSpec((d, LANES), lambda b: (0, 0)),
                  vec, vec, tab, tab],
        out_specs=[pl.BlockSpec((seq, LANES), lambda b: (b, 0)),
                   pl.BlockSpec((1, LANES, seq), lambda b: (b, 0, 0))],
        out_shape=[jax.ShapeDtypeStruct((batch * seq, LANES), F32),
                   jax.ShapeDtypeStruct((batch, LANES, seq), F32)],
        compiler_params=_params(("parallel",)),
        name="even_small",
    )(xn, w, gk, bf, c64, s64)


FOX_TQ = 256
FOX_HPS = 2


def _fox_kernel(q_ref, k_ref, v_ref, g_ref, small_ref, o_ref, vt_ref, ckb_ref, *, nq):
    hg = pl.program_id(1)
    i = pl.program_id(2)
    tq = FOX_TQ
    heads = [slice(u * HEAD_DIM, (u + 1) * HEAD_DIM) for u in range(FOX_HPS)]

    @pl.when(i == 0)
    def _():
        sm = small_ref[...]
        lane = _lane_iota(sm.shape)
        for u, sl in enumerate(heads):
            vt_ref[sl, :] = v_ref[:, sl].astype(F32).T.astype(BF16)
            ck = jnp.sum(jnp.where(lane == SM_CF + hg * FOX_HPS + u, sm, 0.0), axis=1, keepdims=True)
            ckb_ref[:, sl] = jnp.broadcast_to(ck * LOG2E, (sm.shape[0], LANES))

    c1 = HEAD_DIM ** -0.5 * LOG2E

    def qk(u, j):
        return _dot_nt(k_ref[pl.ds(j * tq, tq), heads[u]], q_ref[:, heads[u]])

    def step(u, j, s, carry, diagonal):
        m, l, acc = carry
        off = j * tq
        ck = ckb_ref[pl.ds(off, tq), heads[u]]
        z = s * c1 - jnp.concatenate([ck, ck], axis=1)
        if diagonal:
            key = lax.broadcasted_iota(I32, (tq, tq), 0)
            qry = lax.broadcasted_iota(I32, (tq, tq), 1)
            z = jnp.where(key <= qry, z, NEG)
        m_new = jnp.maximum(m, jnp.max(z, axis=0, keepdims=True))
        p = jnp.exp2(z - m_new)
        alpha = jnp.exp2(m - m_new)
        l = alpha * l + jnp.sum(p, axis=0, keepdims=True)
        acc = alpha * acc + _dot(vt_ref[heads[u], pl.ds(off, tq)], p.astype(BF16))
        return m_new, l, acc

    init = (jnp.full((1, tq), -jnp.inf, F32), jnp.zeros((1, tq), F32), jnp.zeros((HEAD_DIM, tq), F32))
    for i_static in range(nq):
        @pl.when(i == i_static)
        def _(i_static=i_static):
            carry = [init] * FOX_HPS
            s_next = [qk(u, 0) for u in range(FOX_HPS)]
            for j in range(i_static + 1):
                for u in range(FOX_HPS):
                    s = s_next[u]
                    if j < i_static:
                        s_next[u] = qk(u, j + 1)
                    carry[u] = step(u, j, s, carry[u], j == i_static)
            for u, sl in enumerate(heads):
                _, l, acc = carry[u]
                o_ref[:, sl] = ((acc * (1.0 / l)).T * g_ref[:, sl].astype(F32)).astype(o_ref.dtype)


def _fox_attention(hmain, small, batch, seq):
    tq = FOX_TQ
    nq = seq // tq
    width = FOX_HPS * HEAD_DIM
    ng = FOX_W // width
    return pl.pallas_call(
        functools.partial(_fox_kernel, nq=nq),
        grid=(batch, ng, nq),
        in_specs=[pl.BlockSpec((tq, width), lambda b, h, i: (b * nq + i, EV_FQ * ng + h)),
                  pl.BlockSpec((seq, width), lambda b, h, i: (b, EV_FK * ng + h)),
                  pl.BlockSpec((seq, width), lambda b, h, i: (b, EV_FV * ng + h)),
                  pl.BlockSpec((tq, width), lambda b, h, i: (b * nq + i, EV_FG * ng + h)),
                  pl.BlockSpec((seq, LANES), lambda b, h, i: (b, 0))],
        out_specs=pl.BlockSpec((tq, width), lambda b, h, i: (b * nq + i, h)),
        out_shape=jax.ShapeDtypeStruct((batch * seq, FOX_W), BF16),
        scratch_shapes=[pltpu.VMEM((width, seq), BF16), pltpu.VMEM((seq, width), F32)],
        compiler_params=_params(("parallel", "parallel", "arbitrary")),
        name="fox_attention",
    )(hmain, hmain, hmain, hmain, small)


DSA_TQ = 256


def _sort_key(x):
    b = pltpu.bitcast(x, I32)
    return b ^ ((b >> 31) & 0x7FFFFFFF)


def _dsa_kernel(iq_ref, small_ref, smallt_ref, q_ref, k_ref, v_ref, g_ref, o_ref,
                key_ref, hi_ref, lo_ref, vt_ref, lim_ref, m_ref, l_ref, acc_ref, *, topk):
    i = pl.program_id(1)
    tq = DSA_TQ
    nchunk = i + 1
    half = tq // 2
    key0 = lax.broadcasted_iota(I32, (tq, tq), 0)
    qry = i * tq + lax.broadcasted_iota(I32, (tq, tq), 1)

    @pl.when(i == 0)
    def _():
        for hh in range(DSA_HEADS):
            sl = slice(hh * HEAD_DIM, (hh + 1) * HEAD_DIM)
            vt_ref[sl, :] = v_ref[:, sl].astype(F32).T.astype(BF16)

    def chunk_off(c):
        return pl.multiple_of(c * tq, tq)

    def score_chunk(c, _):
        off = chunk_off(c)
        ikc = small_ref[pl.ds(off, tq), SM_IK:SM_IK + IDX_DIM].astype(BF16)

        def logits(hh):
            return _dot_nt(ikc, iq_ref[:, hh * IDX_DIM:(hh + 1) * IDX_DIM])

        acc = jnp.zeros((tq, tq), F32)
        nxt = logits(0)
        for hh in range(IDX_HEADS):
            cur = nxt
            if hh + 1 < IDX_HEADS:
                nxt = logits(hh + 1)
            w = smallt_ref[0, SM_IW + hh:SM_IW + hh + 1, pl.ds(pl.multiple_of(i * tq, tq), tq)]
            acc = acc + w * jnp.maximum(cur, 0.0)
        key = _sort_key(jnp.where(off + key0 <= qry, acc, NEG))
        key_ref[pl.ds(off, tq), :] = key
        hi_ref[pl.ds(off, tq), :] = (key >> 16).astype(I16)
        lo_ref[pl.ds(off, tq), :] = ((key & 0xFFFF) + I16_MIN).astype(I16)
        return 0

    lax.fori_loop(0, nchunk, score_chunk, 0)

    def count(pred):
        def body(c, tot):
            off = chunk_off(c)
            hit = jnp.where(pred(key_ref[pl.ds(off, tq), :], off + key0), 1.0, 0.0)
            return tot + jnp.sum(hit.reshape(tq // 8, 8, tq), axis=0)
        tot = lax.fori_loop(0, nchunk, body, jnp.zeros((8, tq), F32))
        return jnp.sum(tot, axis=0, keepdims=True)

    def count16(ref, cand):
        c16 = cand.astype(I16)

        def body(c, tot):
            off = chunk_off(c)
            hit = jnp.where(ref[pl.ds(off, tq), :] >= c16, jnp.bfloat16(1), jnp.bfloat16(0))
            part = hit[0:16]
            for g in range(1, tq // 16):
                part = part + hit[16 * g:16 * (g + 1)]
            return tot + part.astype(F32)
        tot = lax.fori_loop(0, nchunk, body, jnp.zeros((16, tq), F32))
        return jnp.sum(tot, axis=0, keepdims=True)

    def search16(ref, want):
        def step(it, res):
            cand = res + jnp.left_shift(jnp.int32(1), 15 - it)
            return jnp.where(count16(ref, cand) >= want, cand, res)
        return lax.fori_loop(0, 16, step, jnp.full((1, tq), I16_MIN, I32))

    t_hi = search16(hi_ref, topk)
    t_hi16 = t_hi.astype(I16)
    n_above = count16(hi_ref, t_hi + 1)

    def mask_lo(c, _):
        off = chunk_off(c)
        same = hi_ref[pl.ds(off, tq), :] == t_hi16
        lo_ref[pl.ds(off, tq), :] = jnp.where(same, lo_ref[pl.ds(off, tq), :], jnp.int16(I16_MIN))
        return 0

    lax.fori_loop(0, nchunk, mask_lo, 0)
    t_lo = search16(lo_ref, topk - n_above)
    thr = jnp.left_shift(t_hi, 16) | (t_lo - I16_MIN)
    n_gt = count(lambda kk, ki: kk > thr)
    n_ge = count(lambda kk, ki: kk >= thr)
    need = topk - n_gt

    lim_ref[...] = jnp.full((1, tq), 2 ** IDX_BITS, I32)

    @pl.when(jnp.max(n_ge) > topk)
    def _():
        def lim_step(it, res):
            cand = res + jnp.left_shift(jnp.int32(1), IDX_BITS - 1 - it)
            cnt = count(lambda kk, ki: (kk == thr) & (ki < cand))
            return jnp.where(cnt <= need, cand, res)
        lim_ref[...] = lax.fori_loop(0, IDX_BITS, lim_step, jnp.zeros((1, tq), I32))

    lim = lim_ref[...]

    m_ref[...] = jnp.full(m_ref.shape, -jnp.inf, F32)
    l_ref[...] = jnp.zeros(l_ref.shape, F32)
    acc_ref[...] = jnp.zeros(acc_ref.shape, F32)
    c1 = HEAD_DIM ** -0.5 * LOG2E

    def attn_chunk(c, _):
        off = chunk_off(c)
        kk = key_ref[pl.ds(off, tq), :]
        ki = off + key0
        sel = ((kk > thr) | ((kk == thr) & (ki < lim))) & (ki <= qry)
        bias = jnp.where(sel, 0.0, NEG)

        def qk(hh):
            sl = slice(hh * HEAD_DIM, (hh + 1) * HEAD_DIM)
            return _dot_nt(k_ref[pl.ds(off, tq), sl], q_ref[:, sl])

        nxt = qk(0)
        for hh in range(DSA_HEADS):
            cur = nxt
            if hh + 1 < DSA_HEADS:
                nxt = qk(hh + 1)
            z = cur * c1 + bias
            m = m_ref[hh:hh + 1, :]
            m_new = jnp.maximum(m, jnp.max(z, axis=0, keepdims=True))
            p = jnp.exp2(z - m_new)
            alpha = jnp.exp2(m - m_new)
            l_ref[hh:hh + 1, :] = alpha * l_ref[hh:hh + 1, :] + jnp.sum(p, axis=0, keepdims=True)
            pv = _dot(vt_ref[hh * HEAD_DIM:(hh + 1) * HEAD_DIM, pl.ds(off, tq)], p.astype(BF16))
            acc_ref[hh] = alpha * acc_ref[hh] + pv
            m_ref[hh:hh + 1, :] = m_new
        return 0

    lax.fori_loop(0, nchunk, attn_chunk, 0)
    for hh in range(DSA_HEADS):
        sl = slice(hh * HEAD_DIM, (hh + 1) * HEAD_DIM)
        out = (acc_ref[hh] * (1.0 / l_ref[hh:hh + 1, :])).T
        o_ref[:, sl] = (out * g_ref[:, sl].astype(F32)).astype(o_ref.dtype)


def _dsa_attention(hmain, small, small_t, batch, seq, topk):
    tq = DSA_TQ
    nq = seq // tq
    wide = lambda t: pl.BlockSpec((tq, DSA_W), lambda b, i: (b * nq + i, t))
    full = lambda t: pl.BlockSpec((seq, DSA_W), lambda b, i: (b, t))
    return pl.pallas_call(
        functools.partial(_dsa_kernel, topk=topk),
        grid=(batch, nq),
        in_specs=[wide(EV_IQ),
                  pl.BlockSpec((seq, LANES), lambda b, i: (b, 0)),
                  pl.BlockSpec((1, LANES, seq), lambda b, i: (b, 0, 0)),
                  wide(EV_DQ), full(EV_DK), full(EV_DV), wide(EV_DG)],
        out_specs=pl.BlockSpec((tq, DSA_W), lambda b, i: (b * nq + i, 0)),
        out_shape=jax.ShapeDtypeStruct((batch * seq, DSA_W), BF16),
        scratch_shapes=[pltpu.VMEM((seq, tq), I32),
                        pltpu.VMEM((seq, tq), I16),
                        pltpu.VMEM((seq, tq), I16),
                        pltpu.VMEM((DSA_W, seq), BF16),
                        pltpu.VMEM((1, tq), I32),
                        pltpu.VMEM((DSA_HEADS, tq), F32),
                        pltpu.VMEM((DSA_HEADS, tq), F32),
                        pltpu.VMEM((DSA_HEADS, HEAD_DIM, tq), F32)],
        compiler_params=_params(("parallel", "arbitrary")),
        name="dsa_attention",
    )(hmain, small, small_t, hmain, hmain, hmain, hmain)


def _out_proj_kernel(*refs):
    *y_refs, w_ref, x_ref, o_ref = refs
    acc = x_ref[...]
    off = 0
    for y_ref in y_refs:
        kdim = y_ref.shape[1]
        acc = acc + _dot(y_ref[...], w_ref[off:off + kdim, :])
        off += kdim
    o_ref[...] = acc


def _out_proj(ys, w, x2, tm=512):
    m, d = x2.shape
    return pl.pallas_call(
        _out_proj_kernel,
        grid=(m // tm,),
        in_specs=[pl.BlockSpec((tm, y.shape[1]), lambda i: (i, 0)) for y in ys]
                 + [pl.BlockSpec(w.shape, lambda i: (0, 0)), pl.BlockSpec((tm, d), lambda i: (i, 0))],
        out_specs=pl.BlockSpec((tm, d), lambda i: (i, 0)),
        out_shape=jax.ShapeDtypeStruct((m, d), F32),
        compiler_params=_params(("parallel",)),
        name="out_proj",
    )(*ys, w, x2)


OD_TN = 512
OD_NQ = SWA_W // OD_TN


def _odd_kinds(j, aux, nslab):
    gain_ref, c64_ref, s64_ref = aux
    qk = lambda idx: (lambda a: _rope(_half_rms(a, gain_ref[idx]), c64_ref[...], s64_ref[...], SWA_HEAD_DIM))
    nk = SWA_KV_W // LANES
    return [
        (j < OD_NQ, lambda: [(qk(0), 0, nslab)]),
        ((j >= OD_NQ) & (j < 2 * OD_NQ), lambda: [(lambda a: a * jax.nn.sigmoid(a), 0, nslab)]),
        (j == 2 * OD_NQ, lambda: [(qk(1), 0, nk), (lambda a: a, nk, nslab)]),
    ]


def _odd_proj(xn, w, gains, c64, s64, seq):
    gspec = pl.BlockSpec((2, 1, LANES), lambda t: (0, 0, 0))
    return _skewed_proj(xn, w, (gains, c64, s64), [gspec, None, None], _odd_kinds, seq, OD_TN, "odd_proj")


def _swa_kernel(sink_ref, q_ref, kvp_ref, kvc_ref, g_ref, o_ref):
    n = pl.program_id(1)
    w = SWA_WINDOW
    dh = SWA_HEAD_DIM
    group = SWA_HEADS // SWA_KV_HEADS
    kb = lax.broadcasted_iota(I32, (2 * w, 2 * w), 0)
    qi = lax.broadcasted_iota(I32, (2 * w, 2 * w), 1) & (w - 1)
    valid = (kb > qi) & (kb <= qi + w) & ((n > 0) | (kb >= w))
    bias = jnp.where(valid, 0.0, NEG)
    first = lax.broadcasted_iota(I32, (1, 2 * w), 1) < w
    c1 = dh ** -0.5 * LOG2E

    def band(col):
        sl = slice(col, col + dh)
        return jnp.concatenate([kvp_ref[:, sl], kvc_ref[:, sl]], axis=0)

    def qk(hp, kband):
        qpair = jnp.concatenate([q_ref[:, (2 * hp) * dh:(2 * hp + 1) * dh],
                                 q_ref[:, (2 * hp + 1) * dh:(2 * hp + 2) * dh]], axis=0)
        return _dot_nt(kband, qpair)

    pairs = group // 2
    for kv in range(SWA_KV_HEADS):
        kband = band(kv * dh)
        vt = band(SWA_KV_W + kv * dh).astype(F32).T.astype(BF16)
        nxt = qk(kv * pairs, kband)
        for t in range(pairs):
            hp = kv * pairs + t
            cur = nxt
            if t + 1 < pairs:
                nxt = qk(hp + 1, kband)
            z = cur * c1 + bias
            sink = jnp.where(first, sink_ref[2 * hp], sink_ref[2 * hp + 1]) * LOG2E
            m = jnp.maximum(jnp.max(z, axis=0, keepdims=True), sink)
            p = jnp.exp2(z - m)
            den = jnp.sum(p, axis=0, keepdims=True) + jnp.exp2(sink - m)
            ot = (_dot(vt, p.astype(BF16)) * (1.0 / den)).T
            o = jnp.concatenate([ot[:w], ot[w:]], axis=1)
            sl = slice(hp * LANES, (hp + 1) * LANES)
            o_ref[:, sl] = (o * g_ref[:, sl].astype(F32)).astype(o_ref.dtype)


def _swa_attention(hodd, sinks, batch, seq):
    w = SWA_WINDOW
    nb = seq // w
    kv_col = 2 * SWA_W // (2 * SWA_KV_W)
    return pl.pallas_call(
        _swa_kernel,
        grid=(batch, nb),
        in_specs=[pl.BlockSpec(memory_space=pltpu.SMEM),
                  pl.BlockSpec((w, SWA_W), lambda b, n: (b * nb + n, 0)),
                  pl.BlockSpec((w, 2 * SWA_KV_W), lambda b, n: (b * nb + jnp.maximum(n - 1, 0), kv_col)),
                  pl.BlockSpec((w, 2 * SWA_KV_W), lambda b, n: (b * nb + n, kv_col)),
                  pl.BlockSpec((w, SWA_W), lambda b, n: (b * nb + n, 1))],
        out_specs=pl.BlockSpec((w, SWA_W), lambda b, n: (b * nb + n, 0)),
        out_shape=jax.ShapeDtypeStruct((batch * seq, SWA_W), BF16),
        compiler_params=_params(("parallel", "arbitrary")),
        name="swa_attention",
    )(sinks, hodd, hodd, hodd, hodd)


def _rope_tables(seq, dim):
    inv = 1.0 / (ROPE_THETA ** (jnp.arange(0, dim, 2, dtype=F32) / dim))
    ang = jnp.arange(seq, dtype=F32)[:, None] * inv[None, :]
    cos, sin = jnp.cos(ang), jnp.sin(ang)
    reps = LANES // dim
    return (jnp.tile(jnp.concatenate([cos, cos], -1), (1, reps)),
            jnp.tile(jnp.concatenate([-sin, sin], -1), (1, reps)))


def _cols(w, start, size):
    return lax.slice_in_dim(w, start, start + size, axis=1)


def _even_layer(x2, batch, seq, g_norm, w_in, b_f, g_fox, g_dsa, g_kidx, w_out, tabs, topk):
    c128, s128, c64, s64 = tabs
    o = 0
    offs = {}
    for name, size in (("fq", FOX_W), ("fk", FOX_W), ("fv", FOX_W), ("fg", FOX_W), ("af", FOX_HEADS),
                       ("dq", DSA_W), ("dk", DSA_W), ("dv", DSA_W), ("dg", DSA_W),
                       ("iq", IDX_W), ("ik", IDX_DIM), ("iw", IDX_HEADS)):
        offs[name] = (o, size)
        o += size
    w_main = jnp.concatenate([_cols(w_in, *offs[n]) for n in
                              ("fq", "fk", "dq", "dk", "fv", "dv", "fg", "dg", "iq")], axis=1).astype(BF16)
    pad = LANES - (IDX_DIM + IDX_HEADS + FOX_HEADS)
    w_small = jnp.concatenate([_cols(w_in, *offs["ik"]), _cols(w_in, *offs["iw"]), _cols(w_in, *offs["af"]),
                               jnp.zeros((w_in.shape[0], pad), F32)], axis=1).astype(BF16)
    gains = jnp.stack([g_fox[0], g_fox[1], g_dsa[0], g_dsa[1]]).reshape(4, 1, LANES)
    gk = jnp.concatenate([g_kidx, jnp.zeros((LANES - IDX_DIM,), F32)]).reshape(1, LANES)
    bf = jnp.concatenate([jnp.zeros((SM_CF,), F32), b_f, jnp.zeros((pad,), F32)]).reshape(1, LANES)

    xn = _rmsnorm(x2, g_norm)
    hmain = _even_proj(xn, w_main, gains, c128, s128, c64, s64, seq)
    small, small_t = _even_small(xn, w_small, gk, bf, c64, s64, batch, seq)
    ya = _fox_attention(hmain, small, batch, seq)
    yb = _dsa_attention(hmain, small, small_t, batch, seq, topk)
    return _out_proj([ya, yb], w_out.astype(BF16), x2)


def _odd_layer(x2, batch, seq, g_norm, w_in, g_qk, sinks, w_out, tabs):
    _, _, c64, s64 = tabs
    q_w, k_w, v_w, g_w = (_cols(w_in, 0, SWA_W), _cols(w_in, SWA_W, SWA_KV_W),
                          _cols(w_in, SWA_W + SWA_KV_W, SWA_KV_W), _cols(w_in, SWA_W + 2 * SWA_KV_W, SWA_W))
    w_all = jnp.concatenate([q_w, g_w, k_w, v_w], axis=1).astype(BF16)
    gains = jnp.tile(g_qk, (1, LANES // SWA_HEAD_DIM)).reshape(2, 1, LANES)
    xn = _rmsnorm(x2, g_norm)
    hodd = _odd_proj(xn, w_all, gains, c64, s64, seq)
    y = _swa_attention(hodd, sinks, batch, seq)
    return _out_proj([y], w_out.astype(BF16), x2)


def kernel(x, norm_even, w_in_even, b_f_even, g_qk_fox, g_qk_dsa, g_kidx, w_out_even,
           norm_odd, w_in_odd, g_qk_swa, sinks, w_out_odd):
    batch, seq, d = x.shape
    depth = norm_even.shape[0] + norm_odd.shape[0]
    topk = min(IDX_TOPK_MAX, seq // 4)
    tabs = _rope_tables(seq, HEAD_DIM) + _rope_tables(seq, IDX_DIM)
    x2 = x.reshape(batch * seq, d)
    for layer in range(depth):
        j = layer // 2
        if layer % 2 == 0:
            x2 = _even_layer(x2, batch, seq, norm_even[j], w_in_even[j], b_f_even[j], g_qk_fox[j],
                             g_qk_dsa[j], g_kidx[j], w_out_even[j], tabs, topk)
        else:
            x2 = _odd_layer(x2, batch, seq, norm_odd[j], w_in_odd[j], g_qk_swa[j], sinks[j],
                            w_out_odd[j], tabs)
    return x2.reshape(batch, seq, d)
```

```python
import functools

import jax
import jax.numpy as jnp
from jax import lax
from jax.experimental import pallas as pl
from jax.experimental.pallas import tpu as pltpu

F32 = jnp.float32
BF16 = jnp.bfloat16
I32 = jnp.int32
I16 = jnp.int16

D_MODEL = 2048
HEAD_DIM = 128
FOX_HEADS = 8
DSA_HEADS = 8
IDX_HEADS = 16
IDX_DIM = 64
IDX_TOPK_MAX = 256
SWA_HEADS = 32
SWA_KV_HEADS = 4
SWA_HEAD_DIM = 64
SWA_WINDOW = 128
ROPE_THETA = 10000.0
EPS = 1e-6
NEG = -1e30

FOX_W = FOX_HEADS * HEAD_DIM
DSA_W = DSA_HEADS * HEAD_DIM
IDX_W = IDX_HEADS * IDX_DIM
SWA_W = SWA_HEADS * SWA_HEAD_DIM
SWA_KV_W = SWA_KV_HEADS * SWA_HEAD_DIM

LANES = 128
VMEM_LIMIT = 56 * 2 ** 20

SM_IK = 0
SM_IW = IDX_DIM
SM_CF = IDX_DIM + IDX_HEADS

LOG2E = 1.4426950408889634
INT_MIN = -2 ** 31
I16_MIN = -2 ** 15
IDX_BITS = 12


def _params(sem):
    return pltpu.CompilerParams(dimension_semantics=sem, vmem_limit_bytes=VMEM_LIMIT)


def _dot(a, b):
    return jnp.dot(a, b, preferred_element_type=F32)


def _dot_nt(a, b):
    return lax.dot_general(a, b, (((1,), (1,)), ((), ())), preferred_element_type=F32)


def _lane_iota(shape):
    return lax.broadcasted_iota(I32, shape, 1)


def _rope(y, cos, sin, dim):
    if dim == LANES:
        rot = pltpu.roll(y, LANES // 2, 1)
    else:
        half = dim // 2
        first = (_lane_iota(y.shape) & half) == 0
        rot = jnp.where(first, pltpu.roll(y, LANES - half, 1), pltpu.roll(y, half, 1))
    return y * cos + rot * sin


def _head_rms(a, gain):
    return a * lax.rsqrt(jnp.mean(a * a, axis=-1, keepdims=True) + EPS) * gain


def _half_rms(a, gain):
    lo = _lane_iota(a.shape) < 64
    sq = a * a
    ms_lo = jnp.sum(jnp.where(lo, sq, 0.0), axis=-1, keepdims=True) * (1.0 / 64)
    ms_hi = jnp.sum(jnp.where(lo, 0.0, sq), axis=-1, keepdims=True) * (1.0 / 64)
    return a * lax.rsqrt(jnp.where(lo, ms_lo, ms_hi) + EPS) * gain


def _rmsnorm_kernel(x_ref, g_ref, o_ref):
    x = x_ref[...]
    y = x * lax.rsqrt(jnp.mean(x * x, axis=-1, keepdims=True) + EPS)
    o_ref[...] = (y * g_ref[...]).astype(o_ref.dtype)


def _rmsnorm(x2, g, tm=512):
    m, d = x2.shape
    return pl.pallas_call(
        _rmsnorm_kernel,
        grid=(m // tm,),
        in_specs=[pl.BlockSpec((tm, d), lambda i: (i, 0)), pl.BlockSpec((1, d), lambda i: (0, 0))],
        out_specs=pl.BlockSpec((tm, d), lambda i: (i, 0)),
        out_shape=jax.ShapeDtypeStruct((m, d), BF16),
        compiler_params=_params(("parallel",)),
        name="rmsnorm",
    )(x2, g.reshape(1, d))


def _skewed_proj_kernel(xn_ref, w_ref, *rest, kinds, nj, tn):
    *aux, o_ref, acc_ref = rest
    t = pl.program_id(0)
    last = pl.num_programs(0) - 1
    j = (t + nj - 1) % nj

    def finish(parts):
        for fn, lo, hi in parts:
            for h in range(lo, hi):
                sl = slice(h * LANES, (h + 1) * LANES)
                o_ref[:, sl] = fn(acc_ref[:, sl]).astype(o_ref.dtype)

    def multiply():
        acc_ref[...] = _dot(xn_ref[...], w_ref[...])

    @pl.when(t == 0)
    def _():
        multiply()

    for cond, parts in kinds(j, aux, tn // LANES):
        @pl.when((t > 0) & (t < last) & cond)
        def _(parts=parts):
            finish(parts())
            multiply()

        @pl.when((t == last) & cond)
        def _(parts=parts):
            finish(parts())


def _skewed_proj(xn, w, aux, aux_specs, kinds, seq, tn, name, tm=1024):
    m, d = xn.shape
    n = w.shape[1]
    nj = n // tn
    ntiles = (m // tm) * nj
    nrow = seq // tm
    cur = lambda t: jnp.minimum(t, ntiles - 1)
    prev = lambda t: jnp.maximum(t - 1, 0)
    tab = pl.BlockSpec((tm, LANES), lambda t: ((prev(t) // nj) % nrow, 0))
    specs = [tab if s is None else s for s in aux_specs]
    return pl.pallas_call(
        functools.partial(_skewed_proj_kernel, kinds=kinds, nj=nj, tn=tn),
        grid=(ntiles + 1,),
        in_specs=[pl.BlockSpec((tm, d), lambda t: (cur(t) // nj, 0)),
                  pl.BlockSpec((d, tn), lambda t: (0, cur(t) % nj))] + specs,
        out_specs=pl.BlockSpec((tm, tn), lambda t: (prev(t) // nj, prev(t) % nj)),
        out_shape=jax.ShapeDtypeStruct((m, n), BF16),
        scratch_shapes=[pltpu.VMEM((tm, tn), F32)],
        compiler_params=_params(("arbitrary",)),
        name=name,
    )(xn, w, *aux)


EV_TN = 1024
EV_FQ, EV_FK, EV_FV, EV_FG, EV_DQ, EV_DK, EV_DV, EV_DG, EV_IQ = range(9)


def _even_kinds(j, aux, nslab):
    gain_ref, c128_ref, s128_ref, c64_ref, s64_ref = aux
    gain = lambda: gain_ref[jnp.minimum(j, EV_IQ)]
    return [
        ((j == EV_FQ) | (j == EV_FK), lambda: [(lambda a: _head_rms(a, gain()), 0, nslab)]),
        ((j == EV_DQ) | (j == EV_DK),
         lambda: [(lambda a: _rope(_head_rms(a, gain()), c128_ref[...], s128_ref[...], HEAD_DIM), 0, nslab)]),
        ((j == EV_FV) | (j == EV_DV), lambda: [(lambda a: a, 0, nslab)]),
        ((j == EV_FG) | (j == EV_DG), lambda: [(lambda a: a * jax.nn.sigmoid(a), 0, nslab)]),
        (j == EV_IQ, lambda: [(lambda a: _rope(a, c64_ref[...], s64_ref[...], IDX_DIM), 0, nslab)]),
    ]


def _even_proj(xn, w, gains, c128, s128, c64, s64, seq):
    gspec = pl.BlockSpec(gains.shape, lambda t: (0, 0, 0))
    return _skewed_proj(xn, w, (gains, c128, s128, c64, s64), [gspec, None, None, None, None],
                        _even_kinds, seq, EV_TN, "even_proj")


def _even_small_kernel(xn_ref, w_ref, gk_ref, bf_ref, c64_ref, s64_ref, o_ref, ot_ref):
    h = _dot(xn_ref[...], w_ref[...])
    lane = _lane_iota(h.shape)
    is_ik = lane < SM_IW
    ms = jnp.sum(jnp.where(is_ik, h * h, 0.0), axis=-1, keepdims=True) * (1.0 / IDX_DIM)
    ik = _rope(h * lax.rsqrt(ms + EPS) * gk_ref[...], c64_ref[...], s64_ref[...], IDX_DIM)
    iw = h * (IDX_HEADS ** -0.5 * IDX_DIM ** -0.5)
    c = jax.nn.log_sigmoid(h + bf_ref[...])
    row = lax.broadcasted_iota(I32, h.shape, 0)
    d = 1
    while d < h.shape[0]:
        c = c + jnp.where(row >= d, pltpu.roll(c, d, 0), 0.0)
        d *= 2
    out = jnp.where(is_ik, ik, jnp.where(lane < SM_CF, iw, c))
    o_ref[...] = out
    ot_ref[0] = out.T


def _even_small(xn, w, gk, bf, c64, s64, batch, seq):
    d = xn.shape[1]
    vec = pl.BlockSpec((1, LANES), lambda b: (0, 0))
    tab = pl.BlockSpec((seq, LANES), lambda b: (0, 0))
    return pl.pallas_call(
        _even_small_kernel,
        grid=(batch,),
        in_specs=[pl.BlockSpec((seq, d), lambda b: (b, 0)),
                  pl.BlockSpec((d, LANES), lambda b: (0, 0)),
                  vec, vec, tab, tab],
        out_specs=[pl.BlockSpec((seq, LANES), lambda b: (b, 0)),
                   pl.BlockSpec((1, LANES, seq), lambda b: (b, 0, 0))],
        out_shape=[jax.ShapeDtypeStruct((batch * seq, LANES), F32),
                   jax.ShapeDtypeStruct((batch, LANES, seq), F32)],
        compiler_params=_params(("parallel",)),
        name="even_small",
    )(xn, w, gk, bf, c64, s64)


FOX_TQ = 256
FOX_HPS = 4
ONES_ROWS = 16


def _fox_kernel(q_ref, k_ref, v_ref, g_ref, small_ref, o_ref, ka_ref, vt_ref, *, nq):
    hg = pl.program_id(1)
    i = pl.program_id(2)
    tq = FOX_TQ
    seq = k_ref.shape[0]
    heads = [slice(u * HEAD_DIM, (u + 1) * HEAD_DIM) for u in range(FOX_HPS)]
    lane = _lane_iota((seq, LANES))

    @pl.when(i == 0)
    def _():
        sm = small_ref[...]
        for u, sl in enumerate(heads):
            vt_ref[u, 0:HEAD_DIM, :] = v_ref[:, sl].astype(F32).T.astype(BF16)
            vt_ref[u, HEAD_DIM:, :] = jnp.ones((ONES_ROWS, seq), BF16)
            ck = jnp.sum(jnp.where(lane == SM_CF + hg * FOX_HPS + u, sm, 0.0), axis=1, keepdims=True)
            neg = jnp.broadcast_to(ck * -LOG2E, (seq, LANES))
            hi = neg.astype(BF16).astype(F32)
            mid = (neg - hi).astype(BF16).astype(F32)
            lo = neg - hi - mid
            extra = jnp.where(lane == 0, hi, jnp.where(lane == 1, mid, jnp.where(lane == 2, lo, 0.0)))
            ka_ref[u, :, 0:HEAD_DIM] = k_ref[:, sl]
            ka_ref[u, :, HEAD_DIM:] = extra.astype(BF16)

    ones3 = jnp.where(_lane_iota((tq, LANES)) < 3, 1.0, 0.0).astype(BF16)
    qa = [jnp.concatenate([q_ref[:, sl], ones3], axis=1) for sl in heads]

    def qk(u, j):
        return _dot_nt(ka_ref[u, pl.ds(j * tq, tq), :], qa[u])

    def step(u, j, z, carry, diagonal):
        m, acc = carry
        if diagonal:
            key = lax.broadcasted_iota(I32, (tq, tq), 0)
            qry = lax.broadcasted_iota(I32, (tq, tq), 1)
            z = jnp.where(key <= qry, z, NEG)
        m_new = jnp.maximum(m, jnp.max(z, axis=0, keepdims=True))
        p = jnp.exp2(z - m_new)
        acc = jnp.exp2(m - m_new) * acc + _dot(vt_ref[u, :, pl.ds(j * tq, tq)], p.astype(BF16))
        return m_new, acc

    init = (jnp.full((1, tq), -jnp.inf, F32), jnp.zeros((HEAD_DIM + ONES_ROWS, tq), F32))
    for i_static in range(nq):
        @pl.when(i == i_static)
        def _(i_static=i_static):
            carry = [init] * FOX_HPS
            z_next = [qk(u, 0) for u in range(FOX_HPS)]
            for j in range(i_static + 1):
                for u in range(FOX_HPS):
                    z = z_next[u]
                    if j < i_static:
                        z_next[u] = qk(u, j + 1)
                    carry[u] = step(u, j, z, carry[u], j == i_static)
            for u, sl in enumerate(heads):
                _, acc = carry[u]
                out = (acc[0:HEAD_DIM] * (1.0 / acc[HEAD_DIM:HEAD_DIM + 1])).T
                o_ref[:, sl] = (out * g_ref[:, sl].astype(F32)).astype(o_ref.dtype)


def _fox_attention(hmain, small, batch, seq):
    tq = FOX_TQ
    nq = seq // tq
    width = FOX_HPS * HEAD_DIM
    ng = FOX_W // width
    return pl.pallas_call(
        functools.partial(_fox_kernel, nq=nq),
        grid=(batch, ng, nq),
        in_specs=[pl.BlockSpec((tq, width), lambda b, h, i: (b * nq + i, EV_FQ * ng + h)),
                  pl.BlockSpec((seq, width), lambda b, h, i: (b, EV_FK * ng + h)),
                  pl.BlockSpec((seq, width), lambda b, h, i: (b, EV_FV * ng + h)),
                  pl.BlockSpec((tq, width), lambda b, h, i: (b * nq + i, EV_FG * ng + h)),
                  pl.BlockSpec((seq, LANES), lambda b, h, i: (b, 0))],
        out_specs=pl.BlockSpec((tq, width), lambda b, h, i: (b * nq + i, h)),
        out_shape=jax.ShapeDtypeStruct((batch * seq, FOX_W), BF16),
        scratch_shapes=[pltpu.VMEM((FOX_HPS, seq, 2 * HEAD_DIM), BF16),
                        pltpu.VMEM((FOX_HPS, HEAD_DIM + ONES_ROWS, seq), BF16)],
        compiler_params=_params(("parallel", "parallel", "arbitrary")),
        name="fox_attention",
    )(hmain, hmain, hmain, hmain, small)


DSA_TQ = 256


def _sort_key(x):
    b = pltpu.bitcast(x, I32)
    return b ^ ((b >> 31) & 0x7FFFFFFF)


def _dsa_kernel(iq_ref, small_ref, smallt_ref, q_ref, k_ref, v_ref, g_ref, o_ref,
                key_ref, hi_ref, lo_ref, vt_ref, lim_ref, m_ref, acc_ref, *, topk):
    i = pl.program_id(1)
    tq = DSA_TQ
    nchunk = i + 1
    half = tq // 2
    key0 = lax.broadcasted_iota(I32, (tq, tq), 0)
    qry = i * tq + lax.broadcasted_iota(I32, (tq, tq), 1)

    @pl.when(i == 0)
    def _():
        for hh in range(DSA_HEADS):
            sl = slice(hh * HEAD_DIM, (hh + 1) * HEAD_DIM)
            vt_ref[hh, 0:HEAD_DIM, :] = v_ref[:, sl].astype(F32).T.astype(BF16)
            vt_ref[hh, HEAD_DIM:, :] = jnp.ones((ONES_ROWS, v_ref.shape[0]), BF16)

    def chunk_off(c):
        return pl.multiple_of(c * tq, tq)

    def score_chunk(c, _):
        off = chunk_off(c)
        ikc = small_ref[pl.ds(off, tq), SM_IK:SM_IK + IDX_DIM].astype(BF16)

        def logits(hh):
            return _dot_nt(ikc, iq_ref[:, hh * IDX_DIM:(hh + 1) * IDX_DIM])

        acc = jnp.zeros((tq, tq), F32)
        nxt = logits(0)
        for hh in range(IDX_HEADS):
            cur = nxt
            if hh + 1 < IDX_HEADS:
                nxt = logits(hh + 1)
            w = smallt_ref[0, SM_IW + hh:SM_IW + hh + 1, pl.ds(pl.multiple_of(i * tq, tq), tq)]
            acc = acc + w * jnp.maximum(cur, 0.0)
        key = _sort_key(jnp.where(off + key0 <= qry, acc, NEG))
        key_ref[pl.ds(off, tq), :] = key
        hi_ref[pl.ds(off, tq), :] = (key >> 16).astype(I16)
        lo_ref[pl.ds(off, tq), :] = ((key & 0xFFFF) + I16_MIN).astype(I16)
        return 0

    lax.fori_loop(0, nchunk, score_chunk, 0)

    def count(pred):
        def body(c, tot):
            off = chunk_off(c)
            hit = jnp.where(pred(key_ref[pl.ds(off, tq), :], off + key0), 1.0, 0.0)
            return tot + jnp.sum(hit.reshape(tq // 8, 8, tq), axis=0)
        tot = lax.fori_loop(0, nchunk, body, jnp.zeros((8, tq), F32))
        return jnp.sum(tot, axis=0, keepdims=True)

    def count16(ref, cand):
        c16 = cand.astype(I16)

        def body(c, tot):
            off = chunk_off(c)
            hit = jnp.where(ref[pl.ds(off, tq), :] >= c16, jnp.bfloat16(1), jnp.bfloat16(0))
            part = hit[0:16]
            for g in range(1, tq // 16):
                part = part + hit[16 * g:16 * (g + 1)]
            return tot + part.astype(F32)
        tot = lax.fori_loop(0, nchunk, body, jnp.zeros((16, tq), F32))
        return jnp.sum(tot, axis=0, keepdims=True)

    def search16(ref, want):
        def step(it, res):
            cand = res + jnp.left_shift(jnp.int32(1), 15 - it)
            return jnp.where(count16(ref, cand) >= want, cand, res)
        return lax.fori_loop(0, 16, step, jnp.full((1, tq), I16_MIN, I32))

    t_hi = search16(hi_ref, topk)
    t_hi16 = t_hi.astype(I16)
    n_above = count16(hi_ref, t_hi + 1)

    def mask_lo(c, _):
        off = chunk_off(c)
        same = hi_ref[pl.ds(off, tq), :] == t_hi16
        lo_ref[pl.ds(off, tq), :] = jnp.where(same, lo_ref[pl.ds(off, tq), :], jnp.int16(I16_MIN))
        return 0

    lax.fori_loop(0, nchunk, mask_lo, 0)
    t_lo = search16(lo_ref, topk - n_above)
    thr = jnp.left_shift(t_hi, 16) | (t_lo - I16_MIN)
    n_gt = count(lambda kk, ki: kk > thr)
    n_ge = count(lambda kk, ki: kk >= thr)
    need = topk - n_gt

    lim_ref[...] = jnp.full((1, tq), 2 ** IDX_BITS, I32)

    @pl.when(jnp.max(n_ge) > topk)
    def _():
        def lim_step(it, res):
            cand = res + jnp.left_shift(jnp.int32(1), IDX_BITS - 1 - it)
            cnt = count(lambda kk, ki: (kk == thr) & (ki < cand))
            return jnp.where(cnt <= need, cand, res)
        lim_ref[...] = lax.fori_loop(0, IDX_BITS, lim_step, jnp.zeros((1, tq), I32))

    lim = lim_ref[...]

    m_ref[...] = jnp.full(m_ref.shape, -jnp.inf, F32)
    acc_ref[...] = jnp.zeros(acc_ref.shape, F32)

    def attn_chunk(c, _):
        off = chunk_off(c)
        kk = key_ref[pl.ds(off, tq), :]
        ki = off + key0
        sel = ((kk > thr) | ((kk == thr) & (ki < lim))) & (ki <= qry)
        bias = jnp.where(sel, 0.0, NEG)

        def qk(hh):
            sl = slice(hh * HEAD_DIM, (hh + 1) * HEAD_DIM)
            return _dot_nt(k_ref[pl.ds(off, tq), sl], q_ref[:, sl])

        heads = range(DSA_HEADS)
        zs = [bias + qk(hh) for hh in heads]
        ms = [m_ref[hh:hh + 1, :] for hh in heads]
        m_news = [jnp.maximum(ms[hh], jnp.max(zs[hh], axis=0, keepdims=True)) for hh in heads]
        ps = [jnp.exp2(zs[hh] - m_news[hh]).astype(BF16) for hh in heads]
        pvs = [_dot(vt_ref[hh, :, pl.ds(off, tq)], ps[hh]) for hh in heads]
        for hh in heads:
            acc_ref[hh] = jnp.exp2(ms[hh] - m_news[hh]) * acc_ref[hh] + pvs[hh]
            m_ref[hh:hh + 1, :] = m_news[hh]
        return 0

    lax.fori_loop(0, nchunk, attn_chunk, 0)
    for hh in range(DSA_HEADS):
        sl = slice(hh * HEAD_DIM, (hh + 1) * HEAD_DIM)
        out = (acc_ref[hh, 0:HEAD_DIM, :] * (1.0 / acc_ref[hh, HEAD_DIM:HEAD_DIM + 1, :])).T
        o_ref[:, sl] = (out * g_ref[:, sl].astype(F32)).astype(o_ref.dtype)


def _dsa_attention(hmain, small, small_t, batch, seq, topk):
    tq = DSA_TQ
    nq = seq // tq
    wide = lambda t: pl.BlockSpec((tq, DSA_W), lambda b, i: (b * nq + i, t))
    full = lambda t: pl.BlockSpec((seq, DSA_W), lambda b, i: (b, t))
    return pl.pallas_call(
        functools.partial(_dsa_kernel, topk=topk),
        grid=(batch, nq),
        in_specs=[wide(EV_IQ),
                  pl.BlockSpec((seq, LANES), lambda b, i: (b, 0)),
                  pl.BlockSpec((1, LANES, seq), lambda b, i: (b, 0, 0)),
                  wide(EV_DQ), full(EV_DK), full(EV_DV), wide(EV_DG)],
        out_specs=pl.BlockSpec((tq, DSA_W), lambda b, i: (b * nq + i, 0)),
        out_shape=jax.ShapeDtypeStruct((batch * seq, DSA_W), BF16),
        scratch_shapes=[pltpu.VMEM((seq, tq), I32),
                        pltpu.VMEM((seq, tq), I16),
                        pltpu.VMEM((seq, tq), I16),
                        pltpu.VMEM((DSA_HEADS, HEAD_DIM + ONES_ROWS, seq), BF16),
                        pltpu.VMEM((1, tq), I32),
                        pltpu.VMEM((DSA_HEADS, tq), F32),
                        pltpu.VMEM((DSA_HEADS, HEAD_DIM + ONES_ROWS, tq), F32)],
        compiler_params=_params(("parallel", "arbitrary")),
        name="dsa_attention",
    )(hmain, small, small_t, hmain, hmain, hmain, hmain)


def _out_proj_kernel(*refs):
    *y_refs, w_ref, x_ref, o_ref = refs
    acc = x_ref[...]
    off = 0
    for y_ref in y_refs:
        kdim = y_ref.shape[1]
        acc = acc + _dot(y_ref[...], w_ref[off:off + kdim, :])
        off += kdim
    o_ref[...] = acc


def _out_proj(ys, w, x2, tm=512):
    m, d = x2.shape
    return pl.pallas_call(
        _out_proj_kernel,
        grid=(m // tm,),
        in_specs=[pl.BlockSpec((tm, y.shape[1]), lambda i: (i, 0)) for y in ys]
                 + [pl.BlockSpec(w.shape, lambda i: (0, 0)), pl.BlockSpec((tm, d), lambda i: (i, 0))],
        out_specs=pl.BlockSpec((tm, d), lambda i: (i, 0)),
        out_shape=jax.ShapeDtypeStruct((m, d), F32),
        compiler_params=_params(("parallel",)),
        name="out_proj",
    )(*ys, w, x2)


OD_TN = 512
OD_NQ = SWA_W // OD_TN


def _odd_kinds(j, aux, nslab):
    gain_ref, c64_ref, s64_ref = aux
    qk = lambda idx: (lambda a: _rope(_half_rms(a, gain_ref[idx]), c64_ref[...], s64_ref[...], SWA_HEAD_DIM))
    nk = SWA_KV_W // LANES
    return [
        (j < OD_NQ, lambda: [(qk(0), 0, nslab)]),
        (j == OD_NQ, lambda: [(qk(1), 0, nk), (lambda a: a, nk, nslab)]),
        (j > OD_NQ, lambda: [(lambda a: a * jax.nn.sigmoid(a), 0, nslab)]),
    ]


def _odd_proj(xn, w, gains, c64, s64, seq):
    gspec = pl.BlockSpec((2, 1, LANES), lambda t: (0, 0, 0))
    return _skewed_proj(xn, w, (gains, c64, s64), [gspec, None, None], _odd_kinds, seq, OD_TN, "odd_proj")


def _swa_kernel(sink_ref, q_ref, kvp_ref, kvc_ref, *rest):
    *g_refs, o_ref = rest
    n = pl.program_id(1)
    w = SWA_WINDOW
    dh = SWA_HEAD_DIM
    group = SWA_HEADS // SWA_KV_HEADS
    gw = group * w
    kb = lax.broadcasted_iota(I32, (2 * w, gw), 0)
    qi = lax.broadcasted_iota(I32, (2 * w, gw), 1) & (w - 1)
    valid = (kb > qi) & (kb <= qi + w) & ((n > 0) | (kb >= w))
    bias = jnp.where(valid, 0.0, NEG)
    head_of_lane = lax.broadcasted_iota(I32, (1, gw), 1) // w
    ones = jnp.ones((ONES_ROWS, 2 * w), BF16)

    def band(col):
        sl = slice(col, col + dh)
        return jnp.concatenate([kvp_ref[:, sl], kvc_ref[:, sl]], axis=0)

    for kv in range(SWA_KV_HEADS):
        h0 = kv * group
        qg = jnp.concatenate([q_ref[:, (h0 + t) * dh:(h0 + t + 1) * dh] for t in range(group)], axis=0)
        z = bias + _dot_nt(band(kv * dh), qg)
        sink = jnp.zeros((1, gw), F32)
        for t in range(group):
            sink = jnp.where(head_of_lane == t, sink_ref[h0 + t] * LOG2E, sink)
        m = jnp.maximum(jnp.max(z, axis=0, keepdims=True), sink)
        vt = jnp.concatenate([band(SWA_KV_W + kv * dh).astype(F32).T.astype(BF16), ones], axis=0)
        pv = _dot(vt, jnp.exp2(z - m).astype(BF16))
        den = pv[dh:dh + 1] + jnp.exp2(sink - m)
        ot = (pv[0:dh] * (1.0 / den)).T
        o = jnp.concatenate([ot[t * w:(t + 1) * w] for t in range(group)], axis=1)
        sl = slice(h0 * dh, (h0 + group) * dh)
        o_ref[:, sl] = (o * g_refs[kv][...].astype(F32)).astype(o_ref.dtype)


def _swa_attention(hodd, sinks, batch, seq):
    w = SWA_WINDOW
    nb = seq // w
    kvw = 2 * SWA_KV_W
    kv_col = SWA_W // kvw
    gate = lambda kv: pl.BlockSpec((w, kvw), lambda b, n: (b * nb + n, kv_col + 1 + kv))
    return pl.pallas_call(
        _swa_kernel,
        grid=(batch, nb),
        in_specs=[pl.BlockSpec(memory_space=pltpu.SMEM),
                  pl.BlockSpec((w, SWA_W), lambda b, n: (b * nb + n, 0)),
                  pl.BlockSpec((w, kvw), lambda b, n: (b * nb + jnp.maximum(n - 1, 0), kv_col)),
                  pl.BlockSpec((w, kvw), lambda b, n: (b * nb + n, kv_col))]
                 + [gate(kv) for kv in range(SWA_KV_HEADS)],
        out_specs=pl.BlockSpec((w, SWA_W), lambda b, n: (b * nb + n, 0)),
        out_shape=jax.ShapeDtypeStruct((batch * seq, SWA_W), BF16),
        compiler_params=_params(("parallel", "arbitrary")),
        name="swa_attention",
    )(sinks, hodd, hodd, hodd, *([hodd] * SWA_KV_HEADS))


def _rope_tables(seq, dim):
    inv = 1.0 / (ROPE_THETA ** (jnp.arange(0, dim, 2, dtype=F32) / dim))
    ang = jnp.arange(seq, dtype=F32)[:, None] * inv[None, :]
    cos, sin = jnp.cos(ang), jnp.sin(ang)
    reps = LANES // dim
    return (jnp.tile(jnp.concatenate([cos, cos], -1), (1, reps)),
            jnp.tile(jnp.concatenate([-sin, sin], -1), (1, reps)))


def _cols(w, start, size):
    return lax.slice_in_dim(w, start, start + size, axis=1)


def _even_layer(x2, batch, seq, g_norm, w_in, b_f, g_fox, g_dsa, g_kidx, w_out, tabs, topk):
    c128, s128, c64, s64 = tabs
    fox_w = 4 * FOX_W
    rest = fox_w + FOX_HEADS
    wide = 4 * DSA_W + IDX_W
    w_main = jnp.concatenate([_cols(w_in, 0, fox_w).astype(BF16), _cols(w_in, rest, wide).astype(BF16)], axis=1)
    pad = LANES - (IDX_DIM + IDX_HEADS + FOX_HEADS)
    w_small = jnp.concatenate([_cols(w_in, rest + wide, IDX_DIM + IDX_HEADS), _cols(w_in, fox_w, FOX_HEADS),
                               jnp.zeros((w_in.shape[0], pad), F32)], axis=1).astype(BF16)
    qscale = HEAD_DIM ** -0.5 * LOG2E
    one = jnp.ones((LANES,), F32)
    gains = jnp.stack([g_fox[0] * qscale, g_fox[1], one, one,
                       g_dsa[0] * qscale, g_dsa[1], one, one, one]).reshape(EV_IQ + 1, 1, LANES)
    gk = jnp.concatenate([g_kidx, jnp.zeros((LANES - IDX_DIM,), F32)]).reshape(1, LANES)
    bf = jnp.concatenate([jnp.zeros((SM_CF,), F32), b_f, jnp.zeros((pad,), F32)]).reshape(1, LANES)

    xn = _rmsnorm(x2, g_norm)
    hmain = _even_proj(xn, w_main, gains, c128, s128, c64, s64, seq)
    small, small_t = _even_small(xn, w_small, gk, bf, c64, s64, batch, seq)
    ya = _fox_attention(hmain, small, batch, seq)
    yb = _dsa_attention(hmain, small, small_t, batch, seq, topk)
    return _out_proj([ya, yb], w_out.astype(BF16), x2)


def _odd_layer(x2, batch, seq, g_norm, w_in, g_qk, sinks, w_out, tabs):
    _, _, c64, s64 = tabs
    qscale = jnp.array([[SWA_HEAD_DIM ** -0.5 * LOG2E], [1.0]], F32)
    gains = jnp.tile(g_qk * qscale, (1, LANES // SWA_HEAD_DIM)).reshape(2, 1, LANES)
    xn = _rmsnorm(x2, g_norm)
    hodd = _odd_proj(xn, w_in.astype(BF16), gains, c64, s64, seq)
    y = _swa_attention(hodd, sinks, batch, seq)
    return _out_proj([y], w_out.astype(BF16), x2)


def kernel(x, norm_even, w_in_even, b_f_even, g_qk_fox, g_qk_dsa, g_kidx, w_out_even,
           norm_odd, w_in_odd, g_qk_swa, sinks, w_out_odd):
    batch, seq, d = x.shape
    depth = norm_even.shape[0] + norm_odd.shape[0]
    topk = min(IDX_TOPK_MAX, seq // 4)
    tabs = _rope_tables(seq, HEAD_DIM) + _rope_tables(seq, IDX_DIM)
    x2 = x.reshape(batch * seq, d)
    for layer in range(depth):
        j = layer // 2
        if layer % 2 == 0:
            x2 = _even_layer(x2, batch, seq, norm_even[j], w_in_even[j], b_f_even[j], g_qk_fox[j],
                             g_qk_dsa[j], g_kidx[j], w_out_even[j], tabs, topk)
        else:
            x2 = _odd_layer(x2, batch, seq, norm_odd[j], w_in_odd[j], g_qk_swa[j], sinks[j],
                            w_out_odd[j], tabs)
    return x2.reshape(batch, seq, d)
```

```python
import functools

import jax
import jax.numpy as jnp
from jax import lax
from jax.experimental import pallas as pl
from jax.experimental.pallas import tpu as pltpu

F32 = jnp.float32
BF16 = jnp.bfloat16
I32 = jnp.int32

D_MODEL = 2048
HEAD_DIM = 128
FOX_HEADS = 8
DSA_HEADS = 8
IDX_HEADS = 16
IDX_DIM = 64
IDX_TOPK_MAX = 256
SWA_HEADS = 32
SWA_KV_HEADS = 4
SWA_HEAD_DIM = 64
SWA_WINDOW = 128
ROPE_THETA = 10000.0
EPS = 1e-6
NEG = -1e30

FOX_W = FOX_HEADS * HEAD_DIM
DSA_W = DSA_HEADS * HEAD_DIM
IDX_W = IDX_HEADS * IDX_DIM
SWA_W = SWA_HEADS * SWA_HEAD_DIM
SWA_KV_W = SWA_KV_HEADS * SWA_HEAD_DIM

LANES = 128
VMEM_LIMIT = 56 * 2 ** 20

SM_IK = 0
SM_IW = IDX_DIM
SM_CF = IDX_DIM + IDX_HEADS

LOG2E = 1.4426950408889634
INT_MIN = -2 ** 31
IDX_BITS = 12


def _params(sem):
    return pltpu.CompilerParams(dimension_semantics=sem, vmem_limit_bytes=VMEM_LIMIT)


def _dot(a, b):
    return jnp.dot(a, b, preferred_element_type=F32)


def _dot_nt(a, b):
    return lax.dot_general(a, b, (((1,), (1,)), ((), ())), preferred_element_type=F32)


def _lane_iota(shape):
    return lax.broadcasted_iota(I32, shape, 1)


def _rope(y, cos, sin, dim):
    if dim == LANES:
        rot = pltpu.roll(y, LANES // 2, 1)
    else:
        half = dim // 2
        first = (_lane_iota(y.shape) & half) == 0
        rot = jnp.where(first, pltpu.roll(y, LANES - half, 1), pltpu.roll(y, half, 1))
    return y * cos + rot * sin


def _head_rms(a, gain):
    return a * lax.rsqrt(jnp.mean(a * a, axis=-1, keepdims=True) + EPS) * gain


def _half_rms(a, gain):
    lo = _lane_iota(a.shape) < 64
    sq = a * a
    ms_lo = jnp.sum(jnp.where(lo, sq, 0.0), axis=-1, keepdims=True) * (1.0 / 64)
    ms_hi = jnp.sum(jnp.where(lo, 0.0, sq), axis=-1, keepdims=True) * (1.0 / 64)
    return a * lax.rsqrt(jnp.where(lo, ms_lo, ms_hi) + EPS) * gain


def _rmsnorm_kernel(x_ref, g_ref, o_ref):
    x = x_ref[...]
    y = x * lax.rsqrt(jnp.mean(x * x, axis=-1, keepdims=True) + EPS)
    o_ref[...] = (y * g_ref[...]).astype(o_ref.dtype)


def _rmsnorm(x2, g, tm=512):
    m, d = x2.shape
    return pl.pallas_call(
        _rmsnorm_kernel,
        grid=(m // tm,),
        in_specs=[pl.BlockSpec((tm, d), lambda i: (i, 0)), pl.BlockSpec((1, d), lambda i: (0, 0))],
        out_specs=pl.BlockSpec((tm, d), lambda i: (i, 0)),
        out_shape=jax.ShapeDtypeStruct((m, d), BF16),
        compiler_params=_params(("parallel",)),
        name="rmsnorm",
    )(x2, g.reshape(1, d))


def _skewed_proj_kernel(xn_ref, *rest, kinds, nw, nj, tn):
    w_refs, (*aux, o_ref, acc_ref) = rest[:nw], rest[nw:]
    t = pl.program_id(0)
    last = pl.num_programs(0) - 1
    j = (t + nj - 1) % nj

    def finish(parts):
        for fn, lo, hi in parts:
            for h in range(lo, hi):
                sl = slice(h * LANES, (h + 1) * LANES)
                o_ref[:, sl] = fn(acc_ref[:, sl]).astype(o_ref.dtype)

    def multiply(src):
        acc_ref[...] = _dot(xn_ref[...], w_refs[src][...].astype(BF16))

    @pl.when(t == 0)
    def _():
        multiply(0)

    nslab = tn // LANES
    for cond, parts, src in kinds(j, aux, nslab):
        @pl.when((t > 0) & (t < last) & cond)
        def _(parts=parts, src=src):
            finish(parts())
            multiply(src)

    for cond, parts, _ in kinds(nj - 1, aux, nslab):
        if cond:
            @pl.when(t == last)
            def _(parts=parts):
                finish(parts())


def _skewed_proj(xn, ws, w_cols, aux, aux_specs, kinds, seq, tn, nj, name, tm=1024):
    m, d = xn.shape
    ntiles = (m // tm) * nj
    nrow = seq // tm
    cur = lambda t: jnp.minimum(t, ntiles - 1)
    prev = lambda t: jnp.maximum(t - 1, 0)
    tab = pl.BlockSpec((tm, LANES), lambda t: ((prev(t) // nj) % nrow, 0))
    specs = [tab if s is None else s for s in aux_specs]
    wspec = lambda col: pl.BlockSpec((d, tn), lambda t: (0, col(cur(t) % nj)))
    return pl.pallas_call(
        functools.partial(_skewed_proj_kernel, kinds=kinds, nw=len(ws), nj=nj, tn=tn),
        grid=(ntiles + 1,),
        in_specs=[pl.BlockSpec((tm, d), lambda t: (cur(t) // nj, 0))] + [wspec(c) for c in w_cols] + specs,
        out_specs=pl.BlockSpec((tm, tn), lambda t: (prev(t) // nj, prev(t) % nj)),
        out_shape=jax.ShapeDtypeStruct((m, nj * tn), BF16),
        scratch_shapes=[pltpu.VMEM((tm, tn), F32)],
        compiler_params=_params(("arbitrary",)),
        name=name,
    )(xn, *ws, *aux)


EV_TN = 1024
EV_FQ, EV_FK, EV_FV, EV_FG, EV_DQ, EV_DK, EV_DV, EV_DG, EV_IQ = range(9)
EV_NFOX = 4


def _even_kinds(j, aux, nslab):
    gain_ref, c128_ref, s128_ref, c64_ref, s64_ref = aux
    gain = lambda: gain_ref[jnp.minimum(j, EV_IQ)]
    norm = lambda: [(lambda a: _head_rms(a, gain()), 0, nslab)]
    norm_rope = lambda: [(lambda a: _rope(_head_rms(a, gain()), c128_ref[...], s128_ref[...], HEAD_DIM), 0, nslab)]
    cast = lambda: [(lambda a: a, 0, nslab)]
    silu = lambda: [(lambda a: a * jax.nn.sigmoid(a), 0, nslab)]
    rope64 = lambda: [(lambda a: _rope(a, c64_ref[...], s64_ref[...], IDX_DIM), 0, nslab)]
    return [
        ((j == EV_FQ) | (j == EV_FK), norm, 0),
        (j == EV_FV, cast, 0),
        (j == EV_FG, silu, 1),
        ((j == EV_DQ) | (j == EV_DK), norm_rope, 1),
        (j == EV_DV, cast, 1),
        (j == EV_DG, silu, 1),
        (j == EV_IQ, rope64, 0),
    ]


def _even_proj(xn, w_fox, w_rest, gains, c128, s128, c64, s64, seq):
    gspec = pl.BlockSpec(gains.shape, lambda t: (0, 0, 0))
    cols = [lambda j: jnp.minimum(j, EV_NFOX - 1), lambda j: jnp.maximum(j - EV_NFOX, 0)]
    return _skewed_proj(xn, [w_fox, w_rest], cols, (gains, c128, s128, c64, s64),
                        [gspec, None, None, None, None], _even_kinds, seq, EV_TN, EV_IQ + 1, "even_proj")


def _even_small_kernel(xn_ref, w_ref, gk_ref, bf_ref, c64_ref, s64_ref, o_ref, ot_ref):
    h = _dot(xn_ref[...], w_ref[...])
    lane = _lane_iota(h.shape)
    is_ik = lane < SM_IW
    ms = jnp.sum(jnp.where(is_ik, h * h, 0.0), axis=-1, keepdims=True) * (1.0 / IDX_DIM)
    ik = _rope(h * lax.rsqrt(ms + EPS) * gk_ref[...], c64_ref[...], s64_ref[...], IDX_DIM)
    iw = h * (IDX_HEADS ** -0.5 * IDX_DIM ** -0.5)
    c = jax.nn.log_sigmoid(h + bf_ref[...])
    row = lax.broadcasted_iota(I32, h.shape, 0)
    d = 1
    while d < h.shape[0]:
        c = c + jnp.where(row >= d, pltpu.roll(c, d, 0), 0.0)
        d *= 2
    out = jnp.where(is_ik, ik, jnp.where(lane < SM_CF, iw, c))
    o_ref[...] = out
    ot_ref[0] = out.T


def _even_small(xn, w, gk, bf, c64, s64, batch, seq):
    d = xn.shape[1]
    vec = pl.BlockSpec((1, LANES), lambda b: (0, 0))
    tab = pl.BlockSpec((seq, LANES), lambda b: (0, 0))
    return pl.pallas_call(
        _even_small_kernel,
        grid=(batch,),
        in_specs=[pl.BlockSpec((seq, d), lambda b: (b, 0)),
                  pl.BlockSpec((d, LANES), lambda b: (0, 0)),
                  vec, vec, tab, tab],
        out_specs=[pl.BlockSpec((seq, LANES), lambda b: (b, 0)),
                   pl.BlockSpec((1, LANES, seq), lambda b: (b, 0, 0))],
        out_shape=[jax.ShapeDtypeStruct((batch * seq, LANES), F32),
                   jax.ShapeDtypeStruct((batch, LANES, seq), F32)],
        compiler_params=_params(("parallel",)),
        name="even_small",
    )(xn, w, gk, bf, c64, s64)


FOX_TQ = 256
FOX_HPS = 4
ONES_ROWS = 16


def _fox_kernel(q_ref, k_ref, v_ref, g_ref, small_ref, o_ref, ka_ref, vt_ref, *, nq):
    hg = pl.program_id(1)
    i = pl.program_id(2)
    tq = FOX_TQ
    seq = k_ref.shape[0]
    heads = [slice(u * HEAD_DIM, (u + 1) * HEAD_DIM) for u in range(FOX_HPS)]
    lane = _lane_iota((seq, LANES))

    @pl.when(i == 0)
    def _():
        sm = small_ref[...]
        for u, sl in enumerate(heads):
            vt_ref[u, 0:HEAD_DIM, :] = v_ref[:, sl].astype(F32).T.astype(BF16)
            vt_ref[u, HEAD_DIM:, :] = jnp.ones((ONES_ROWS, seq), BF16)
            ck = jnp.sum(jnp.where(lane == SM_CF + hg * FOX_HPS + u, sm, 0.0), axis=1, keepdims=True)
            neg = jnp.broadcast_to(ck * -LOG2E, (seq, LANES))
            hi = neg.astype(BF16).astype(F32)
            mid = (neg - hi).astype(BF16).astype(F32)
            lo = neg - hi - mid
            extra = jnp.where(lane == 0, hi, jnp.where(lane == 1, mid, jnp.where(lane == 2, lo, 0.0)))
            ka_ref[u, :, 0:HEAD_DIM] = k_ref[:, sl]
            ka_ref[u, :, HEAD_DIM:] = extra.astype(BF16)

    ones3 = jnp.where(_lane_iota((tq, LANES)) < 3, 1.0, 0.0).astype(BF16)
    qa = [jnp.concatenate([q_ref[:, sl], ones3], axis=1) for sl in heads]

    def qk(u, j):
        return _dot_nt(ka_ref[u, pl.ds(j * tq, tq), :], qa[u])

    def step(u, j, z, carry, diagonal):
        m, acc = carry
        if diagonal:
            key = lax.broadcasted_iota(I32, (tq, tq), 0)
            qry = lax.broadcasted_iota(I32, (tq, tq), 1)
            z = jnp.where(key <= qry, z, NEG)
        m_new = jnp.maximum(m, jnp.max(z, axis=0, keepdims=True))
        p = jnp.exp2(z - m_new)
        acc = jnp.exp2(m - m_new) * acc + _dot(vt_ref[u, :, pl.ds(j * tq, tq)], p.astype(BF16))
        return m_new, acc

    init = (jnp.full((1, tq), -jnp.inf, F32), jnp.zeros((HEAD_DIM + ONES_ROWS, tq), F32))
    for i_static in range(nq):
        @pl.when(i == i_static)
        def _(i_static=i_static):
            carry = [init] * FOX_HPS
            z_next = [qk(u, 0) for u in range(FOX_HPS)]
            for j in range(i_static + 1):
                for u in range(FOX_HPS):
                    z = z_next[u]
                    if j < i_static:
                        z_next[u] = qk(u, j + 1)
                    carry[u] = step(u, j, z, carry[u], j == i_static)
            for u, sl in enumerate(heads):
                _, acc = carry[u]
                out = (acc[0:HEAD_DIM] * (1.0 / acc[HEAD_DIM:HEAD_DIM + 1])).T
                o_ref[:, sl] = (out * g_ref[:, sl].astype(F32)).astype(o_ref.dtype)


def _fox_attention(hmain, small, batch, seq):
    tq = FOX_TQ
    nq = seq // tq
    width = FOX_HPS * HEAD_DIM
    ng = FOX_W // width
    return pl.pallas_call(
        functools.partial(_fox_kernel, nq=nq),
        grid=(batch, ng, nq),
        in_specs=[pl.BlockSpec((tq, width), lambda b, h, i: (b * nq + i, EV_FQ * ng + h)),
                  pl.BlockSpec((seq, width), lambda b, h, i: (b, EV_FK * ng + h)),
                  pl.BlockSpec((seq, width), lambda b, h, i: (b, EV_FV * ng + h)),
                  pl.BlockSpec((tq, width), lambda b, h, i: (b * nq + i, EV_FG * ng + h)),
                  pl.BlockSpec((seq, LANES), lambda b, h, i: (b, 0))],
        out_specs=pl.BlockSpec((tq, width), lambda b, h, i: (b * nq + i, h)),
        out_shape=jax.ShapeDtypeStruct((batch * seq, FOX_W), BF16),
        scratch_shapes=[pltpu.VMEM((FOX_HPS, seq, 2 * HEAD_DIM), BF16),
                        pltpu.VMEM((FOX_HPS, HEAD_DIM + ONES_ROWS, seq), BF16)],
        compiler_params=_params(("parallel", "parallel", "arbitrary")),
        name="fox_attention",
    )(hmain, hmain, hmain, hmain, small)


DSA_TQ = 256


def _key_to_f32(key):
    return pltpu.bitcast(key ^ ((key >> 31) & 0x7FFFFFFF), F32)


def _dsa_kernel(iq_ref, small_ref, smallt_ref, q_ref, k_ref, v_ref, g_ref, o_ref,
                score_ref, vt_ref, lim_ref, m_ref, acc_ref, *, topk):
    i = pl.program_id(1)
    tq = DSA_TQ
    nchunk = i + 1
    half = tq // 2
    key0 = lax.broadcasted_iota(I32, (tq, tq), 0)
    qry = i * tq + lax.broadcasted_iota(I32, (tq, tq), 1)

    @pl.when(i == 0)
    def _():
        for hh in range(DSA_HEADS):
            sl = slice(hh * HEAD_DIM, (hh + 1) * HEAD_DIM)
            vt_ref[hh, 0:HEAD_DIM, :] = v_ref[:, sl].astype(F32).T.astype(BF16)
            vt_ref[hh, HEAD_DIM:, :] = jnp.ones((ONES_ROWS, v_ref.shape[0]), BF16)

    def chunk_off(c):
        return pl.multiple_of(c * tq, tq)

    def score_chunk(c, _):
        off = chunk_off(c)
        ikc = small_ref[pl.ds(off, tq), SM_IK:SM_IK + IDX_DIM].astype(BF16)

        def logits(hh):
            return _dot_nt(ikc, iq_ref[:, hh * IDX_DIM:(hh + 1) * IDX_DIM])

        acc = jnp.zeros((tq, tq), F32)
        nxt = logits(0)
        for hh in range(IDX_HEADS):
            cur = nxt
            if hh + 1 < IDX_HEADS:
                nxt = logits(hh + 1)
            w = smallt_ref[0, SM_IW + hh:SM_IW + hh + 1, pl.ds(pl.multiple_of(i * tq, tq), tq)]
            acc = acc + w * jnp.maximum(cur, 0.0)
        score_ref[pl.ds(off, tq), :] = jnp.where(off + key0 <= qry, acc, NEG)
        return 0

    lax.fori_loop(0, nchunk, score_chunk, 0)

    def count(pred):
        def body(c, tot):
            off = chunk_off(c)
            hit = jnp.where(pred(score_ref[pl.ds(off, tq), :], off + key0), 1.0, 0.0)
            return tot + jnp.sum(hit.reshape(tq // 8, 8, tq), axis=0)
        tot = lax.fori_loop(0, nchunk, body, jnp.zeros((8, tq), F32))
        return jnp.sum(tot, axis=0, keepdims=True)

    def thr_step(it, res):
        cand = res + jnp.left_shift(jnp.int32(1), 31 - it)
        cand_f = _key_to_f32(cand)
        cnt = count(lambda sc, ki: sc >= cand_f)
        return jnp.where(cnt >= topk, cand, res)

    thr_key = lax.fori_loop(0, 32, thr_step, jnp.full((1, tq), INT_MIN, I32))
    thr = jnp.where(thr_key == INT_MIN, -jnp.inf, _key_to_f32(thr_key))
    n_gt = count(lambda sc, ki: sc > thr)
    n_ge = count(lambda sc, ki: sc >= thr)
    need = topk - n_gt

    lim_ref[...] = jnp.full((1, tq), 2 ** IDX_BITS, I32)

    @pl.when(jnp.max(n_ge) > topk)
    def _():
        def lim_step(it, res):
            cand = res + jnp.left_shift(jnp.int32(1), IDX_BITS - 1 - it)
            cnt = count(lambda sc, ki: (sc == thr) & (ki < cand))
            return jnp.where(cnt <= need, cand, res)
        lim_ref[...] = lax.fori_loop(0, IDX_BITS, lim_step, jnp.zeros((1, tq), I32))

    lim = lim_ref[...]

    m_ref[...] = jnp.full(m_ref.shape, -jnp.inf, F32)
    acc_ref[...] = jnp.zeros(acc_ref.shape, F32)

    def attn_chunk(c, _):
        off = chunk_off(c)
        sc = score_ref[pl.ds(off, tq), :]
        ki = off + key0
        sel = ((sc > thr) | ((sc == thr) & (ki < lim))) & (ki <= qry)
        bias = jnp.where(sel, 0.0, NEG)

        def qk(hh):
            sl = slice(hh * HEAD_DIM, (hh + 1) * HEAD_DIM)
            return _dot_nt(k_ref[pl.ds(off, tq), sl], q_ref[:, sl])

        heads = range(DSA_HEADS)
        zs = [bias + qk(hh) for hh in heads]
        ms = [m_ref[hh:hh + 1, :] for hh in heads]
        m_news = [jnp.maximum(ms[hh], jnp.max(zs[hh], axis=0, keepdims=True)) for hh in heads]
        ps = [jnp.exp2(zs[hh] - m_news[hh]).astype(BF16) for hh in heads]
        pvs = [_dot(vt_ref[hh, :, pl.ds(off, tq)], ps[hh]) for hh in heads]
        for hh in heads:
            acc_ref[hh] = jnp.exp2(ms[hh] - m_news[hh]) * acc_ref[hh] + pvs[hh]
            m_ref[hh:hh + 1, :] = m_news[hh]
        return 0

    lax.fori_loop(0, nchunk, attn_chunk, 0)
    for hh in range(DSA_HEADS):
        sl = slice(hh * HEAD_DIM, (hh + 1) * HEAD_DIM)
        out = (acc_ref[hh, 0:HEAD_DIM, :] * (1.0 / acc_ref[hh, HEAD_DIM:HEAD_DIM + 1, :])).T
        o_ref[:, sl] = (out * g_ref[:, sl].astype(F32)).astype(o_ref.dtype)


def _dsa_attention(hmain, small, small_t, batch, seq, topk):
    tq = DSA_TQ
    nq = seq // tq
    wide = lambda t: pl.BlockSpec((tq, DSA_W), lambda b, i: (b * nq + i, t))
    full = lambda t: pl.BlockSpec((seq, DSA_W), lambda b, i: (b, t))
    return pl.pallas_call(
        functools.partial(_dsa_kernel, topk=topk),
        grid=(batch, nq),
        in_specs=[wide(EV_IQ),
                  pl.BlockSpec((seq, LANES), lambda b, i: (b, 0)),
                  pl.BlockSpec((1, LANES, seq), lambda b, i: (b, 0, 0)),
                  wide(EV_DQ), full(EV_DK), full(EV_DV), wide(EV_DG)],
        out_specs=pl.BlockSpec((tq, DSA_W), lambda b, i: (b * nq + i, 0)),
        out_shape=jax.ShapeDtypeStruct((batch * seq, DSA_W), BF16),
        scratch_shapes=[pltpu.VMEM((seq, tq), F32),
                        pltpu.VMEM((DSA_HEADS, HEAD_DIM + ONES_ROWS, seq), BF16),
                        pltpu.VMEM((1, tq), I32),
                        pltpu.VMEM((DSA_HEADS, tq), F32),
                        pltpu.VMEM((DSA_HEADS, HEAD_DIM + ONES_ROWS, tq), F32)],
        compiler_params=_params(("parallel", "arbitrary")),
        name="dsa_attention",
    )(hmain, small, small_t, hmain, hmain, hmain, hmain)


def _out_proj_kernel(*refs, ny, normed):
    y_refs, w_ref, x_ref = refs[:ny], refs[ny], refs[ny + 1]
    if normed:
        g_ref, o_ref, n_ref = refs[ny + 2:]
    else:
        (o_ref,) = refs[ny + 2:]
    acc = x_ref[...]
    off = 0
    for y_ref in y_refs:
        kdim = y_ref.shape[1]
        acc = acc + _dot(y_ref[...], w_ref[off:off + kdim, :])
        off += kdim
    o_ref[...] = acc
    if normed:
        y = acc * lax.rsqrt(jnp.mean(acc * acc, axis=-1, keepdims=True) + EPS)
        n_ref[...] = (y * g_ref[...]).astype(n_ref.dtype)


def _out_proj(ys, w, x2, next_gain=None, tm=512):
    m, d = x2.shape
    row = pl.BlockSpec((tm, d), lambda i: (i, 0))
    normed = next_gain is not None
    extra = ([next_gain.reshape(1, d)], [pl.BlockSpec((1, d), lambda i: (0, 0))]) if normed else ([], [])
    out = pl.pallas_call(
        functools.partial(_out_proj_kernel, ny=len(ys), normed=normed),
        grid=(m // tm,),
        in_specs=[pl.BlockSpec((tm, y.shape[1]), lambda i: (i, 0)) for y in ys]
                 + [pl.BlockSpec(w.shape, lambda i: (0, 0)), row] + extra[1],
        out_specs=[row, row] if normed else [row],
        out_shape=[jax.ShapeDtypeStruct((m, d), F32)] + ([jax.ShapeDtypeStruct((m, d), BF16)] if normed else []),
        compiler_params=_params(("parallel",)),
        name="out_proj",
    )(*ys, w, x2, *extra[0])
    return (out[0], out[1]) if normed else (out[0], None)


OD_TN = 512
OD_NQ = SWA_W // OD_TN


def _odd_kinds(j, aux, nslab):
    gain_ref, c64_ref, s64_ref = aux
    qk = lambda idx: (lambda a: _rope(_half_rms(a, gain_ref[idx]), c64_ref[...], s64_ref[...], SWA_HEAD_DIM))
    nk = SWA_KV_W // LANES
    return [
        (j < OD_NQ, lambda: [(qk(0), 0, nslab)], 0),
        (j == OD_NQ, lambda: [(qk(1), 0, nk), (lambda a: a, nk, nslab)], 0),
        (j > OD_NQ, lambda: [(lambda a: a * jax.nn.sigmoid(a), 0, nslab)], 0),
    ]


def _odd_proj(xn, w, gains, c64, s64, seq):
    gspec = pl.BlockSpec((2, 1, LANES), lambda t: (0, 0, 0))
    return _skewed_proj(xn, [w], [lambda j: j], (gains, c64, s64), [gspec, None, None],
                        _odd_kinds, seq, OD_TN, w.shape[1] // OD_TN, "odd_proj")


def _swa_kernel(sink_ref, q_ref, kvp_ref, kvc_ref, *rest):
    *g_refs, o_ref = rest
    n = pl.program_id(1)
    w = SWA_WINDOW
    dh = SWA_HEAD_DIM
    group = SWA_HEADS // SWA_KV_HEADS
    gw = group * w
    kb = lax.broadcasted_iota(I32, (2 * w, gw), 0)
    qi = lax.broadcasted_iota(I32, (2 * w, gw), 1) & (w - 1)
    valid = (kb > qi) & (kb <= qi + w) & ((n > 0) | (kb >= w))
    bias = jnp.where(valid, 0.0, NEG)
    head_of_lane = lax.broadcasted_iota(I32, (1, gw), 1) // w
    ones = jnp.ones((ONES_ROWS, 2 * w), BF16)

    def band(col):
        sl = slice(col, col + dh)
        return jnp.concatenate([kvp_ref[:, sl], kvc_ref[:, sl]], axis=0)

    for kv in range(SWA_KV_HEADS):
        h0 = kv * group
        qg = jnp.concatenate([q_ref[:, (h0 + t) * dh:(h0 + t + 1) * dh] for t in range(group)], axis=0)
        z = bias + _dot_nt(band(kv * dh), qg)
        sink = jnp.zeros((1, gw), F32)
        for t in range(group):
            sink = jnp.where(head_of_lane == t, sink_ref[h0 + t] * LOG2E, sink)
        m = jnp.maximum(jnp.max(z, axis=0, keepdims=True), sink)
        vt = jnp.concatenate([band(SWA_KV_W + kv * dh).astype(F32).T.astype(BF16), ones], axis=0)
        pv = _dot(vt, jnp.exp2(z - m).astype(BF16))
        den = pv[dh:dh + 1] + jnp.exp2(sink - m)
        ot = (pv[0:dh] * (1.0 / den)).T
        o = jnp.concatenate([ot[t * w:(t + 1) * w] for t in range(group)], axis=1)
        sl = slice(h0 * dh, (h0 + group) * dh)
        o_ref[:, sl] = (o * g_refs[kv][...].astype(F32)).astype(o_ref.dtype)


def _swa_attention(hodd, sinks, batch, seq):
    w = SWA_WINDOW
    nb = seq // w
    kvw = 2 * SWA_KV_W
    kv_col = SWA_W // kvw
    gate = lambda kv: pl.BlockSpec((w, kvw), lambda b, n: (b * nb + n, kv_col + 1 + kv))
    return pl.pallas_call(
        _swa_kernel,
        grid=(batch, nb),
        in_specs=[pl.BlockSpec(memory_space=pltpu.SMEM),
                  pl.BlockSpec((w, SWA_W), lambda b, n: (b * nb + n, 0)),
                  pl.BlockSpec((w, kvw), lambda b, n: (b * nb + jnp.maximum(n - 1, 0), kv_col)),
                  pl.BlockSpec((w, kvw), lambda b, n: (b * nb + n, kv_col))]
                 + [gate(kv) for kv in range(SWA_KV_HEADS)],
        out_specs=pl.BlockSpec((w, SWA_W), lambda b, n: (b * nb + n, 0)),
        out_shape=jax.ShapeDtypeStruct((batch * seq, SWA_W), BF16),
        compiler_params=_params(("parallel", "arbitrary")),
        name="swa_attention",
    )(sinks, hodd, hodd, hodd, *([hodd] * SWA_KV_HEADS))


def _rope_tables(seq, dim):
    inv = 1.0 / (ROPE_THETA ** (jnp.arange(0, dim, 2, dtype=F32) / dim))
    ang = jnp.arange(seq, dtype=F32)[:, None] * inv[None, :]
    cos, sin = jnp.cos(ang), jnp.sin(ang)
    reps = LANES // dim
    return (jnp.tile(jnp.concatenate([cos, cos], -1), (1, reps)),
            jnp.tile(jnp.concatenate([-sin, sin], -1), (1, reps)))


def _cols(w, start, size):
    return lax.slice_in_dim(w, start, start + size, axis=1)


def _even_layer(x2, xn, next_gain, batch, seq, w_in, b_f, g_fox, g_dsa, g_kidx, w_out, tabs, topk):
    c128, s128, c64, s64 = tabs
    fox_w = 4 * FOX_W
    rest = fox_w + FOX_HEADS
    wide = 4 * DSA_W + IDX_W
    w_rest = _cols(w_in, rest, wide).astype(BF16)
    pad = LANES - (IDX_DIM + IDX_HEADS + FOX_HEADS)
    w_small = jnp.concatenate([_cols(w_in, rest + wide, IDX_DIM + IDX_HEADS), _cols(w_in, fox_w, FOX_HEADS),
                               jnp.zeros((w_in.shape[0], pad), F32)], axis=1).astype(BF16)
    qscale = HEAD_DIM ** -0.5 * LOG2E
    one = jnp.ones((LANES,), F32)
    gains = jnp.stack([g_fox[0] * qscale, g_fox[1], one, one,
                       g_dsa[0] * qscale, g_dsa[1], one, one, one]).reshape(EV_IQ + 1, 1, LANES)
    gk = jnp.concatenate([g_kidx, jnp.zeros((LANES - IDX_DIM,), F32)]).reshape(1, LANES)
    bf = jnp.concatenate([jnp.zeros((SM_CF,), F32), b_f, jnp.zeros((pad,), F32)]).reshape(1, LANES)

    hmain = _even_proj(xn, w_in, w_rest, gains, c128, s128, c64, s64, seq)
    small, small_t = _even_small(xn, w_small, gk, bf, c64, s64, batch, seq)
    ya = _fox_attention(hmain, small, batch, seq)
    yb = _dsa_attention(hmain, small, small_t, batch, seq, topk)
    return _out_proj([ya, yb], w_out.astype(BF16), x2, next_gain)


def _odd_layer(x2, xn, next_gain, batch, seq, w_in, g_qk, sinks, w_out, tabs):
    _, _, c64, s64 = tabs
    qscale = jnp.array([[SWA_HEAD_DIM ** -0.5 * LOG2E], [1.0]], F32)
    gains = jnp.tile(g_qk * qscale, (1, LANES // SWA_HEAD_DIM)).reshape(2, 1, LANES)
    hodd = _odd_proj(xn, w_in, gains, c64, s64, seq)
    y = _swa_attention(hodd, sinks, batch, seq)
    return _out_proj([y], w_out.astype(BF16), x2, next_gain)


def kernel(x, norm_even, w_in_even, b_f_even, g_qk_fox, g_qk_dsa, g_kidx, w_out_even,
           norm_odd, w_in_odd, g_qk_swa, sinks, w_out_odd):
    batch, seq, d = x.shape
    depth = norm_even.shape[0] + norm_odd.shape[0]
    topk = min(IDX_TOPK_MAX, seq // 4)
    tabs = _rope_tables(seq, HEAD_DIM) + _rope_tables(seq, IDX_DIM)
    x2 = x.reshape(batch * seq, d)
    pre_gain = lambda layer: (norm_even, norm_odd)[layer % 2][layer // 2]
    xn = _rmsnorm(x2, pre_gain(0))
    for layer in range(depth):
        j = layer // 2
        next_gain = pre_gain(layer + 1) if layer + 1 < depth else None
        if layer % 2 == 0:
            x2, xn = _even_layer(x2, xn, next_gain, batch, seq, w_in_even[j], b_f_even[j], g_qk_fox[j],
                                 g_qk_dsa[j], g_kidx[j], w_out_even[j], tabs, topk)
        else:
            x2, xn = _odd_layer(x2, xn, next_gain, batch, seq, w_in_odd[j], g_qk_swa[j], sinks[j],
                                w_out_odd[j], tabs)
    return x2.reshape(batch, seq, d)
```

```python
import functools

import jax
import jax.numpy as jnp
from jax import lax
from jax.experimental import pallas as pl
from jax.experimental.pallas import tpu as pltpu

F32 = jnp.float32
BF16 = jnp.bfloat16
I32 = jnp.int32

D_MODEL = 2048
HEAD_DIM = 128
FOX_HEADS = 8
DSA_HEADS = 8
IDX_HEADS = 16
IDX_DIM = 64
IDX_TOPK_MAX = 256
SWA_HEADS = 32
SWA_KV_HEADS = 4
SWA_HEAD_DIM = 64
SWA_WINDOW = 128
ROPE_THETA = 10000.0
EPS = 1e-6
NEG = -1e30

FOX_W = FOX_HEADS * HEAD_DIM
DSA_W = DSA_HEADS * HEAD_DIM
IDX_W = IDX_HEADS * IDX_DIM
SWA_W = SWA_HEADS * SWA_HEAD_DIM
SWA_KV_W = SWA_KV_HEADS * SWA_HEAD_DIM

LANES = 128
VMEM_LIMIT = 56 * 2 ** 20

SM_IK = 0
SM_IW = IDX_DIM
SM_CF = IDX_DIM + IDX_HEADS

LOG2E = 1.4426950408889634
INT_MIN = -2 ** 31
IDX_BITS = 12


def _params(sem):
    return pltpu.CompilerParams(dimension_semantics=sem, vmem_limit_bytes=VMEM_LIMIT)


def _dot(a, b):
    return jnp.dot(a, b, preferred_element_type=F32)


def _dot_nt(a, b):
    return lax.dot_general(a, b, (((1,), (1,)), ((), ())), preferred_element_type=F32)


def _lane_iota(shape):
    return lax.broadcasted_iota(I32, shape, 1)


def _rope(y, cos, sin, dim):
    if dim == LANES:
        rot = pltpu.roll(y, LANES // 2, 1)
    else:
        half = dim // 2
        first = (_lane_iota(y.shape) & half) == 0
        rot = jnp.where(first, pltpu.roll(y, LANES - half, 1), pltpu.roll(y, half, 1))
    return y * cos + rot * sin


def _head_rms(a, gain):
    return a * lax.rsqrt(jnp.mean(a * a, axis=-1, keepdims=True) + EPS) * gain


def _half_rms(a, gain):
    lo = _lane_iota(a.shape) < 64
    sq = a * a
    ms_lo = jnp.sum(jnp.where(lo, sq, 0.0), axis=-1, keepdims=True) * (1.0 / 64)
    ms_hi = jnp.sum(jnp.where(lo, 0.0, sq), axis=-1, keepdims=True) * (1.0 / 64)
    return a * lax.rsqrt(jnp.where(lo, ms_lo, ms_hi) + EPS) * gain


def _rmsnorm_kernel(x_ref, g_ref, o_ref):
    x = x_ref[...]
    y = x * lax.rsqrt(jnp.mean(x * x, axis=-1, keepdims=True) + EPS)
    o_ref[...] = (y * g_ref[...]).astype(o_ref.dtype)


def _rmsnorm(x2, g, tm=512):
    m, d = x2.shape
    return pl.pallas_call(
        _rmsnorm_kernel,
        grid=(m // tm,),
        in_specs=[pl.BlockSpec((tm, d), lambda i: (i, 0)), pl.BlockSpec((1, d), lambda i: (0, 0))],
        out_specs=pl.BlockSpec((tm, d), lambda i: (i, 0)),
        out_shape=jax.ShapeDtypeStruct((m, d), BF16),
        compiler_params=_params(("parallel",)),
        name="rmsnorm",
    )(x2, g.reshape(1, d))


def _skewed_proj_kernel(xn_ref, *rest, kinds, nw, nj, tn):
    w_refs, (*aux, o_ref, acc_ref) = rest[:nw], rest[nw:]
    t = pl.program_id(0)
    last = pl.num_programs(0) - 1
    j = (t + nj - 1) % nj

    def finish(parts):
        for fn, lo, hi in parts:
            for h in range(lo, hi):
                sl = slice(h * LANES, (h + 1) * LANES)
                o_ref[:, sl] = fn(acc_ref[:, sl]).astype(o_ref.dtype)

    def multiply(src):
        acc_ref[...] = _dot(xn_ref[...], w_refs[src][...].astype(BF16))

    @pl.when(t == 0)
    def _():
        multiply(0)

    nslab = tn // LANES
    for cond, parts, src in kinds(j, aux, nslab):
        @pl.when((t > 0) & (t < last) & cond)
        def _(parts=parts, src=src):
            finish(parts())
            multiply(src)

    for cond, parts, _ in kinds(nj - 1, aux, nslab):
        if cond:
            @pl.when(t == last)
            def _(parts=parts):
                finish(parts())


def _skewed_proj(xn, ws, w_cols, aux, aux_specs, kinds, seq, tn, nj, name, tm=1024):
    m, d = xn.shape
    ntiles = (m // tm) * nj
    nrow = seq // tm
    cur = lambda t: jnp.minimum(t, ntiles - 1)
    prev = lambda t: jnp.maximum(t - 1, 0)
    tab = pl.BlockSpec((tm, LANES), lambda t: ((prev(t) // nj) % nrow, 0))
    specs = [tab if s is None else s for s in aux_specs]

    def wspec(w, col):
        if isinstance(w, tuple):
            return pl.BlockSpec((None, d, tn), lambda t: (w[1], 0, col(cur(t) % nj)))
        return pl.BlockSpec((d, tn), lambda t: (0, col(cur(t) % nj)))

    return pl.pallas_call(
        functools.partial(_skewed_proj_kernel, kinds=kinds, nw=len(ws), nj=nj, tn=tn),
        grid=(ntiles + 1,),
        in_specs=[pl.BlockSpec((tm, d), lambda t: (cur(t) // nj, 0))] + [wspec(w, c) for w, c in zip(ws, w_cols)] + specs,
        out_specs=pl.BlockSpec((tm, tn), lambda t: (prev(t) // nj, prev(t) % nj)),
        out_shape=jax.ShapeDtypeStruct((m, nj * tn), BF16),
        scratch_shapes=[pltpu.VMEM((tm, tn), F32)],
        compiler_params=_params(("arbitrary",)),
        name=name,
    )(xn, *[w[0] if isinstance(w, tuple) else w for w in ws], *aux)


def _shift_cast_kernel(a_ref, b_ref, o_ref, *, shift):
    x = jnp.concatenate([a_ref[...], b_ref[...]], axis=1)
    o_ref[...] = pltpu.roll(x, x.shape[1] - shift, 1)[:, :o_ref.shape[1]].astype(o_ref.dtype)


def _shift_cast(w_stack, layer, start, ntile, tn, tr=512):
    d = w_stack.shape[1]
    base, shift = divmod(start, LANES)
    assert (base * LANES) % tn == 0 and 0 < shift < LANES
    tile0, per = base * LANES // tn, tn // LANES
    return pl.pallas_call(
        functools.partial(_shift_cast_kernel, shift=shift),
        grid=(ntile, d // tr),
        in_specs=[pl.BlockSpec((None, tr, tn), lambda k, r: (layer, r, tile0 + k)),
                  pl.BlockSpec((None, tr, LANES), lambda k, r: (layer, r, (tile0 + k + 1) * per))],
        out_specs=pl.BlockSpec((tr, tn), lambda k, r: (r, k)),
        out_shape=jax.ShapeDtypeStruct((d, ntile * tn), BF16),
        compiler_params=_params(("parallel", "parallel")),
        name="shift_cast",
    )(w_stack, w_stack)


EV_TN = 1024
EV_FQ, EV_FK, EV_FV, EV_FG, EV_DQ, EV_DK, EV_DV, EV_DG, EV_IQ = range(9)
EV_NFOX = 4


def _even_kinds(j, aux, nslab):
    gain_ref, c128_ref, s128_ref, c64_ref, s64_ref = aux
    gain = lambda: gain_ref[jnp.minimum(j, EV_IQ)]
    norm = lambda: [(lambda a: _head_rms(a, gain()), 0, nslab)]
    norm_rope = lambda: [(lambda a: _rope(_head_rms(a, gain()), c128_ref[...], s128_ref[...], HEAD_DIM), 0, nslab)]
    cast = lambda: [(lambda a: a, 0, nslab)]
    silu = lambda: [(lambda a: a * jax.nn.sigmoid(a), 0, nslab)]
    rope64 = lambda: [(lambda a: _rope(a, c64_ref[...], s64_ref[...], IDX_DIM), 0, nslab)]
    return [
        ((j == EV_FQ) | (j == EV_FK), norm, 0),
        (j == EV_FV, cast, 0),
        (j == EV_FG, silu, 1),
        ((j == EV_DQ) | (j == EV_DK), norm_rope, 1),
        (j == EV_DV, cast, 1),
        (j == EV_DG, silu, 1),
        (j == EV_IQ, rope64, 0),
    ]


def _even_proj(xn, w_fox, w_rest, gains, c128, s128, c64, s64, seq):
    gspec = pl.BlockSpec(gains.shape, lambda t: (0, 0, 0))
    cols = [lambda j: jnp.minimum(j, EV_NFOX - 1), lambda j: jnp.maximum(j - EV_NFOX, 0)]
    return _skewed_proj(xn, [w_fox, w_rest], cols, (gains, c128, s128, c64, s64),
                        [gspec, None, None, None, None], _even_kinds, seq, EV_TN, EV_IQ + 1, "even_proj")


def _even_small_kernel(xn_ref, w_ref, gk_ref, bf_ref, c64_ref, s64_ref, o_ref, ot_ref):
    h = _dot(xn_ref[...], w_ref[...])
    lane = _lane_iota(h.shape)
    is_ik = lane < SM_IW
    ms = jnp.sum(jnp.where(is_ik, h * h, 0.0), axis=-1, keepdims=True) * (1.0 / IDX_DIM)
    ik = _rope(h * lax.rsqrt(ms + EPS) * gk_ref[...], c64_ref[...], s64_ref[...], IDX_DIM)
    iw = h * (IDX_HEADS ** -0.5 * IDX_DIM ** -0.5)
    c = jax.nn.log_sigmoid(h + bf_ref[...])
    row = lax.broadcasted_iota(I32, h.shape, 0)
    d = 1
    while d < h.shape[0]:
        c = c + jnp.where(row >= d, pltpu.roll(c, d, 0), 0.0)
        d *= 2
    out = jnp.where(is_ik, ik, jnp.where(lane < SM_CF, iw, c))
    o_ref[...] = out
    ot_ref[0] = out.T


def _even_small(xn, w, gk, bf, c64, s64, batch, seq):
    d = xn.shape[1]
    vec = pl.BlockSpec((1, LANES), lambda b: (0, 0))
    tab = pl.BlockSpec((seq, LANES), lambda b: (0, 0))
    return pl.pallas_call(
        _even_small_kernel,
        grid=(batch,),
        in_specs=[pl.BlockSpec((seq, d), lambda b: (b, 0)),
                  pl.BlockSpec((d, LANES), lambda b: (0, 0)),
                  vec, vec, tab, tab],
        out_specs=[pl.BlockSpec((seq, LANES), lambda b: (b, 0)),
                   pl.BlockSpec((1, LANES, seq), lambda b: (b, 0, 0))],
        out_shape=[jax.ShapeDtypeStruct((batch * seq, LANES), F32),
                   jax.ShapeDtypeStruct((batch, LANES, seq), F32)],
        compiler_params=_params(("parallel",)),
        name="even_small",
    )(xn, w, gk, bf, c64, s64)


FOX_TQ = 256
FOX_HPS = 4
ONES_ROWS = 16


def _fox_kernel(q_ref, k_ref, v_ref, g_ref, small_ref, o_ref, ka_ref, vt_ref, *, nq):
    hg = pl.program_id(1)
    i = pl.program_id(2)
    tq = FOX_TQ
    seq = k_ref.shape[0]
    heads = [slice(u * HEAD_DIM, (u + 1) * HEAD_DIM) for u in range(FOX_HPS)]
    lane = _lane_iota((seq, LANES))

    @pl.when(i == 0)
    def _():
        sm = small_ref[...]
        for u, sl in enumerate(heads):
            vt_ref[u, 0:HEAD_DIM, :] = v_ref[:, sl].astype(F32).T.astype(BF16)
            vt_ref[u, HEAD_DIM:, :] = jnp.ones((ONES_ROWS, seq), BF16)
            ck = jnp.sum(jnp.where(lane == SM_CF + hg * FOX_HPS + u, sm, 0.0), axis=1, keepdims=True)
            neg = jnp.broadcast_to(ck * -LOG2E, (seq, LANES))
            hi = neg.astype(BF16).astype(F32)
            mid = (neg - hi).astype(BF16).astype(F32)
            lo = neg - hi - mid
            extra = jnp.where(lane == 0, hi, jnp.where(lane == 1, mid, jnp.where(lane == 2, lo, 0.0)))
            ka_ref[u, :, 0:HEAD_DIM] = k_ref[:, sl]
            ka_ref[u, :, HEAD_DIM:] = extra.astype(BF16)

    ones3 = jnp.where(_lane_iota((tq, LANES)) < 3, 1.0, 0.0).astype(BF16)
    qa = [jnp.concatenate([q_ref[:, sl], ones3], axis=1) for sl in heads]

    def qk(u, j):
        return _dot_nt(ka_ref[u, pl.ds(j * tq, tq), :], qa[u])

    def step(u, j, z, carry, diagonal):
        m, acc = carry
        if diagonal:
            key = lax.broadcasted_iota(I32, (tq, tq), 0)
            qry = lax.broadcasted_iota(I32, (tq, tq), 1)
            z = jnp.where(key <= qry, z, NEG)
        m_new = jnp.maximum(m, jnp.max(z, axis=0, keepdims=True))
        p = jnp.exp2(z - m_new)
        acc = jnp.exp2(m - m_new) * acc + _dot(vt_ref[u, :, pl.ds(j * tq, tq)], p.astype(BF16))
        return m_new, acc

    init = (jnp.full((1, tq), -jnp.inf, F32), jnp.zeros((HEAD_DIM + ONES_ROWS, tq), F32))
    for i_static in range(nq):
        @pl.when(i == i_static)
        def _(i_static=i_static):
            carry = [init] * FOX_HPS
            z_next = [qk(u, 0) for u in range(FOX_HPS)]
            for j in range(i_static + 1):
                for u in range(FOX_HPS):
                    z = z_next[u]
                    if j < i_static:
                        z_next[u] = qk(u, j + 1)
                    carry[u] = step(u, j, z, carry[u], j == i_static)
            for u, sl in enumerate(heads):
                _, acc = carry[u]
                out = (acc[0:HEAD_DIM] * (1.0 / acc[HEAD_DIM:HEAD_DIM + 1])).T
                o_ref[:, sl] = (out * g_ref[:, sl].astype(F32)).astype(o_ref.dtype)


def _fox_attention(hmain, small, batch, seq):
    tq = FOX_TQ
    nq = seq // tq
    width = FOX_HPS * HEAD_DIM
    ng = FOX_W // width
    return pl.pallas_call(
        functools.partial(_fox_kernel, nq=nq),
        grid=(batch, ng, nq),
        in_specs=[pl.BlockSpec((tq, width), lambda b, h, i: (b * nq + i, EV_FQ * ng + h)),
                  pl.BlockSpec((seq, width), lambda b, h, i: (b, EV_FK * ng + h)),
                  pl.BlockSpec((seq, width), lambda b, h, i: (b, EV_FV * ng + h)),
                  pl.BlockSpec((tq, width), lambda b, h, i: (b * nq + i, EV_FG * ng + h)),
                  pl.BlockSpec((seq, LANES), lambda b, h, i: (b, 0))],
        out_specs=pl.BlockSpec((tq, width), lambda b, h, i: (b * nq + i, h)),
        out_shape=jax.ShapeDtypeStruct((batch * seq, FOX_W), BF16),
        scratch_shapes=[pltpu.VMEM((FOX_HPS, seq, 2 * HEAD_DIM), BF16),
                        pltpu.VMEM((FOX_HPS, HEAD_DIM + ONES_ROWS, seq), BF16)],
        compiler_params=_params(("parallel", "parallel", "arbitrary")),
        name="fox_attention",
    )(hmain, hmain, hmain, hmain, small)


DSA_TQ = 256


def _key_to_f32(key):
    return pltpu.bitcast(key ^ ((key >> 31) & 0x7FFFFFFF), F32)


def _dsa_kernel(iq_ref, small_ref, smallt_ref, q_ref, k_ref, v_ref, g_ref, o_ref,
                score_ref, vt_ref, lim_ref, m_ref, acc_ref, *, topk):
    i = pl.program_id(1)
    tq = DSA_TQ
    nchunk = i + 1
    half = tq // 2
    key0 = lax.broadcasted_iota(I32, (tq, tq), 0)
    qry = i * tq + lax.broadcasted_iota(I32, (tq, tq), 1)

    @pl.when(i == 0)
    def _():
        for hh in range(DSA_HEADS):
            sl = slice(hh * HEAD_DIM, (hh + 1) * HEAD_DIM)
            vt_ref[hh, 0:HEAD_DIM, :] = v_ref[:, sl].astype(F32).T.astype(BF16)
            vt_ref[hh, HEAD_DIM:, :] = jnp.ones((ONES_ROWS, v_ref.shape[0]), BF16)

    def chunk_off(c):
        return pl.multiple_of(c * tq, tq)

    def score_chunk(c, _):
        off = chunk_off(c)
        ikc = small_ref[pl.ds(off, tq), SM_IK:SM_IK + IDX_DIM].astype(BF16)

        def logits(hh):
            return _dot_nt(ikc, iq_ref[:, hh * IDX_DIM:(hh + 1) * IDX_DIM])

        acc = jnp.zeros((tq, tq), F32)
        nxt = logits(0)
        for hh in range(IDX_HEADS):
            cur = nxt
            if hh + 1 < IDX_HEADS:
                nxt = logits(hh + 1)
            w = smallt_ref[0, SM_IW + hh:SM_IW + hh + 1, pl.ds(pl.multiple_of(i * tq, tq), tq)]
            acc = acc + w * jnp.maximum(cur, 0.0)
        score_ref[pl.ds(off, tq), :] = jnp.where(off + key0 <= qry, acc, NEG)
        return 0

    lax.fori_loop(0, nchunk, score_chunk, 0)

    def count(pred):
        def body(c, tot):
            off = chunk_off(c)
            hit = jnp.where(pred(score_ref[pl.ds(off, tq), :], off + key0), 1.0, 0.0)
            return tot + jnp.sum(hit.reshape(tq // 8, 8, tq), axis=0)
        tot = lax.fori_loop(0, nchunk, body, jnp.zeros((8, tq), F32))
        return jnp.sum(tot, axis=0, keepdims=True)

    def thr_step(it, res):
        cand = res + jnp.left_shift(jnp.int32(1), 31 - it)
        cand_f = _key_to_f32(cand)
        cnt = count(lambda sc, ki: sc >= cand_f)
        return jnp.where(cnt >= topk, cand, res)

    thr_key = lax.fori_loop(0, 32, thr_step, jnp.full((1, tq), INT_MIN, I32))
    thr = jnp.where(thr_key == INT_MIN, -jnp.inf, _key_to_f32(thr_key))
    n_gt = count(lambda sc, ki: sc > thr)
    n_ge = count(lambda sc, ki: sc >= thr)
    need = topk - n_gt

    lim_ref[...] = jnp.full((1, tq), 2 ** IDX_BITS, I32)

    @pl.when(jnp.max(n_ge) > topk)
    def _():
        def lim_step(it, res):
            cand = res + jnp.left_shift(jnp.int32(1), IDX_BITS - 1 - it)
            cnt = count(lambda sc, ki: (sc == thr) & (ki < cand))
            return jnp.where(cnt <= need, cand, res)
        lim_ref[...] = lax.fori_loop(0, IDX_BITS, lim_step, jnp.zeros((1, tq), I32))

    lim = lim_ref[...]

    m_ref[...] = jnp.full(m_ref.shape, -jnp.inf, F32)
    acc_ref[...] = jnp.zeros(acc_ref.shape, F32)

    def attn_chunk(c, _):
        off = chunk_off(c)
        sc = score_ref[pl.ds(off, tq), :]
        ki = off + key0
        sel = ((sc > thr) | ((sc == thr) & (ki < lim))) & (ki <= qry)
        bias = jnp.where(sel, 0.0, NEG)

        def qk(hh):
            sl = slice(hh * HEAD_DIM, (hh + 1) * HEAD_DIM)
            return _dot_nt(k_ref[pl.ds(off, tq), sl], q_ref[:, sl])

        heads = range(DSA_HEADS)
        zs = [bias + qk(hh) for hh in heads]
        ms = [m_ref[hh:hh + 1, :] for hh in heads]
        m_news = [jnp.maximum(ms[hh], jnp.max(zs[hh], axis=0, keepdims=True)) for hh in heads]
        ps = [jnp.exp2(zs[hh] - m_news[hh]).astype(BF16) for hh in heads]
        pvs = [_dot(vt_ref[hh, :, pl.ds(off, tq)], ps[hh]) for hh in heads]
        for hh in heads:
            acc_ref[hh] = jnp.exp2(ms[hh] - m_news[hh]) * acc_ref[hh] + pvs[hh]
            m_ref[hh:hh + 1, :] = m_news[hh]
        return 0

    lax.fori_loop(0, nchunk, attn_chunk, 0)
    for hh in range(DSA_HEADS):
        sl = slice(hh * HEAD_DIM, (hh + 1) * HEAD_DIM)
        out = (acc_ref[hh, 0:HEAD_DIM, :] * (1.0 / acc_ref[hh, HEAD_DIM:HEAD_DIM + 1, :])).T
        o_ref[:, sl] = (out * g_ref[:, sl].astype(F32)).astype(o_ref.dtype)


def _dsa_attention(hmain, small, small_t, batch, seq, topk):
    tq = DSA_TQ
    nq = seq // tq
    wide = lambda t: pl.BlockSpec((tq, DSA_W), lambda b, i: (b * nq + i, t))
    full = lambda t: pl.BlockSpec((seq, DSA_W), lambda b, i: (b, t))
    return pl.pallas_call(
        functools.partial(_dsa_kernel, topk=topk),
        grid=(batch, nq),
        in_specs=[wide(EV_IQ),
                  pl.BlockSpec((seq, LANES), lambda b, i: (b, 0)),
                  pl.BlockSpec((1, LANES, seq), lambda b, i: (b, 0, 0)),
                  wide(EV_DQ), full(EV_DK), full(EV_DV), wide(EV_DG)],
        out_specs=pl.BlockSpec((tq, DSA_W), lambda b, i: (b * nq + i, 0)),
        out_shape=jax.ShapeDtypeStruct((batch * seq, DSA_W), BF16),
        scratch_shapes=[pltpu.VMEM((seq, tq), F32),
                        pltpu.VMEM((DSA_HEADS, HEAD_DIM + ONES_ROWS, seq), BF16),
                        pltpu.VMEM((1, tq), I32),
                        pltpu.VMEM((DSA_HEADS, tq), F32),
                        pltpu.VMEM((DSA_HEADS, HEAD_DIM + ONES_ROWS, tq), F32)],
        compiler_params=_params(("parallel", "arbitrary")),
        name="dsa_attention",
    )(hmain, small, small_t, hmain, hmain, hmain, hmain)


def _out_proj_kernel(*refs, ny, normed):
    y_refs, w_ref, x_ref = refs[:ny], refs[ny], refs[ny + 1]
    if normed:
        g_ref, o_ref, n_ref = refs[ny + 2:]
    else:
        (o_ref,) = refs[ny + 2:]
    acc = x_ref[...]
    off = 0
    for y_ref in y_refs:
        kdim = y_ref.shape[1]
        acc = acc + _dot(y_ref[...], w_ref[off:off + kdim, :])
        off += kdim
    o_ref[...] = acc
    if normed:
        y = acc * lax.rsqrt(jnp.mean(acc * acc, axis=-1, keepdims=True) + EPS)
        n_ref[...] = (y * g_ref[...]).astype(n_ref.dtype)


def _out_proj(ys, w, x2, next_gain=None, tm=512):
    m, d = x2.shape
    row = pl.BlockSpec((tm, d), lambda i: (i, 0))
    normed = next_gain is not None
    extra = ([next_gain.reshape(1, d)], [pl.BlockSpec((1, d), lambda i: (0, 0))]) if normed else ([], [])
    out = pl.pallas_call(
        functools.partial(_out_proj_kernel, ny=len(ys), normed=normed),
        grid=(m // tm,),
        in_specs=[pl.BlockSpec((tm, y.shape[1]), lambda i: (i, 0)) for y in ys]
                 + [pl.BlockSpec(w.shape, lambda i: (0, 0)), row] + extra[1],
        out_specs=[row, row] if normed else [row],
        out_shape=[jax.ShapeDtypeStruct((m, d), F32)] + ([jax.ShapeDtypeStruct((m, d), BF16)] if normed else []),
        compiler_params=_params(("parallel",)),
        name="out_proj",
    )(*ys, w, x2, *extra[0])
    return (out[0], out[1]) if normed else (out[0], None)


OD_TN = 512
OD_NQ = SWA_W // OD_TN


def _odd_kinds(j, aux, nslab):
    gain_ref, c64_ref, s64_ref = aux
    qk = lambda idx: (lambda a: _rope(_half_rms(a, gain_ref[idx]), c64_ref[...], s64_ref[...], SWA_HEAD_DIM))
    nk = SWA_KV_W // LANES
    return [
        (j < OD_NQ, lambda: [(qk(0), 0, nslab)], 0),
        (j == OD_NQ, lambda: [(qk(1), 0, nk), (lambda a: a, nk, nslab)], 0),
        (j > OD_NQ, lambda: [(lambda a: a * jax.nn.sigmoid(a), 0, nslab)], 0),
    ]


def _odd_proj(xn, w, layer, gains, c64, s64, seq):
    gspec = pl.BlockSpec((2, 1, LANES), lambda t: (0, 0, 0))
    return _skewed_proj(xn, [(w, layer)], [lambda j: j], (gains, c64, s64), [gspec, None, None],
                        _odd_kinds, seq, OD_TN, w.shape[2] // OD_TN, "odd_proj")


def _swa_kernel(sink_ref, q_ref, kvp_ref, kvc_ref, *rest):
    *g_refs, o_ref = rest
    n = pl.program_id(1)
    w = SWA_WINDOW
    dh = SWA_HEAD_DIM
    group = SWA_HEADS // SWA_KV_HEADS
    gw = group * w
    kb = lax.broadcasted_iota(I32, (2 * w, gw), 0)
    qi = lax.broadcasted_iota(I32, (2 * w, gw), 1) & (w - 1)
    valid = (kb > qi) & (kb <= qi + w) & ((n > 0) | (kb >= w))
    bias = jnp.where(valid, 0.0, NEG)
    head_of_lane = lax.broadcasted_iota(I32, (1, gw), 1) // w
    ones = jnp.ones((ONES_ROWS, 2 * w), BF16)

    def band(col):
        sl = slice(col, col + dh)
        return jnp.concatenate([kvp_ref[:, sl], kvc_ref[:, sl]], axis=0)

    for kv in range(SWA_KV_HEADS):
        h0 = kv * group
        qg = jnp.concatenate([q_ref[:, (h0 + t) * dh:(h0 + t + 1) * dh] for t in range(group)], axis=0)
        z = bias + _dot_nt(band(kv * dh), qg)
        sink = jnp.zeros((1, gw), F32)
        for t in range(group):
            sink = jnp.where(head_of_lane == t, sink_ref[h0 + t] * LOG2E, sink)
        m = jnp.maximum(jnp.max(z, axis=0, keepdims=True), sink)
        vt = jnp.concatenate([band(SWA_KV_W + kv * dh).astype(F32).T.astype(BF16), ones], axis=0)
        pv = _dot(vt, jnp.exp2(z - m).astype(BF16))
        den = pv[dh:dh + 1] + jnp.exp2(sink - m)
        ot = (pv[0:dh] * (1.0 / den)).T
        o = jnp.concatenate([ot[t * w:(t + 1) * w] for t in range(group)], axis=1)
        sl = slice(h0 * dh, (h0 + group) * dh)
        o_ref[:, sl] = (o * g_refs[kv][...].astype(F32)).astype(o_ref.dtype)


def _swa_attention(hodd, sinks, batch, seq):
    w = SWA_WINDOW
    nb = seq // w
    kvw = 2 * SWA_KV_W
    kv_col = SWA_W // kvw
    gate = lambda kv: pl.BlockSpec((w, kvw), lambda b, n: (b * nb + n, kv_col + 1 + kv))
    return pl.pallas_call(
        _swa_kernel,
        grid=(batch, nb),
        in_specs=[pl.BlockSpec(memory_space=pltpu.SMEM),
                  pl.BlockSpec((w, SWA_W), lambda b, n: (b * nb + n, 0)),
                  pl.BlockSpec((w, kvw), lambda b, n: (b * nb + jnp.maximum(n - 1, 0), kv_col)),
                  pl.BlockSpec((w, kvw), lambda b, n: (b * nb + n, kv_col))]
                 + [gate(kv) for kv in range(SWA_KV_HEADS)],
        out_specs=pl.BlockSpec((w, SWA_W), lambda b, n: (b * nb + n, 0)),
        out_shape=jax.ShapeDtypeStruct((batch * seq, SWA_W), BF16),
        compiler_params=_params(("parallel", "arbitrary")),
        name="swa_attention",
    )(sinks, hodd, hodd, hodd, *([hodd] * SWA_KV_HEADS))


def _rope_tables(seq, dim):
    inv = 1.0 / (ROPE_THETA ** (jnp.arange(0, dim, 2, dtype=F32) / dim))
    ang = jnp.arange(seq, dtype=F32)[:, None] * inv[None, :]
    cos, sin = jnp.cos(ang), jnp.sin(ang)
    reps = LANES // dim
    return (jnp.tile(jnp.concatenate([cos, cos], -1), (1, reps)),
            jnp.tile(jnp.concatenate([-sin, sin], -1), (1, reps)))


def _cols(w, start, size):
    return lax.slice_in_dim(w, start, start + size, axis=1)


def _even_layer(x2, xn, next_gain, batch, seq, w_in_stack, j, b_f, g_fox, g_dsa, g_kidx, w_out, tabs, topk):
    c128, s128, c64, s64 = tabs
    w_in = w_in_stack[j]
    fox_w = 4 * FOX_W
    rest = fox_w + FOX_HEADS
    wide = 4 * DSA_W + IDX_W
    w_rest = _shift_cast(w_in_stack, j, rest, wide // EV_TN, EV_TN)
    pad = LANES - (IDX_DIM + IDX_HEADS + FOX_HEADS)
    w_small = jnp.concatenate([_cols(w_in, rest + wide, IDX_DIM + IDX_HEADS), _cols(w_in, fox_w, FOX_HEADS),
                               jnp.zeros((w_in.shape[0], pad), F32)], axis=1).astype(BF16)
    qscale = HEAD_DIM ** -0.5 * LOG2E
    one = jnp.ones((LANES,), F32)
    gains = jnp.stack([g_fox[0] * qscale, g_fox[1], one, one,
                       g_dsa[0] * qscale, g_dsa[1], one, one, one]).reshape(EV_IQ + 1, 1, LANES)
    gk = jnp.concatenate([g_kidx, jnp.zeros((LANES - IDX_DIM,), F32)]).reshape(1, LANES)
    bf = jnp.concatenate([jnp.zeros((SM_CF,), F32), b_f, jnp.zeros((pad,), F32)]).reshape(1, LANES)

    hmain = _even_proj(xn, (w_in_stack, j), w_rest, gains, c128, s128, c64, s64, seq)
    small, small_t = _even_small(xn, w_small, gk, bf, c64, s64, batch, seq)
    ya = _fox_attention(hmain, small, batch, seq)
    yb = _dsa_attention(hmain, small, small_t, batch, seq, topk)
    return _out_proj([ya, yb], w_out.astype(BF16), x2, next_gain)


def _odd_layer(x2, xn, next_gain, batch, seq, w_in_stack, j, g_qk, sinks, w_out, tabs):
    _, _, c64, s64 = tabs
    qscale = jnp.array([[SWA_HEAD_DIM ** -0.5 * LOG2E], [1.0]], F32)
    gains = jnp.tile(g_qk * qscale, (1, LANES // SWA_HEAD_DIM)).reshape(2, 1, LANES)
    hodd = _odd_proj(xn, w_in_stack, j, gains, c64, s64, seq)
    y = _swa_attention(hodd, sinks, batch, seq)
    return _out_proj([y], w_out.astype(BF16), x2, next_gain)


def kernel(x, norm_even, w_in_even, b_f_even, g_qk_fox, g_qk_dsa, g_kidx, w_out_even,
           norm_odd, w_in_odd, g_qk_swa, sinks, w_out_odd):
    batch, seq, d = x.shape
    depth = norm_even.shape[0] + norm_odd.shape[0]
    topk = min(IDX_TOPK_MAX, seq // 4)
    tabs = _rope_tables(seq, HEAD_DIM) + _rope_tables(seq, IDX_DIM)
    x2 = x.reshape(batch * seq, d)
    pre_gain = lambda layer: (norm_even, norm_odd)[layer % 2][layer // 2]
    xn = _rmsnorm(x2, pre_gain(0))
    for layer in range(depth):
        j = layer // 2
        next_gain = pre_gain(layer + 1) if layer + 1 < depth else None
        if layer % 2 == 0:
            x2, xn = _even_layer(x2, xn, next_gain, batch, seq, w_in_even, j, b_f_even[j], g_qk_fox[j],
                                 g_qk_dsa[j], g_kidx[j], w_out_even[j], tabs, topk)
        else:
            x2, xn = _odd_layer(x2, xn, next_gain, batch, seq, w_in_odd, j, g_qk_swa[j], sinks[j],
                                w_out_odd[j], tabs)
    return x2.reshape(batch, seq, d)
```

```python
import functools

import jax
import jax.numpy as jnp
from jax import lax
from jax.experimental import pallas as pl
from jax.experimental.pallas import tpu as pltpu

F32 = jnp.float32
BF16 = jnp.bfloat16
I32 = jnp.int32

D_MODEL = 2048
HEAD_DIM = 128
FOX_HEADS = 8
DSA_HEADS = 8
IDX_HEADS = 16
IDX_DIM = 64
IDX_TOPK_MAX = 256
SWA_HEADS = 32
SWA_KV_HEADS = 4
SWA_HEAD_DIM = 64
SWA_WINDOW = 128
ROPE_THETA = 10000.0
EPS = 1e-6
NEG = -1e30

FOX_W = FOX_HEADS * HEAD_DIM
DSA_W = DSA_HEADS * HEAD_DIM
IDX_W = IDX_HEADS * IDX_DIM
SWA_W = SWA_HEADS * SWA_HEAD_DIM
SWA_KV_W = SWA_KV_HEADS * SWA_HEAD_DIM

LANES = 128
VMEM_LIMIT = 56 * 2 ** 20

SM_IK = 0
SM_IW = IDX_DIM
SM_CF = IDX_DIM + IDX_HEADS

LOG2E = 1.4426950408889634
INT_MIN = -2 ** 31
IDX_BITS = 12


def _params(sem):
    return pltpu.CompilerParams(dimension_semantics=sem, vmem_limit_bytes=VMEM_LIMIT)


def _dot(a, b):
    return jnp.dot(a, b, preferred_element_type=F32)


def _dot_nt(a, b):
    return lax.dot_general(a, b, (((1,), (1,)), ((), ())), preferred_element_type=F32)


def _lane_iota(shape):
    return lax.broadcasted_iota(I32, shape, 1)


def _rope(y, cos, sin, dim):
    if dim == LANES:
        rot = pltpu.roll(y, LANES // 2, 1)
    else:
        half = dim // 2
        first = (_lane_iota(y.shape) & half) == 0
        rot = jnp.where(first, pltpu.roll(y, LANES - half, 1), pltpu.roll(y, half, 1))
    return y * cos + rot * sin


def _head_rms(a, gain):
    return a * lax.rsqrt(jnp.mean(a * a, axis=-1, keepdims=True) + EPS) * gain


def _half_rms(a, gain):
    lo = _lane_iota(a.shape) < 64
    sq = a * a
    ms_lo = jnp.sum(jnp.where(lo, sq, 0.0), axis=-1, keepdims=True) * (1.0 / 64)
    ms_hi = jnp.sum(jnp.where(lo, 0.0, sq), axis=-1, keepdims=True) * (1.0 / 64)
    return a * lax.rsqrt(jnp.where(lo, ms_lo, ms_hi) + EPS) * gain


def _rmsnorm_kernel(x_ref, g_ref, o_ref):
    x = x_ref[...]
    y = x * lax.rsqrt(jnp.mean(x * x, axis=-1, keepdims=True) + EPS)
    o_ref[...] = (y * g_ref[...]).astype(o_ref.dtype)


def _rmsnorm(x2, g, tm=512):
    m, d = x2.shape
    return pl.pallas_call(
        _rmsnorm_kernel,
        grid=(m // tm,),
        in_specs=[pl.BlockSpec((tm, d), lambda i: (i, 0)), pl.BlockSpec((1, d), lambda i: (0, 0))],
        out_specs=pl.BlockSpec((tm, d), lambda i: (i, 0)),
        out_shape=jax.ShapeDtypeStruct((m, d), BF16),
        compiler_params=_params(("parallel",)),
        name="rmsnorm",
    )(x2, g.reshape(1, d))


def _skewed_proj_kernel(xn_ref, w_ref, *rest, kinds, transposed, nj, tn):
    *aux, o_ref, acc_ref = rest
    t = pl.program_id(0)
    last = pl.num_programs(0) - 1
    j = (t + nj - 1) % nj

    def finish(parts):
        for fn, lo, hi in parts:
            for h in range(lo, hi):
                sl = slice(h * LANES, (h + 1) * LANES)
                o_ref[:, sl] = fn(acc_ref[:, sl]).astype(o_ref.dtype)

    def multiply():
        w = (w_ref[...] if transposed else w_ref[0]).astype(BF16)
        acc_ref[...] = _dot_nt(xn_ref[...], w) if transposed else _dot(xn_ref[...], w)

    @pl.when(t == 0)
    def _():
        multiply()

    nslab = tn // LANES
    for cond, parts in kinds(j, aux, nslab):
        @pl.when((t > 0) & (t < last) & cond)
        def _(parts=parts):
            finish(parts())
            multiply()

    for cond, parts in kinds(nj - 1, aux, nslab):
        if cond:
            @pl.when(t == last)
            def _(parts=parts):
                finish(parts())


def _skewed_proj(xn, w_stack, layer, transposed, locate, aux, aux_specs, kinds, seq, tn, nj, name, tm=1024):
    m, d = xn.shape
    ntiles = (m // tm) * nj
    nrow = seq // tm
    cur = lambda t: jnp.minimum(t, ntiles - 1)
    prev = lambda t: jnp.maximum(t - 1, 0)
    tab = pl.BlockSpec((tm, LANES), lambda t: ((prev(t) // nj) % nrow, 0))
    specs = [tab if s is None else s for s in aux_specs]
    if transposed:
        wspec = pl.BlockSpec((pl.Element(tn), pl.Element(d)), lambda t: (locate(cur(t) % nj), 0))
    else:
        wspec = pl.BlockSpec((1, d, tn), lambda t: (layer, 0, locate(cur(t) % nj)))
    return pl.pallas_call(
        functools.partial(_skewed_proj_kernel, kinds=kinds, transposed=transposed, nj=nj, tn=tn),
        grid=(ntiles + 1,),
        in_specs=[pl.BlockSpec((tm, d), lambda t: (cur(t) // nj, 0)), wspec] + specs,
        out_specs=pl.BlockSpec((tm, tn), lambda t: (prev(t) // nj, prev(t) % nj)),
        out_shape=jax.ShapeDtypeStruct((m, nj * tn), BF16),
        scratch_shapes=[pltpu.VMEM((tm, tn), F32)],
        compiler_params=_params(("arbitrary",)),
        name=name,
    )(xn, w_stack, *aux)


EV_TN = 1024
EV_FQ, EV_FK, EV_FV, EV_FG, EV_DQ, EV_DK, EV_DV, EV_DG, EV_IQ = range(9)
EV_NFOX = 4


def _even_kinds(j, aux, nslab):
    gain_ref, c128_ref, s128_ref, c64_ref, s64_ref = aux
    gain = lambda: gain_ref[jnp.minimum(j, EV_IQ)]
    norm = lambda: [(lambda a: _head_rms(a, gain()), 0, nslab)]
    norm_rope = lambda: [(lambda a: _rope(_head_rms(a, gain()), c128_ref[...], s128_ref[...], HEAD_DIM), 0, nslab)]
    cast = lambda: [(lambda a: a, 0, nslab)]
    silu = lambda: [(lambda a: a * jax.nn.sigmoid(a), 0, nslab)]
    rope64 = lambda: [(lambda a: _rope(a, c64_ref[...], s64_ref[...], IDX_DIM), 0, nslab)]
    return [
        ((j == EV_FQ) | (j == EV_FK), norm),
        ((j == EV_DQ) | (j == EV_DK), norm_rope),
        ((j == EV_FV) | (j == EV_DV), cast),
        ((j == EV_FG) | (j == EV_DG), silu),
        (j == EV_IQ, rope64),
    ]


def _even_proj(xn, wt, row0, gains, c128, s128, c64, s64, seq):
    gspec = pl.BlockSpec(gains.shape, lambda t: (0, 0, 0))
    row = lambda j: (row0 // 8 + j * (EV_TN // 8) + (j + EV_NFOX) // (2 * EV_NFOX)) * 8
    return _skewed_proj(xn, wt, None, True, row, (gains, c128, s128, c64, s64),
                        [gspec, None, None, None, None], _even_kinds, seq, EV_TN, EV_IQ + 1, "even_proj")


def _even_small_kernel(xn_ref, wi_ref, wf_ref, gk_ref, bf_ref, c64_ref, s64_ref, o_ref, ot_ref):
    d = xn_ref.shape[1]
    pad = jnp.zeros((LANES - wi_ref.shape[0] - wf_ref.shape[0], d), F32)
    wt = jnp.concatenate([wi_ref[...], wf_ref[...], pad], axis=0).astype(BF16)
    h = _dot_nt(xn_ref[...], wt)
    lane = _lane_iota(h.shape)
    is_ik = lane < SM_IW
    ms = jnp.sum(jnp.where(is_ik, h * h, 0.0), axis=-1, keepdims=True) * (1.0 / IDX_DIM)
    ik = _rope(h * lax.rsqrt(ms + EPS) * gk_ref[...], c64_ref[...], s64_ref[...], IDX_DIM)
    iw = h * (IDX_HEADS ** -0.5 * IDX_DIM ** -0.5)
    c = jax.nn.log_sigmoid(h + bf_ref[...])
    row = lax.broadcasted_iota(I32, h.shape, 0)
    d = 1
    while d < h.shape[0]:
        c = c + jnp.where(row >= d, pltpu.roll(c, d, 0), 0.0)
        d *= 2
    out = jnp.where(is_ik, ik, jnp.where(lane < SM_CF, iw, c))
    o_ref[...] = out
    ot_ref[0] = out.T


def _even_small(xn, wt, idx_row, fgt_row, gk, bf, c64, s64, batch, seq):
    d = xn.shape[1]
    vec = pl.BlockSpec((1, LANES), lambda b: (0, 0))
    tab = pl.BlockSpec((seq, LANES), lambda b: (0, 0))
    return pl.pallas_call(
        _even_small_kernel,
        grid=(batch,),
        in_specs=[pl.BlockSpec((seq, d), lambda b: (b, 0)),
                  pl.BlockSpec((pl.Element(IDX_DIM + IDX_HEADS), pl.Element(d)), lambda b: (idx_row, 0)),
                  pl.BlockSpec((pl.Element(FOX_HEADS), pl.Element(d)), lambda b: (fgt_row, 0)),
                  vec, vec, tab, tab],
        out_specs=[pl.BlockSpec((seq, LANES), lambda b: (b, 0)),
                   pl.BlockSpec((1, LANES, seq), lambda b: (b, 0, 0))],
        out_shape=[jax.ShapeDtypeStruct((batch * seq, LANES), F32),
                   jax.ShapeDtypeStruct((batch, LANES, seq), F32)],
        compiler_params=_params(("parallel",)),
        name="even_small",
    )(xn, wt, wt, gk, bf, c64, s64)


FOX_TQ = 256
FOX_HPS = 4
ONES_ROWS = 16


def _fox_kernel(q_ref, k_ref, v_ref, g_ref, small_ref, o_ref, ka_ref, vt_ref, *, nq):
    hg = pl.program_id(1)
    i = pl.program_id(2)
    tq = FOX_TQ
    seq = k_ref.shape[0]
    heads = [slice(u * HEAD_DIM, (u + 1) * HEAD_DIM) for u in range(FOX_HPS)]
    lane = _lane_iota((seq, LANES))

    @pl.when(i == 0)
    def _():
        sm = small_ref[...]
        for u, sl in enumerate(heads):
            vt_ref[u, 0:HEAD_DIM, :] = v_ref[:, sl].astype(F32).T.astype(BF16)
            vt_ref[u, HEAD_DIM:, :] = jnp.ones((ONES_ROWS, seq), BF16)
            ck = jnp.sum(jnp.where(lane == SM_CF + hg * FOX_HPS + u, sm, 0.0), axis=1, keepdims=True)
            neg = jnp.broadcast_to(ck * -LOG2E, (seq, LANES))
            hi = neg.astype(BF16).astype(F32)
            mid = (neg - hi).astype(BF16).astype(F32)
            lo = neg - hi - mid
            extra = jnp.where(lane == 0, hi, jnp.where(lane == 1, mid, jnp.where(lane == 2, lo, 0.0)))
            ka_ref[u, :, 0:HEAD_DIM] = k_ref[:, sl]
            ka_ref[u, :, HEAD_DIM:] = extra.astype(BF16)

    ones3 = jnp.where(_lane_iota((tq, LANES)) < 3, 1.0, 0.0).astype(BF16)
    qa = [jnp.concatenate([q_ref[:, sl], ones3], axis=1) for sl in heads]

    def qk(u, j):
        return _dot_nt(ka_ref[u, pl.ds(j * tq, tq), :], qa[u])

    def step(u, j, z, carry, diagonal):
        m, acc = carry
        if diagonal:
            key = lax.broadcasted_iota(I32, (tq, tq), 0)
            qry = lax.broadcasted_iota(I32, (tq, tq), 1)
            z = jnp.where(key <= qry, z, NEG)
        m_new = jnp.maximum(m, jnp.max(z, axis=0, keepdims=True))
        p = jnp.exp2(z - m_new)
        acc = jnp.exp2(m - m_new) * acc + _dot(vt_ref[u, :, pl.ds(j * tq, tq)], p.astype(BF16))
        return m_new, acc

    init = (jnp.full((1, tq), -jnp.inf, F32), jnp.zeros((HEAD_DIM + ONES_ROWS, tq), F32))
    for i_static in range(nq):
        @pl.when(i == i_static)
        def _(i_static=i_static):
            carry = [init] * FOX_HPS
            z_next = [qk(u, 0) for u in range(FOX_HPS)]
            for j in range(i_static + 1):
                for u in range(FOX_HPS):
                    z = z_next[u]
                    if j < i_static:
                        z_next[u] = qk(u, j + 1)
                    carry[u] = step(u, j, z, carry[u], j == i_static)
            for u, sl in enumerate(heads):
                _, acc = carry[u]
                out = (acc[0:HEAD_DIM] * (1.0 / acc[HEAD_DIM:HEAD_DIM + 1])).T
                o_ref[:, sl] = (out * g_ref[:, sl].astype(F32)).astype(o_ref.dtype)


def _fox_attention(hmain, small, batch, seq):
    tq = FOX_TQ
    nq = seq // tq
    width = FOX_HPS * HEAD_DIM
    ng = FOX_W // width
    return pl.pallas_call(
        functools.partial(_fox_kernel, nq=nq),
        grid=(batch, ng, nq),
        in_specs=[pl.BlockSpec((tq, width), lambda b, h, i: (b * nq + i, EV_FQ * ng + h)),
                  pl.BlockSpec((seq, width), lambda b, h, i: (b, EV_FK * ng + h)),
                  pl.BlockSpec((seq, width), lambda b, h, i: (b, EV_FV * ng + h)),
                  pl.BlockSpec((tq, width), lambda b, h, i: (b * nq + i, EV_FG * ng + h)),
                  pl.BlockSpec((seq, LANES), lambda b, h, i: (b, 0))],
        out_specs=pl.BlockSpec((tq, width), lambda b, h, i: (b * nq + i, h)),
        out_shape=jax.ShapeDtypeStruct((batch * seq, FOX_W), BF16),
        scratch_shapes=[pltpu.VMEM((FOX_HPS, seq, 2 * HEAD_DIM), BF16),
                        pltpu.VMEM((FOX_HPS, HEAD_DIM + ONES_ROWS, seq), BF16)],
        compiler_params=_params(("parallel", "parallel", "arbitrary")),
        name="fox_attention",
    )(hmain, hmain, hmain, hmain, small)


DSA_TQ = 256


def _key_to_f32(key):
    return pltpu.bitcast(key ^ ((key >> 31) & 0x7FFFFFFF), F32)


def _dsa_kernel(iq_ref, small_ref, smallt_ref, q_ref, k_ref, v_ref, g_ref, o_ref,
                score_ref, vt_ref, lim_ref, m_ref, acc_ref, *, topk):
    i = pl.program_id(1)
    tq = DSA_TQ
    nchunk = i + 1
    half = tq // 2
    key0 = lax.broadcasted_iota(I32, (tq, tq), 0)
    qry = i * tq + lax.broadcasted_iota(I32, (tq, tq), 1)

    @pl.when(i == 0)
    def _():
        for hh in range(DSA_HEADS):
            sl = slice(hh * HEAD_DIM, (hh + 1) * HEAD_DIM)
            vt_ref[hh, 0:HEAD_DIM, :] = v_ref[:, sl].astype(F32).T.astype(BF16)
            vt_ref[hh, HEAD_DIM:, :] = jnp.ones((ONES_ROWS, v_ref.shape[0]), BF16)

    def chunk_off(c):
        return pl.multiple_of(c * tq, tq)

    def score_chunk(c, _):
        off = chunk_off(c)
        ikc = small_ref[pl.ds(off, tq), SM_IK:SM_IK + IDX_DIM].astype(BF16)

        def logits(hh):
            return _dot_nt(ikc, iq_ref[:, hh * IDX_DIM:(hh + 1) * IDX_DIM])

        acc = jnp.zeros((tq, tq), F32)
        nxt = logits(0)
        for hh in range(IDX_HEADS):
            cur = nxt
            if hh + 1 < IDX_HEADS:
                nxt = logits(hh + 1)
            w = smallt_ref[0, SM_IW + hh:SM_IW + hh + 1, pl.ds(pl.multiple_of(i * tq, tq), tq)]
            acc = acc + w * jnp.maximum(cur, 0.0)
        score_ref[pl.ds(off, tq), :] = jnp.where(off + key0 <= qry, acc, NEG)
        return 0

    lax.fori_loop(0, nchunk, score_chunk, 0)

    def count(pred):
        def body(c, tot):
            off = chunk_off(c)
            hit = jnp.where(pred(score_ref[pl.ds(off, tq), :], off + key0), 1.0, 0.0)
            return tot + jnp.sum(hit.reshape(tq // 8, 8, tq), axis=0)
        tot = lax.fori_loop(0, nchunk, body, jnp.zeros((8, tq), F32))
        return jnp.sum(tot, axis=0, keepdims=True)

    def thr_step(it, res):
        cand = res + jnp.left_shift(jnp.int32(1), 31 - it)
        cand_f = _key_to_f32(cand)
        cnt = count(lambda sc, ki: sc >= cand_f)
        return jnp.where(cnt >= topk, cand, res)

    thr_key = lax.fori_loop(0, 32, thr_step, jnp.full((1, tq), INT_MIN, I32))
    thr = jnp.where(thr_key == INT_MIN, -jnp.inf, _key_to_f32(thr_key))
    n_gt = count(lambda sc, ki: sc > thr)
    n_ge = count(lambda sc, ki: sc >= thr)
    need = topk - n_gt

    lim_ref[...] = jnp.full((1, tq), 2 ** IDX_BITS, I32)

    @pl.when(jnp.max(n_ge) > topk)
    def _():
        def lim_step(it, res):
            cand = res + jnp.left_shift(jnp.int32(1), IDX_BITS - 1 - it)
            cnt = count(lambda sc, ki: (sc == thr) & (ki < cand))
            return jnp.where(cnt <= need, cand, res)
        lim_ref[...] = lax.fori_loop(0, IDX_BITS, lim_step, jnp.zeros((1, tq), I32))

    lim = lim_ref[...]

    m_ref[...] = jnp.full(m_ref.shape, -jnp.inf, F32)
    acc_ref[...] = jnp.zeros(acc_ref.shape, F32)

    def attn_chunk(c, _):
        off = chunk_off(c)
        sc = score_ref[pl.ds(off, tq), :]
        ki = off + key0
        sel = ((sc > thr) | ((sc == thr) & (ki < lim))) & (ki <= qry)
        bias = jnp.where(sel, 0.0, NEG)

        def qk(hh):
            sl = slice(hh * HEAD_DIM, (hh + 1) * HEAD_DIM)
            return _dot_nt(k_ref[pl.ds(off, tq), sl], q_ref[:, sl])

        heads = range(DSA_HEADS)
        zs = [bias + qk(hh) for hh in heads]
        ms = [m_ref[hh:hh + 1, :] for hh in heads]
        m_news = [jnp.maximum(ms[hh], jnp.max(zs[hh], axis=0, keepdims=True)) for hh in heads]
        ps = [jnp.exp2(zs[hh] - m_news[hh]).astype(BF16) for hh in heads]
        pvs = [_dot(vt_ref[hh, :, pl.ds(off, tq)], ps[hh]) for hh in heads]
        for hh in heads:
            acc_ref[hh] = jnp.exp2(ms[hh] - m_news[hh]) * acc_ref[hh] + pvs[hh]
            m_ref[hh:hh + 1, :] = m_news[hh]
        return 0

    lax.fori_loop(0, nchunk, attn_chunk, 0)
    for hh in range(DSA_HEADS):
        sl = slice(hh * HEAD_DIM, (hh + 1) * HEAD_DIM)
        out = (acc_ref[hh, 0:HEAD_DIM, :] * (1.0 / acc_ref[hh, HEAD_DIM:HEAD_DIM + 1, :])).T
        o_ref[:, sl] = (out * g_ref[:, sl].astype(F32)).astype(o_ref.dtype)


def _dsa_attention(hmain, small, small_t, batch, seq, topk):
    tq = DSA_TQ
    nq = seq // tq
    wide = lambda t: pl.BlockSpec((tq, DSA_W), lambda b, i: (b * nq + i, t))
    full = lambda t: pl.BlockSpec((seq, DSA_W), lambda b, i: (b, t))
    return pl.pallas_call(
        functools.partial(_dsa_kernel, topk=topk),
        grid=(batch, nq),
        in_specs=[wide(EV_IQ),
                  pl.BlockSpec((seq, LANES), lambda b, i: (b, 0)),
                  pl.BlockSpec((1, LANES, seq), lambda b, i: (b, 0, 0)),
                  wide(EV_DQ), full(EV_DK), full(EV_DV), wide(EV_DG)],
        out_specs=pl.BlockSpec((tq, DSA_W), lambda b, i: (b * nq + i, 0)),
        out_shape=jax.ShapeDtypeStruct((batch * seq, DSA_W), BF16),
        scratch_shapes=[pltpu.VMEM((seq, tq), F32),
                        pltpu.VMEM((DSA_HEADS, HEAD_DIM + ONES_ROWS, seq), BF16),
                        pltpu.VMEM((1, tq), I32),
                        pltpu.VMEM((DSA_HEADS, tq), F32),
                        pltpu.VMEM((DSA_HEADS, HEAD_DIM + ONES_ROWS, tq), F32)],
        compiler_params=_params(("parallel", "arbitrary")),
        name="dsa_attention",
    )(hmain, small, small_t, hmain, hmain, hmain, hmain)


def _out_proj_kernel(*refs, ny, normed):
    y_refs, w_ref, x_ref = refs[:ny], refs[ny], refs[ny + 1]
    if normed:
        g_ref, o_ref, n_ref = refs[ny + 2:]
    else:
        (o_ref,) = refs[ny + 2:]
    acc = x_ref[...]
    off = 0
    for y_ref in y_refs:
        kdim = y_ref.shape[1]
        acc = acc + _dot(y_ref[...], w_ref[off:off + kdim, :])
        off += kdim
    o_ref[...] = acc
    if normed:
        y = acc * lax.rsqrt(jnp.mean(acc * acc, axis=-1, keepdims=True) + EPS)
        n_ref[...] = (y * g_ref[...]).astype(n_ref.dtype)


def _out_proj(ys, w, x2, next_gain=None, tm=512):
    m, d = x2.shape
    row = pl.BlockSpec((tm, d), lambda i: (i, 0))
    normed = next_gain is not None
    extra = ([next_gain.reshape(1, d)], [pl.BlockSpec((1, d), lambda i: (0, 0))]) if normed else ([], [])
    out = pl.pallas_call(
        functools.partial(_out_proj_kernel, ny=len(ys), normed=normed),
        grid=(m // tm,),
        in_specs=[pl.BlockSpec((tm, y.shape[1]), lambda i: (i, 0)) for y in ys]
                 + [pl.BlockSpec(w.shape, lambda i: (0, 0)), row] + extra[1],
        out_specs=[row, row] if normed else [row],
        out_shape=[jax.ShapeDtypeStruct((m, d), F32)] + ([jax.ShapeDtypeStruct((m, d), BF16)] if normed else []),
        compiler_params=_params(("parallel",)),
        name="out_proj",
    )(*ys, w, x2, *extra[0])
    return (out[0], out[1]) if normed else (out[0], None)


OD_TN = 512
OD_NQ = SWA_W // OD_TN


def _odd_kinds(j, aux, nslab):
    gain_ref, c64_ref, s64_ref = aux
    qk = lambda idx: (lambda a: _rope(_half_rms(a, gain_ref[idx]), c64_ref[...], s64_ref[...], SWA_HEAD_DIM))
    nk = SWA_KV_W // LANES
    return [
        (j < OD_NQ, lambda: [(qk(0), 0, nslab)]),
        (j == OD_NQ, lambda: [(qk(1), 0, nk), (lambda a: a, nk, nslab)]),
        (j > OD_NQ, lambda: [(lambda a: a * jax.nn.sigmoid(a), 0, nslab)]),
    ]


def _odd_proj(xn, w, layer, gains, c64, s64, seq):
    gspec = pl.BlockSpec((2, 1, LANES), lambda t: (0, 0, 0))
    return _skewed_proj(xn, w, layer, False, lambda j: j, (gains, c64, s64), [gspec, None, None],
                        _odd_kinds, seq, OD_TN, w.shape[2] // OD_TN, "odd_proj")


def _swa_kernel(sink_ref, q_ref, kvp_ref, kvc_ref, *rest):
    *g_refs, o_ref = rest
    n = pl.program_id(1)
    w = SWA_WINDOW
    dh = SWA_HEAD_DIM
    group = SWA_HEADS // SWA_KV_HEADS
    gw = group * w
    kb = lax.broadcasted_iota(I32, (2 * w, gw), 0)
    qi = lax.broadcasted_iota(I32, (2 * w, gw), 1) & (w - 1)
    valid = (kb > qi) & (kb <= qi + w) & ((n > 0) | (kb >= w))
    bias = jnp.where(valid, 0.0, NEG)
    head_of_lane = lax.broadcasted_iota(I32, (1, gw), 1) // w
    ones = jnp.ones((ONES_ROWS, 2 * w), BF16)

    def band(col):
        sl = slice(col, col + dh)
        return jnp.concatenate([kvp_ref[:, sl], kvc_ref[:, sl]], axis=0)

    for kv in range(SWA_KV_HEADS):
        h0 = kv * group
        qg = jnp.concatenate([q_ref[:, (h0 + t) * dh:(h0 + t + 1) * dh] for t in range(group)], axis=0)
        z = bias + _dot_nt(band(kv * dh), qg)
        sink = jnp.zeros((1, gw), F32)
        for t in range(group):
            sink = jnp.where(head_of_lane == t, sink_ref[h0 + t] * LOG2E, sink)
        m = jnp.maximum(jnp.max(z, axis=0, keepdims=True), sink)
        vt = jnp.concatenate([band(SWA_KV_W + kv * dh).astype(F32).T.astype(BF16), ones], axis=0)
        pv = _dot(vt, jnp.exp2(z - m).astype(BF16))
        den = pv[dh:dh + 1] + jnp.exp2(sink - m)
        ot = (pv[0:dh] * (1.0 / den)).T
        o = jnp.concatenate([ot[t * w:(t + 1) * w] for t in range(group)], axis=1)
        sl = slice(h0 * dh, (h0 + group) * dh)
        o_ref[:, sl] = (o * g_refs[kv][...].astype(F32)).astype(o_ref.dtype)


def _swa_attention(hodd, sinks, batch, seq):
    w = SWA_WINDOW
    nb = seq // w
    kvw = 2 * SWA_KV_W
    kv_col = SWA_W // kvw
    gate = lambda kv: pl.BlockSpec((w, kvw), lambda b, n: (b * nb + n, kv_col + 1 + kv))
    return pl.pallas_call(
        _swa_kernel,
        grid=(batch, nb),
        in_specs=[pl.BlockSpec(memory_space=pltpu.SMEM),
                  pl.BlockSpec((w, SWA_W), lambda b, n: (b * nb + n, 0)),
                  pl.BlockSpec((w, kvw), lambda b, n: (b * nb + jnp.maximum(n - 1, 0), kv_col)),
                  pl.BlockSpec((w, kvw), lambda b, n: (b * nb + n, kv_col))]
                 + [gate(kv) for kv in range(SWA_KV_HEADS)],
        out_specs=pl.BlockSpec((w, SWA_W), lambda b, n: (b * nb + n, 0)),
        out_shape=jax.ShapeDtypeStruct((batch * seq, SWA_W), BF16),
        compiler_params=_params(("parallel", "arbitrary")),
        name="swa_attention",
    )(sinks, hodd, hodd, hodd, *([hodd] * SWA_KV_HEADS))


def _rope_tables(seq, dim):
    inv = 1.0 / (ROPE_THETA ** (jnp.arange(0, dim, 2, dtype=F32) / dim))
    ang = jnp.arange(seq, dtype=F32)[:, None] * inv[None, :]
    cos, sin = jnp.cos(ang), jnp.sin(ang)
    reps = LANES // dim
    return (jnp.tile(jnp.concatenate([cos, cos], -1), (1, reps)),
            jnp.tile(jnp.concatenate([-sin, sin], -1), (1, reps)))


def _cols(w, start, size):
    return lax.slice_in_dim(w, start, start + size, axis=1)


def _even_layer(x2, xn, next_gain, batch, seq, w_in_stack, j, b_f, g_fox, g_dsa, g_kidx, w_out, tabs, topk):
    c128, s128, c64, s64 = tabs
    ncol = w_in_stack.shape[2]
    row0 = j * ncol
    fgt_row = row0 + 4 * FOX_W
    idx_row = fgt_row + FOX_HEADS + 4 * DSA_W + IDX_W
    pad = LANES - (IDX_DIM + IDX_HEADS + FOX_HEADS)
    wt = jnp.swapaxes(w_in_stack, 1, 2).reshape(-1, w_in_stack.shape[1])
    qscale = HEAD_DIM ** -0.5 * LOG2E
    one = jnp.ones((LANES,), F32)
    gains = jnp.stack([g_fox[0] * qscale, g_fox[1], one, one,
                       g_dsa[0] * qscale, g_dsa[1], one, one, one]).reshape(EV_IQ + 1, 1, LANES)
    gk = jnp.concatenate([g_kidx, jnp.zeros((LANES - IDX_DIM,), F32)]).reshape(1, LANES)
    bf = jnp.concatenate([jnp.zeros((SM_CF,), F32), b_f, jnp.zeros((pad,), F32)]).reshape(1, LANES)

    hmain = _even_proj(xn, wt, row0, gains, c128, s128, c64, s64, seq)
    small, small_t = _even_small(xn, wt, idx_row, fgt_row, gk, bf, c64, s64, batch, seq)
    ya = _fox_attention(hmain, small, batch, seq)
    yb = _dsa_attention(hmain, small, small_t, batch, seq, topk)
    return _out_proj([ya, yb], w_out.astype(BF16), x2, next_gain)


def _odd_layer(x2, xn, next_gain, batch, seq, w_in_stack, j, g_qk, sinks, w_out, tabs):
    _, _, c64, s64 = tabs
    qscale = jnp.array([[SWA_HEAD_DIM ** -0.5 * LOG2E], [1.0]], F32)
    gains = jnp.tile(g_qk * qscale, (1, LANES // SWA_HEAD_DIM)).reshape(2, 1, LANES)
    hodd = _odd_proj(xn, w_in_stack, j, gains, c64, s64, seq)
    y = _swa_attention(hodd, sinks, batch, seq)
    return _out_proj([y], w_out.astype(BF16), x2, next_gain)


def kernel(x, norm_even, w_in_even, b_f_even, g_qk_fox, g_qk_dsa, g_kidx, w_out_even,
           norm_odd, w_in_odd, g_qk_swa, sinks, w_out_odd):
    batch, seq, d = x.shape
    depth = norm_even.shape[0] + norm_odd.shape[0]
    topk = min(IDX_TOPK_MAX, seq // 4)
    tabs = _rope_tables(seq, HEAD_DIM) + _rope_tables(seq, IDX_DIM)
    x2 = x.reshape(batch * seq, d)
    pre_gain = lambda layer: (norm_even, norm_odd)[layer % 2][layer // 2]
    xn = _rmsnorm(x2, pre_gain(0))
    for layer in range(depth):
        j = layer // 2
        next_gain = pre_gain(layer + 1) if layer + 1 < depth else None
        if layer % 2 == 0:
            x2, xn = _even_layer(x2, xn, next_gain, batch, seq, w_in_even, j, b_f_even[j], g_qk_fox[j],
                                 g_qk_dsa[j], g_kidx[j], w_out_even[j], tabs, topk)
        else:
            x2, xn = _odd_layer(x2, xn, next_gain, batch, seq, w_in_odd, j, g_qk_swa[j], sinks[j],
                                w_out_odd[j], tabs)
    return x2.reshape(batch, seq, d)
```

```python
import functools

import jax
import jax.numpy as jnp
from jax import lax
from jax.experimental import pallas as pl
from jax.experimental.pallas import tpu as pltpu

F32 = jnp.float32
BF16 = jnp.bfloat16
I32 = jnp.int32

D_MODEL = 2048
HEAD_DIM = 128
FOX_HEADS = 8
DSA_HEADS = 8
IDX_HEADS = 16
IDX_DIM = 64
IDX_TOPK_MAX = 256
SWA_HEADS = 32
SWA_KV_HEADS = 4
SWA_HEAD_DIM = 64
SWA_WINDOW = 128
ROPE_THETA = 10000.0
EPS = 1e-6
NEG = -1e30

FOX_W = FOX_HEADS * HEAD_DIM
DSA_W = DSA_HEADS * HEAD_DIM
IDX_W = IDX_HEADS * IDX_DIM
SWA_W = SWA_HEADS * SWA_HEAD_DIM
SWA_KV_W = SWA_KV_HEADS * SWA_HEAD_DIM

LANES = 128
VMEM_LIMIT = 56 * 2 ** 20

SM_IK = 0
SM_IW = IDX_DIM
SM_CF = IDX_DIM + IDX_HEADS

LOG2E = 1.4426950408889634
INT_MIN = -2 ** 31
IDX_BITS = 12


def _params(sem):
    return pltpu.CompilerParams(dimension_semantics=sem, vmem_limit_bytes=VMEM_LIMIT)


def _dot(a, b):
    return jnp.dot(a, b, preferred_element_type=F32)


def _dot_nt(a, b):
    return lax.dot_general(a, b, (((1,), (1,)), ((), ())), preferred_element_type=F32)


def _lane_iota(shape):
    return lax.broadcasted_iota(I32, shape, 1)


def _rope(y, cos, sin, dim):
    if dim == LANES:
        rot = pltpu.roll(y, LANES // 2, 1)
    else:
        half = dim // 2
        first = (_lane_iota(y.shape) & half) == 0
        rot = jnp.where(first, pltpu.roll(y, LANES - half, 1), pltpu.roll(y, half, 1))
    return y * cos + rot * sin


def _head_rms(a, gain):
    return a * lax.rsqrt(jnp.mean(a * a, axis=-1, keepdims=True) + EPS) * gain


def _half_rms(a, gain):
    lo = _lane_iota(a.shape) < 64
    sq = a * a
    ms_lo = jnp.sum(jnp.where(lo, sq, 0.0), axis=-1, keepdims=True) * (1.0 / 64)
    ms_hi = jnp.sum(jnp.where(lo, 0.0, sq), axis=-1, keepdims=True) * (1.0 / 64)
    return a * lax.rsqrt(jnp.where(lo, ms_lo, ms_hi) + EPS) * gain


def _rmsnorm_kernel(x_ref, g_ref, o_ref):
    x = x_ref[...]
    y = x * lax.rsqrt(jnp.mean(x * x, axis=-1, keepdims=True) + EPS)
    o_ref[...] = (y * g_ref[...]).astype(o_ref.dtype)


def _rmsnorm(x2, g, tm=512):
    m, d = x2.shape
    return pl.pallas_call(
        _rmsnorm_kernel,
        grid=(m // tm,),
        in_specs=[pl.BlockSpec((tm, d), lambda i: (i, 0)), pl.BlockSpec((1, d), lambda i: (0, 0))],
        out_specs=pl.BlockSpec((tm, d), lambda i: (i, 0)),
        out_shape=jax.ShapeDtypeStruct((m, d), BF16),
        compiler_params=_params(("parallel",)),
        name="rmsnorm",
    )(x2, g.reshape(1, d))


def _skewed_proj_kernel(xn_ref, w_ref, *rest, kinds, transposed, nj, tn):
    *aux, o_ref, acc_ref = rest
    t = pl.program_id(0)
    last = pl.num_programs(0) - 1
    j = (t + nj - 1) % nj

    def finish(parts):
        for fn, lo, hi in parts:
            for h in range(lo, hi):
                sl = slice(h * LANES, (h + 1) * LANES)
                o_ref[:, sl] = fn(acc_ref[:, sl]).astype(o_ref.dtype)

    def multiply():
        w = (w_ref[...] if transposed else w_ref[0]).astype(BF16)
        acc_ref[...] = _dot_nt(xn_ref[...], w) if transposed else _dot(xn_ref[...], w)

    @pl.when(t == 0)
    def _():
        multiply()

    nslab = tn // LANES
    for cond, parts in kinds(j, aux, nslab):
        @pl.when((t > 0) & (t < last) & cond)
        def _(parts=parts):
            finish(parts())
            multiply()

    for cond, parts in kinds(nj - 1, aux, nslab):
        if cond:
            @pl.when(t == last)
            def _(parts=parts):
                finish(parts())


def _skewed_proj(xn, w_stack, layer, transposed, locate, aux, aux_specs, kinds, seq, tn, nj, name, tm=1024):
    m, d = xn.shape
    ntiles = (m // tm) * nj
    nrow = seq // tm
    cur = lambda t: jnp.minimum(t, ntiles - 1)
    prev = lambda t: jnp.maximum(t - 1, 0)
    tab = pl.BlockSpec((tm, LANES), lambda t: ((prev(t) // nj) % nrow, 0))
    specs = [tab if s is None else s for s in aux_specs]
    if transposed:
        wspec = pl.BlockSpec((pl.Element(tn), pl.Element(d)), lambda t: (locate(cur(t) % nj), 0))
    else:
        wspec = pl.BlockSpec((1, d, tn), lambda t: (layer, 0, locate(cur(t) % nj)))
    return pl.pallas_call(
        functools.partial(_skewed_proj_kernel, kinds=kinds, transposed=transposed, nj=nj, tn=tn),
        grid=(ntiles + 1,),
        in_specs=[pl.BlockSpec((tm, d), lambda t: (cur(t) // nj, 0)), wspec] + specs,
        out_specs=pl.BlockSpec((tm, tn), lambda t: (prev(t) // nj, prev(t) % nj)),
        out_shape=jax.ShapeDtypeStruct((m, nj * tn), BF16),
        scratch_shapes=[pltpu.VMEM((tm, tn), F32)],
        compiler_params=_params(("arbitrary",)),
        name=name,
    )(xn, w_stack, *aux)


EV_TN = 1024
EV_FQ, EV_FK, EV_FV, EV_FG, EV_DQ, EV_DK, EV_DV, EV_DG, EV_IQ = range(9)
EV_NFOX = 4


def _even_kinds(j, aux, nslab):
    gain_ref, c128_ref, s128_ref, c64_ref, s64_ref = aux
    gain = lambda: gain_ref[jnp.minimum(j, EV_IQ)]
    norm = lambda: [(lambda a: _head_rms(a, gain()), 0, nslab)]
    norm_rope = lambda: [(lambda a: _rope(_head_rms(a, gain()), c128_ref[...], s128_ref[...], HEAD_DIM), 0, nslab)]
    cast = lambda: [(lambda a: a, 0, nslab)]
    silu = lambda: [(lambda a: a * jax.nn.sigmoid(a), 0, nslab)]
    rope64 = lambda: [(lambda a: _rope(a, c64_ref[...], s64_ref[...], IDX_DIM), 0, nslab)]
    return [
        ((j == EV_FQ) | (j == EV_FK), norm),
        ((j == EV_DQ) | (j == EV_DK), norm_rope),
        ((j == EV_FV) | (j == EV_DV), cast),
        ((j == EV_FG) | (j == EV_DG), silu),
        (j == EV_IQ, rope64),
    ]


def _even_proj(xn, wt, row0, gains, c128, s128, c64, s64, seq):
    gspec = pl.BlockSpec(gains.shape, lambda t: (0, 0, 0))
    row = lambda j: (row0 // 8 + j * (EV_TN // 8) + (j + EV_NFOX) // (2 * EV_NFOX)) * 8
    return _skewed_proj(xn, wt, None, True, row, (gains, c128, s128, c64, s64),
                        [gspec, None, None, None, None], _even_kinds, seq, EV_TN, EV_IQ + 1, "even_proj")


def _even_small_kernel(xn_ref, wi_ref, wf_ref, gk_ref, bf_ref, c64_ref, s64_ref, o_ref, ot_ref):
    d = xn_ref.shape[1]
    pad = jnp.zeros((LANES - wi_ref.shape[0] - wf_ref.shape[0], d), F32)
    wt = jnp.concatenate([wi_ref[...], wf_ref[...], pad], axis=0).astype(BF16)
    h = _dot_nt(xn_ref[...], wt)
    lane = _lane_iota(h.shape)
    is_ik = lane < SM_IW
    ms = jnp.sum(jnp.where(is_ik, h * h, 0.0), axis=-1, keepdims=True) * (1.0 / IDX_DIM)
    ik = _rope(h * lax.rsqrt(ms + EPS) * gk_ref[...], c64_ref[...], s64_ref[...], IDX_DIM)
    iw = h * (IDX_HEADS ** -0.5 * IDX_DIM ** -0.5)
    c = jax.nn.log_sigmoid(h + bf_ref[...])
    row = lax.broadcasted_iota(I32, h.shape, 0)
    d = 1
    while d < h.shape[0]:
        c = c + jnp.where(row >= d, pltpu.roll(c, d, 0), 0.0)
        d *= 2
    out = jnp.where(is_ik, ik, jnp.where(lane < SM_CF, iw, c))
    o_ref[...] = out
    ot_ref[0] = out.T


def _even_small(xn, wt, idx_row, fgt_row, gk, bf, c64, s64, batch, seq):
    d = xn.shape[1]
    vec = pl.BlockSpec((1, LANES), lambda b: (0, 0))
    tab = pl.BlockSpec((seq, LANES), lambda b: (0, 0))
    return pl.pallas_call(
        _even_small_kernel,
        grid=(batch,),
        in_specs=[pl.BlockSpec((seq, d), lambda b: (b, 0)),
                  pl.BlockSpec((pl.Element(IDX_DIM + IDX_HEADS), pl.Element(d)), lambda b: (idx_row, 0)),
                  pl.BlockSpec((pl.Element(FOX_HEADS), pl.Element(d)), lambda b: (fgt_row, 0)),
                  vec, vec, tab, tab],
        out_specs=[pl.BlockSpec((seq, LANES), lambda b: (b, 0)),
                   pl.BlockSpec((1, LANES, seq), lambda b: (b, 0, 0))],
        out_shape=[jax.ShapeDtypeStruct((batch * seq, LANES), F32),
                   jax.ShapeDtypeStruct((batch, LANES, seq), F32)],
        compiler_params=_params(("parallel",)),
        name="even_small",
    )(xn, wt, wt, gk, bf, c64, s64)


FOX_TQ = 256
FOX_HPS = 8
ONES_ROWS = 16


def _fox_kernel(q_ref, k_ref, v_ref, g_ref, small_ref, o_ref, ka_ref, vt_ref, *, nq):
    hg = pl.program_id(1)
    i = pl.program_id(2)
    tq = FOX_TQ
    seq = k_ref.shape[0]
    heads = [slice(u * HEAD_DIM, (u + 1) * HEAD_DIM) for u in range(FOX_HPS)]
    lane = _lane_iota((seq, LANES))

    @pl.when(i == 0)
    def _():
        sm = small_ref[...]
        for u, sl in enumerate(heads):
            vt_ref[u, 0:HEAD_DIM, :] = v_ref[:, sl].astype(F32).T.astype(BF16)
            vt_ref[u, HEAD_DIM:, :] = jnp.ones((ONES_ROWS, seq), BF16)
            ck = jnp.sum(jnp.where(lane == SM_CF + hg * FOX_HPS + u, sm, 0.0), axis=1, keepdims=True)
            neg = jnp.broadcast_to(ck * -LOG2E, (seq, LANES))
            hi = neg.astype(BF16).astype(F32)
            mid = (neg - hi).astype(BF16).astype(F32)
            lo = neg - hi - mid
            extra = jnp.where(lane == 0, hi, jnp.where(lane == 1, mid, jnp.where(lane == 2, lo, 0.0)))
            ka_ref[u, :, 0:HEAD_DIM] = k_ref[:, sl]
            ka_ref[u, :, HEAD_DIM:] = extra.astype(BF16)

    ones3 = jnp.where(_lane_iota((tq, LANES)) < 3, 1.0, 0.0).astype(BF16)
    qa = [jnp.concatenate([q_ref[:, sl], ones3], axis=1) for sl in heads]

    def qk(u, j):
        return _dot_nt(ka_ref[u, pl.ds(j * tq, tq), :], qa[u])

    def step(u, j, z, carry, diagonal):
        m, acc = carry
        if diagonal:
            key = lax.broadcasted_iota(I32, (tq, tq), 0)
            qry = lax.broadcasted_iota(I32, (tq, tq), 1)
            z = jnp.where(key <= qry, z, NEG)
        m_new = jnp.maximum(m, jnp.max(z, axis=0, keepdims=True))
        p = jnp.exp2(z - m_new)
        acc = jnp.exp2(m - m_new) * acc + _dot(vt_ref[u, :, pl.ds(j * tq, tq)], p.astype(BF16))
        return m_new, acc

    init = (jnp.full((1, tq), -jnp.inf, F32), jnp.zeros((HEAD_DIM + ONES_ROWS, tq), F32))
    for i_static in range(nq):
        @pl.when(i == i_static)
        def _(i_static=i_static):
            carry = [init] * FOX_HPS
            z_next = [qk(u, 0) for u in range(FOX_HPS)]
            for j in range(i_static + 1):
                for u in range(FOX_HPS):
                    z = z_next[u]
                    if j < i_static:
                        z_next[u] = qk(u, j + 1)
                    carry[u] = step(u, j, z, carry[u], j == i_static)
            for u, sl in enumerate(heads):
                _, acc = carry[u]
                out = (acc[0:HEAD_DIM] * (1.0 / acc[HEAD_DIM:HEAD_DIM + 1])).T
                o_ref[:, sl] = (out * g_ref[:, sl].astype(F32)).astype(o_ref.dtype)


def _fox_attention(hmain, small, batch, seq):
    tq = FOX_TQ
    nq = seq // tq
    width = FOX_HPS * HEAD_DIM
    ng = FOX_W // width
    return pl.pallas_call(
        functools.partial(_fox_kernel, nq=nq),
        grid=(batch, ng, nq),
        in_specs=[pl.BlockSpec((tq, width), lambda b, h, i: (b * nq + i, EV_FQ * ng + h)),
                  pl.BlockSpec((seq, width), lambda b, h, i: (b, EV_FK * ng + h)),
                  pl.BlockSpec((seq, width), lambda b, h, i: (b, EV_FV * ng + h)),
                  pl.BlockSpec((tq, width), lambda b, h, i: (b * nq + i, EV_FG * ng + h)),
                  pl.BlockSpec((seq, LANES), lambda b, h, i: (b, 0))],
        out_specs=pl.BlockSpec((tq, width), lambda b, h, i: (b * nq + i, h)),
        out_shape=jax.ShapeDtypeStruct((batch * seq, FOX_W), BF16),
        scratch_shapes=[pltpu.VMEM((FOX_HPS, seq, 2 * HEAD_DIM), BF16),
                        pltpu.VMEM((FOX_HPS, HEAD_DIM + ONES_ROWS, seq), BF16)],
        compiler_params=_params(("parallel", "parallel", "arbitrary")),
        name="fox_attention",
    )(hmain, hmain, hmain, hmain, small)


DSA_TQ = 256


def _key_to_f32(key):
    return pltpu.bitcast(key ^ ((key >> 31) & 0x7FFFFFFF), F32)


def _dsa_kernel(iq_ref, small_ref, smallt_ref, q_ref, k_ref, v_ref, g_ref, o_ref,
                score_ref, vt_ref, lim_ref, m_ref, acc_ref, *, topk, nq):
    i = pl.program_id(1)
    tq = DSA_TQ
    nchunk = i + 1
    half = tq // 2
    key0 = lax.broadcasted_iota(I32, (tq, tq), 0)
    qry = i * tq + lax.broadcasted_iota(I32, (tq, tq), 1)

    @pl.when(i == 0)
    def _():
        for hh in range(DSA_HEADS):
            sl = slice(hh * HEAD_DIM, (hh + 1) * HEAD_DIM)
            vt_ref[hh, 0:HEAD_DIM, :] = v_ref[:, sl].astype(F32).T.astype(BF16)
            vt_ref[hh, HEAD_DIM:, :] = jnp.ones((ONES_ROWS, v_ref.shape[0]), BF16)

    def chunk_off(c):
        return pl.multiple_of(c * tq, tq)

    def score_chunk(c, _):
        off = chunk_off(c)
        ikc = small_ref[pl.ds(off, tq), SM_IK:SM_IK + IDX_DIM].astype(BF16)

        def logits(hh):
            return _dot_nt(ikc, iq_ref[:, hh * IDX_DIM:(hh + 1) * IDX_DIM])

        acc = jnp.zeros((tq, tq), F32)
        nxt = logits(0)
        for hh in range(IDX_HEADS):
            cur = nxt
            if hh + 1 < IDX_HEADS:
                nxt = logits(hh + 1)
            w = smallt_ref[0, SM_IW + hh:SM_IW + hh + 1, pl.ds(pl.multiple_of(i * tq, tq), tq)]
            acc = acc + w * jnp.maximum(cur, 0.0)
        score_ref[pl.ds(off, tq), :] = jnp.where(off + key0 <= qry, acc, NEG)
        return 0

    lax.fori_loop(0, nchunk, score_chunk, 0)

    def count(pred):
        def body(c, tot):
            off = chunk_off(c)
            hit = jnp.where(pred(score_ref[pl.ds(off, tq), :], off + key0), 1.0, 0.0)
            return tot + jnp.sum(hit.reshape(tq // 8, 8, tq), axis=0)
        tot = lax.fori_loop(0, nchunk, body, jnp.zeros((8, tq), F32))
        return jnp.sum(tot, axis=0, keepdims=True)

    def thr_step(it, res, nlive):
        cand = res + jnp.left_shift(jnp.int32(1), 31 - it)
        cand_f = _key_to_f32(cand)
        tot = jnp.zeros((8, tq), F32)
        for c in range(nlive):
            hit = jnp.where(score_ref[c * tq:(c + 1) * tq, :] >= cand_f, 1.0, 0.0)
            tot = tot + jnp.sum(hit.reshape(tq // 8, 8, tq), axis=0)
        cnt = jnp.sum(tot, axis=0, keepdims=True)
        return jnp.where(cnt >= topk, cand, res)

    for i_static in range(nq):
        @pl.when(i == i_static)
        def _(i_static=i_static):
            key = lax.fori_loop(0, 32, functools.partial(thr_step, nlive=i_static + 1),
                                jnp.full((1, tq), INT_MIN, I32))
            lim_ref[...] = key

    thr_key = lim_ref[...]
    thr = jnp.where(thr_key == INT_MIN, -jnp.inf, _key_to_f32(thr_key))
    n_gt = count(lambda sc, ki: sc > thr)
    n_ge = count(lambda sc, ki: sc >= thr)
    need = topk - n_gt

    lim_ref[...] = jnp.full((1, tq), 2 ** IDX_BITS, I32)

    @pl.when(jnp.max(n_ge) > topk)
    def _():
        def lim_step(it, res):
            cand = res + jnp.left_shift(jnp.int32(1), IDX_BITS - 1 - it)
            cnt = count(lambda sc, ki: (sc == thr) & (ki < cand))
            return jnp.where(cnt <= need, cand, res)
        lim_ref[...] = lax.fori_loop(0, IDX_BITS, lim_step, jnp.zeros((1, tq), I32))

    lim = lim_ref[...]

    m_ref[...] = jnp.full(m_ref.shape, -jnp.inf, F32)
    acc_ref[...] = jnp.zeros(acc_ref.shape, F32)

    def attn_chunk(c, _):
        off = chunk_off(c)
        sc = score_ref[pl.ds(off, tq), :]
        ki = off + key0
        sel = ((sc > thr) | ((sc == thr) & (ki < lim))) & (ki <= qry)
        bias = jnp.where(sel, 0.0, NEG)

        def qk(hh):
            sl = slice(hh * HEAD_DIM, (hh + 1) * HEAD_DIM)
            return _dot_nt(k_ref[pl.ds(off, tq), sl], q_ref[:, sl])

        heads = range(DSA_HEADS)
        zs = [bias + qk(hh) for hh in heads]
        ms = [m_ref[hh:hh + 1, :] for hh in heads]
        m_news = [jnp.maximum(ms[hh], jnp.max(zs[hh], axis=0, keepdims=True)) for hh in heads]
        ps = [jnp.exp2(zs[hh] - m_news[hh]).astype(BF16) for hh in heads]
        pvs = [_dot(vt_ref[hh, :, pl.ds(off, tq)], ps[hh]) for hh in heads]
        for hh in heads:
            acc_ref[hh] = jnp.exp2(ms[hh] - m_news[hh]) * acc_ref[hh] + pvs[hh]
            m_ref[hh:hh + 1, :] = m_news[hh]
        return 0

    lax.fori_loop(0, nchunk, attn_chunk, 0)
    for hh in range(DSA_HEADS):
        sl = slice(hh * HEAD_DIM, (hh + 1) * HEAD_DIM)
        out = (acc_ref[hh, 0:HEAD_DIM, :] * (1.0 / acc_ref[hh, HEAD_DIM:HEAD_DIM + 1, :])).T
        o_ref[:, sl] = (out * g_ref[:, sl].astype(F32)).astype(o_ref.dtype)


def _dsa_attention(hmain, small, small_t, batch, seq, topk):
    tq = DSA_TQ
    nq = seq // tq
    wide = lambda t: pl.BlockSpec((tq, DSA_W), lambda b, i: (b * nq + i, t))
    full = lambda t: pl.BlockSpec((seq, DSA_W), lambda b, i: (b, t))
    return pl.pallas_call(
        functools.partial(_dsa_kernel, topk=topk, nq=nq),
        grid=(batch, nq),
        in_specs=[wide(EV_IQ),
                  pl.BlockSpec((seq, LANES), lambda b, i: (b, 0)),
                  pl.BlockSpec((1, LANES, seq), lambda b, i: (b, 0, 0)),
                  wide(EV_DQ), full(EV_DK), full(EV_DV), wide(EV_DG)],
        out_specs=pl.BlockSpec((tq, DSA_W), lambda b, i: (b * nq + i, 0)),
        out_shape=jax.ShapeDtypeStruct((batch * seq, DSA_W), BF16),
        scratch_shapes=[pltpu.VMEM((seq, tq), F32),
                        pltpu.VMEM((DSA_HEADS, HEAD_DIM + ONES_ROWS, seq), BF16),
                        pltpu.VMEM((1, tq), I32),
                        pltpu.VMEM((DSA_HEADS, tq), F32),
                        pltpu.VMEM((DSA_HEADS, HEAD_DIM + ONES_ROWS, tq), F32)],
        compiler_params=_params(("parallel", "arbitrary")),
        name="dsa_attention",
    )(hmain, small, small_t, hmain, hmain, hmain, hmain)


def _out_proj_kernel(*refs, ny, normed):
    y_refs, w_ref, x_ref = refs[:ny], refs[ny], refs[ny + 1]
    if normed:
        g_ref, o_ref, n_ref = refs[ny + 2:]
    else:
        (o_ref,) = refs[ny + 2:]
    acc = x_ref[...]
    off = 0
    for y_ref in y_refs:
        kdim = y_ref.shape[1]
        acc = acc + _dot(y_ref[...], w_ref[off:off + kdim, :])
        off += kdim
    o_ref[...] = acc
    if normed:
        y = acc * lax.rsqrt(jnp.mean(acc * acc, axis=-1, keepdims=True) + EPS)
        n_ref[...] = (y * g_ref[...]).astype(n_ref.dtype)


def _out_proj(ys, w, x2, next_gain=None, tm=512):
    m, d = x2.shape
    row = pl.BlockSpec((tm, d), lambda i: (i, 0))
    normed = next_gain is not None
    extra = ([next_gain.reshape(1, d)], [pl.BlockSpec((1, d), lambda i: (0, 0))]) if normed else ([], [])
    out = pl.pallas_call(
        functools.partial(_out_proj_kernel, ny=len(ys), normed=normed),
        grid=(m // tm,),
        in_specs=[pl.BlockSpec((tm, y.shape[1]), lambda i: (i, 0)) for y in ys]
                 + [pl.BlockSpec(w.shape, lambda i: (0, 0)), row] + extra[1],
        out_specs=[row, row] if normed else [row],
        out_shape=[jax.ShapeDtypeStruct((m, d), F32)] + ([jax.ShapeDtypeStruct((m, d), BF16)] if normed else []),
        compiler_params=_params(("parallel",)),
        name="out_proj",
    )(*ys, w, x2, *extra[0])
    return (out[0], out[1]) if normed else (out[0], None)


OD_TN = 512
OD_NQ = SWA_W // OD_TN


def _odd_kinds(j, aux, nslab):
    gain_ref, c64_ref, s64_ref = aux
    qk = lambda idx: (lambda a: _rope(_half_rms(a, gain_ref[idx]), c64_ref[...], s64_ref[...], SWA_HEAD_DIM))
    nk = SWA_KV_W // LANES
    return [
        (j < OD_NQ, lambda: [(qk(0), 0, nslab)]),
        (j == OD_NQ, lambda: [(qk(1), 0, nk), (lambda a: a, nk, nslab)]),
        (j > OD_NQ, lambda: [(lambda a: a * jax.nn.sigmoid(a), 0, nslab)]),
    ]


def _odd_proj(xn, w, layer, gains, c64, s64, seq):
    gspec = pl.BlockSpec((2, 1, LANES), lambda t: (0, 0, 0))
    return _skewed_proj(xn, w, layer, False, lambda j: j, (gains, c64, s64), [gspec, None, None],
                        _odd_kinds, seq, OD_TN, w.shape[2] // OD_TN, "odd_proj")


def _swa_kernel(sink_ref, q_ref, kvp_ref, kvc_ref, *rest):
    *g_refs, o_ref = rest
    n = pl.program_id(1)
    w = SWA_WINDOW
    dh = SWA_HEAD_DIM
    group = SWA_HEADS // SWA_KV_HEADS
    gw = group * w
    kb = lax.broadcasted_iota(I32, (2 * w, gw), 0)
    qi = lax.broadcasted_iota(I32, (2 * w, gw), 1) & (w - 1)
    valid = (kb > qi) & (kb <= qi + w) & ((n > 0) | (kb >= w))
    bias = jnp.where(valid, 0.0, NEG)
    head_of_lane = lax.broadcasted_iota(I32, (1, gw), 1) // w
    ones = jnp.ones((ONES_ROWS, 2 * w), BF16)

    def band(col):
        sl = slice(col, col + dh)
        return jnp.concatenate([kvp_ref[:, sl], kvc_ref[:, sl]], axis=0)

    for kv in range(SWA_KV_HEADS):
        h0 = kv * group
        qg = jnp.concatenate([q_ref[:, (h0 + t) * dh:(h0 + t + 1) * dh] for t in range(group)], axis=0)
        z = bias + _dot_nt(band(kv * dh), qg)
        sink = jnp.zeros((1, gw), F32)
        for t in range(group):
            sink = jnp.where(head_of_lane == t, sink_ref[h0 + t] * LOG2E, sink)
        m = jnp.maximum(jnp.max(z, axis=0, keepdims=True), sink)
        vt = jnp.concatenate([band(SWA_KV_W + kv * dh).astype(F32).T.astype(BF16), ones], axis=0)
        pv = _dot(vt, jnp.exp2(z - m).astype(BF16))
        den = pv[dh:dh + 1] + jnp.exp2(sink - m)
        ot = (pv[0:dh] * (1.0 / den)).T
        o = jnp.concatenate([ot[t * w:(t + 1) * w] for t in range(group)], axis=1)
        sl = slice(h0 * dh, (h0 + group) * dh)
        o_ref[:, sl] = (o * g_refs[kv][...].astype(F32)).astype(o_ref.dtype)


def _swa_attention(hodd, sinks, batch, seq):
    w = SWA_WINDOW
    nb = seq // w
    kvw = 2 * SWA_KV_W
    kv_col = SWA_W // kvw
    gate = lambda kv: pl.BlockSpec((w, kvw), lambda b, n: (b * nb + n, kv_col + 1 + kv))
    return pl.pallas_call(
        _swa_kernel,
        grid=(batch, nb),
        in_specs=[pl.BlockSpec(memory_space=pltpu.SMEM),
                  pl.BlockSpec((w, SWA_W), lambda b, n: (b * nb + n, 0)),
                  pl.BlockSpec((w, kvw), lambda b, n: (b * nb + jnp.maximum(n - 1, 0), kv_col)),
                  pl.BlockSpec((w, kvw), lambda b, n: (b * nb + n, kv_col))]
                 + [gate(kv) for kv in range(SWA_KV_HEADS)],
        out_specs=pl.BlockSpec((w, SWA_W), lambda b, n: (b * nb + n, 0)),
        out_shape=jax.ShapeDtypeStruct((batch * seq, SWA_W), BF16),
        compiler_params=_params(("parallel", "arbitrary")),
        name="swa_attention",
    )(sinks, hodd, hodd, hodd, *([hodd] * SWA_KV_HEADS))


def _rope_tables(seq, dim):
    inv = 1.0 / (ROPE_THETA ** (jnp.arange(0, dim, 2, dtype=F32) / dim))
    ang = jnp.arange(seq, dtype=F32)[:, None] * inv[None, :]
    cos, sin = jnp.cos(ang), jnp.sin(ang)
    reps = LANES // dim
    return (jnp.tile(jnp.concatenate([cos, cos], -1), (1, reps)),
            jnp.tile(jnp.concatenate([-sin, sin], -1), (1, reps)))


def _cols(w, start, size):
    return lax.slice_in_dim(w, start, start + size, axis=1)


def _even_layer(x2, xn, next_gain, batch, seq, w_in_stack, j, b_f, g_fox, g_dsa, g_kidx, w_out, tabs, topk):
    c128, s128, c64, s64 = tabs
    ncol = w_in_stack.shape[2]
    row0 = j * ncol
    fgt_row = row0 + 4 * FOX_W
    idx_row = fgt_row + FOX_HEADS + 4 * DSA_W + IDX_W
    pad = LANES - (IDX_DIM + IDX_HEADS + FOX_HEADS)
    wt = jnp.swapaxes(w_in_stack, 1, 2).reshape(-1, w_in_stack.shape[1])
    qscale = HEAD_DIM ** -0.5 * LOG2E
    one = jnp.ones((LANES,), F32)
    gains = jnp.stack([g_fox[0] * qscale, g_fox[1], one, one,
                       g_dsa[0] * qscale, g_dsa[1], one, one, one]).reshape(EV_IQ + 1, 1, LANES)
    gk = jnp.concatenate([g_kidx, jnp.zeros((LANES - IDX_DIM,), F32)]).reshape(1, LANES)
    bf = jnp.concatenate([jnp.zeros((SM_CF,), F32), b_f, jnp.zeros((pad,), F32)]).reshape(1, LANES)

    hmain = _even_proj(xn, wt, row0, gains, c128, s128, c64, s64, seq)
    small, small_t = _even_small(xn, wt, idx_row, fgt_row, gk, bf, c64, s64, batch, seq)
    ya = _fox_attention(hmain, small, batch, seq)
    yb = _dsa_attention(hmain, small, small_t, batch, seq, topk)
    return _out_proj([ya, yb], w_out.astype(BF16), x2, next_gain)


def _odd_layer(x2, xn, next_gain, batch, seq, w_in_stack, j, g_qk, sinks, w_out, tabs):
    _, _, c64, s64 = tabs
    qscale = jnp.array([[SWA_HEAD_DIM ** -0.5 * LOG2E], [1.0]], F32)
    gains = jnp.tile(g_qk * qscale, (1, LANES // SWA_HEAD_DIM)).reshape(2, 1, LANES)
    hodd = _odd_proj(xn, w_in_stack, j, gains, c64, s64, seq)
    y = _swa_attention(hodd, sinks, batch, seq)
    return _out_proj([y], w_out.astype(BF16), x2, next_gain)


def kernel(x, norm_even, w_in_even, b_f_even, g_qk_fox, g_qk_dsa, g_kidx, w_out_even,
           norm_odd, w_in_odd, g_qk_swa, sinks, w_out_odd):
    batch, seq, d = x.shape
    depth = norm_even.shape[0] + norm_odd.shape[0]
    topk = min(IDX_TOPK_MAX, seq // 4)
    tabs = _rope_tables(seq, HEAD_DIM) + _rope_tables(seq, IDX_DIM)
    x2 = x.reshape(batch * seq, d)
    pre_gain = lambda layer: (norm_even, norm_odd)[layer % 2][layer // 2]
    xn = _rmsnorm(x2, pre_gain(0))
    for layer in range(depth):
        j = layer // 2
        next_gain = pre_gain(layer + 1) if layer + 1 < depth else None
        if layer % 2 == 0:
            x2, xn = _even_layer(x2, xn, next_gain, batch, seq, w_in_even, j, b_f_even[j], g_qk_fox[j],
                                 g_qk_dsa[j], g_kidx[j], w_out_even[j], tabs, topk)
        else:
            x2, xn = _odd_layer(x2, xn, next_gain, batch, seq, w_in_odd, j, g_qk_swa[j], sinks[j],
                                w_out_odd[j], tabs)
    return x2.reshape(batch, seq, d)
```

```python
import functools

import jax
import jax.numpy as jnp
from jax import lax
from jax.experimental import pallas as pl
from jax.experimental.pallas import tpu as pltpu

F32 = jnp.float32
BF16 = jnp.bfloat16
I32 = jnp.int32

D_MODEL = 2048
HEAD_DIM = 128
FOX_HEADS = 8
DSA_HEADS = 8
IDX_HEADS = 16
IDX_DIM = 64
IDX_TOPK_MAX = 256
SWA_HEADS = 32
SWA_KV_HEADS = 4
SWA_HEAD_DIM = 64
SWA_WINDOW = 128
ROPE_THETA = 10000.0
EPS = 1e-6
NEG = -1e30

FOX_W = FOX_HEADS * HEAD_DIM
DSA_W = DSA_HEADS * HEAD_DIM
IDX_W = IDX_HEADS * IDX_DIM
SWA_W = SWA_HEADS * SWA_HEAD_DIM
SWA_KV_W = SWA_KV_HEADS * SWA_HEAD_DIM

LANES = 128
VMEM_LIMIT = 56 * 2 ** 20

SM_IK = 0
SM_IW = IDX_DIM
SM_CF = IDX_DIM + IDX_HEADS

LOG2E = 1.4426950408889634
INT_MIN = -2 ** 31
IDX_BITS = 12


def _params(sem):
    return pltpu.CompilerParams(dimension_semantics=sem, vmem_limit_bytes=VMEM_LIMIT)


def _dot(a, b):
    return jnp.dot(a, b, preferred_element_type=F32)


def _dot_nt(a, b):
    return lax.dot_general(a, b, (((1,), (1,)), ((), ())), preferred_element_type=F32)


def _lane_iota(shape):
    return lax.broadcasted_iota(I32, shape, 1)


def _rope(y, cos, sin, dim):
    if dim == LANES:
        rot = pltpu.roll(y, LANES // 2, 1)
    else:
        half = dim // 2
        first = (_lane_iota(y.shape) & half) == 0
        rot = jnp.where(first, pltpu.roll(y, LANES - half, 1), pltpu.roll(y, half, 1))
    return y * cos + rot * sin


def _head_rms(a, gain):
    return a * lax.rsqrt(jnp.mean(a * a, axis=-1, keepdims=True) + EPS) * gain


def _half_rms(a, gain):
    lo = _lane_iota(a.shape) < 64
    sq = a * a
    ms_lo = jnp.sum(jnp.where(lo, sq, 0.0), axis=-1, keepdims=True) * (1.0 / 64)
    ms_hi = jnp.sum(jnp.where(lo, 0.0, sq), axis=-1, keepdims=True) * (1.0 / 64)
    return a * lax.rsqrt(jnp.where(lo, ms_lo, ms_hi) + EPS) * gain


def _rmsnorm_kernel(x_ref, g_ref, o_ref):
    x = x_ref[...]
    y = x * lax.rsqrt(jnp.mean(x * x, axis=-1, keepdims=True) + EPS)
    o_ref[...] = (y * g_ref[...]).astype(o_ref.dtype)


def _rmsnorm(x2, g, tm=512):
    m, d = x2.shape
    return pl.pallas_call(
        _rmsnorm_kernel,
        grid=(m // tm,),
        in_specs=[pl.BlockSpec((tm, d), lambda i: (i, 0)), pl.BlockSpec((1, d), lambda i: (0, 0))],
        out_specs=pl.BlockSpec((tm, d), lambda i: (i, 0)),
        out_shape=jax.ShapeDtypeStruct((m, d), BF16),
        compiler_params=_params(("parallel",)),
        name="rmsnorm",
    )(x2, g.reshape(1, d))


def _skewed_proj_kernel(xn_ref, w_ref, *rest, kinds, transposed, nj, nm, tn):
    *aux, o_ref, acc_ref = rest
    t = pl.program_id(0)
    last = pl.num_programs(0) - 1
    j = jnp.maximum(t - 1, 0) // nm

    def finish(parts):
        for fn, lo, hi in parts:
            for h in range(lo, hi):
                sl = slice(h * LANES, (h + 1) * LANES)
                o_ref[:, sl] = fn(acc_ref[:, sl]).astype(o_ref.dtype)

    def multiply():
        w = (w_ref[...] if transposed else w_ref[0]).astype(BF16)
        acc_ref[...] = _dot_nt(xn_ref[...], w) if transposed else _dot(xn_ref[...], w)

    @pl.when(t == 0)
    def _():
        multiply()

    nslab = tn // LANES
    for cond, parts in kinds(j, aux, nslab):
        @pl.when((t > 0) & (t < last) & cond)
        def _(parts=parts):
            finish(parts())
            multiply()

    for cond, parts in kinds(nj - 1, aux, nslab):
        if cond:
            @pl.when(t == last)
            def _(parts=parts):
                finish(parts())


def _skewed_proj(xn, w_stack, layer, transposed, locate, aux, aux_specs, kinds, seq, tn, nj, name, tm=1024):
    m, d = xn.shape
    nm = m // tm
    ntiles = nm * nj
    nrow = seq // tm
    cur = lambda t: jnp.minimum(t, ntiles - 1)
    prev = lambda t: jnp.maximum(t - 1, 0)
    tab = pl.BlockSpec((tm, LANES), lambda t: ((prev(t) % nm) % nrow, 0))
    specs = [tab if s is None else s for s in aux_specs]
    if transposed:
        wspec = pl.BlockSpec((pl.Element(tn), pl.Element(d)), lambda t: (locate(cur(t) // nm), 0))
    else:
        wspec = pl.BlockSpec((1, d, tn), lambda t: (layer, 0, locate(cur(t) // nm)))
    return pl.pallas_call(
        functools.partial(_skewed_proj_kernel, kinds=kinds, transposed=transposed, nj=nj, nm=nm, tn=tn),
        grid=(ntiles + 1,),
        in_specs=[pl.BlockSpec((tm, d), lambda t: (cur(t) % nm, 0)), wspec] + specs,
        out_specs=pl.BlockSpec((tm, tn), lambda t: (prev(t) % nm, prev(t) // nm)),
        out_shape=jax.ShapeDtypeStruct((m, nj * tn), BF16),
        scratch_shapes=[pltpu.VMEM((tm, tn), F32)],
        compiler_params=_params(("arbitrary",)),
        name=name,
    )(xn, w_stack, *aux)


EV_TN = 1024
EV_FQ, EV_FK, EV_FV, EV_FG, EV_DQ, EV_DK, EV_DV, EV_DG, EV_IQ = range(9)
EV_NFOX = 4


def _even_kinds(j, aux, nslab):
    gain_ref, c128_ref, s128_ref, c64_ref, s64_ref = aux
    gain = lambda: gain_ref[jnp.minimum(j, EV_IQ)]
    norm = lambda: [(lambda a: _head_rms(a, gain()), 0, nslab)]
    norm_rope = lambda: [(lambda a: _rope(_head_rms(a, gain()), c128_ref[...], s128_ref[...], HEAD_DIM), 0, nslab)]
    cast = lambda: [(lambda a: a, 0, nslab)]
    silu = lambda: [(lambda a: a * jax.nn.sigmoid(a), 0, nslab)]
    rope64 = lambda: [(lambda a: _rope(a, c64_ref[...], s64_ref[...], IDX_DIM), 0, nslab)]
    return [
        ((j == EV_FQ) | (j == EV_FK), norm),
        ((j == EV_DQ) | (j == EV_DK), norm_rope),
        ((j == EV_FV) | (j == EV_DV), cast),
        ((j == EV_FG) | (j == EV_DG), silu),
        (j == EV_IQ, rope64),
    ]


def _even_proj(xn, wt, row0, gains, c128, s128, c64, s64, seq):
    gspec = pl.BlockSpec(gains.shape, lambda t: (0, 0, 0))
    row = lambda j: (row0 // 8 + j * (EV_TN // 8) + (j + EV_NFOX) // (2 * EV_NFOX)) * 8
    return _skewed_proj(xn, wt, None, True, row, (gains, c128, s128, c64, s64),
                        [gspec, None, None, None, None], _even_kinds, seq, EV_TN, EV_IQ + 1, "even_proj")


def _even_small_kernel(xn_ref, wi_ref, wf_ref, gk_ref, bf_ref, c64_ref, s64_ref, o_ref, ot_ref):
    d = xn_ref.shape[1]
    pad = jnp.zeros((LANES - wi_ref.shape[0] - wf_ref.shape[0], d), F32)
    wt = jnp.concatenate([wi_ref[...], wf_ref[...], pad], axis=0).astype(BF16)
    h = _dot_nt(xn_ref[...], wt)
    lane = _lane_iota(h.shape)
    is_ik = lane < SM_IW
    ms = jnp.sum(jnp.where(is_ik, h * h, 0.0), axis=-1, keepdims=True) * (1.0 / IDX_DIM)
    ik = _rope(h * lax.rsqrt(ms + EPS) * gk_ref[...], c64_ref[...], s64_ref[...], IDX_DIM)
    iw = h * (IDX_HEADS ** -0.5 * IDX_DIM ** -0.5)
    c = jax.nn.log_sigmoid(h + bf_ref[...])
    row = lax.broadcasted_iota(I32, h.shape, 0)
    d = 1
    while d < h.shape[0]:
        c = c + jnp.where(row >= d, pltpu.roll(c, d, 0), 0.0)
        d *= 2
    out = jnp.where(is_ik, ik, jnp.where(lane < SM_CF, iw, c))
    o_ref[...] = out
    ot_ref[0] = out.T


def _even_small(xn, wt, idx_row, fgt_row, gk, bf, c64, s64, batch, seq):
    d = xn.shape[1]
    vec = pl.BlockSpec((1, LANES), lambda b: (0, 0))
    tab = pl.BlockSpec((seq, LANES), lambda b: (0, 0))
    return pl.pallas_call(
        _even_small_kernel,
        grid=(batch,),
        in_specs=[pl.BlockSpec((seq, d), lambda b: (b, 0)),
                  pl.BlockSpec((pl.Element(IDX_DIM + IDX_HEADS), pl.Element(d)), lambda b: (idx_row, 0)),
                  pl.BlockSpec((pl.Element(FOX_HEADS), pl.Element(d)), lambda b: (fgt_row, 0)),
                  vec, vec, tab, tab],
        out_specs=[pl.BlockSpec((seq, LANES), lambda b: (b, 0)),
                   pl.BlockSpec((1, LANES, seq), lambda b: (b, 0, 0))],
        out_shape=[jax.ShapeDtypeStruct((batch * seq, LANES), F32),
                   jax.ShapeDtypeStruct((batch, LANES, seq), F32)],
        compiler_params=_params(("parallel",)),
        name="even_small",
    )(xn, wt, wt, gk, bf, c64, s64)


FOX_TQ = 256
FOX_HPS = 8
ONES_ROWS = 16


def _fox_kernel(q_ref, k_ref, v_ref, g_ref, small_ref, o_ref, ka_ref, vt_ref, *, nq):
    hg = pl.program_id(1)
    i = pl.program_id(2)
    tq = FOX_TQ
    seq = k_ref.shape[0]
    heads = [slice(u * HEAD_DIM, (u + 1) * HEAD_DIM) for u in range(FOX_HPS)]
    lane = _lane_iota((seq, LANES))

    @pl.when(i == 0)
    def _():
        sm = small_ref[...]
        for u, sl in enumerate(heads):
            vt_ref[u, 0:HEAD_DIM, :] = v_ref[:, sl].astype(F32).T.astype(BF16)
            vt_ref[u, HEAD_DIM:, :] = jnp.ones((ONES_ROWS, seq), BF16)
            ck = jnp.sum(jnp.where(lane == SM_CF + hg * FOX_HPS + u, sm, 0.0), axis=1, keepdims=True)
            neg = jnp.broadcast_to(ck * -LOG2E, (seq, LANES))
            hi = neg.astype(BF16).astype(F32)
            mid = (neg - hi).astype(BF16).astype(F32)
            lo = neg - hi - mid
            extra = jnp.where(lane == 0, hi, jnp.where(lane == 1, mid, jnp.where(lane == 2, lo, 0.0)))
            ka_ref[u, :, 0:HEAD_DIM] = k_ref[:, sl]
            ka_ref[u, :, HEAD_DIM:] = extra.astype(BF16)

    ones3 = jnp.where(_lane_iota((tq, LANES)) < 3, 1.0, 0.0).astype(BF16)
    qa = [jnp.concatenate([q_ref[:, sl], ones3], axis=1) for sl in heads]

    def qk(u, j):
        return _dot_nt(ka_ref[u, pl.ds(j * tq, tq), :], qa[u])

    def step(u, j, z, carry, diagonal):
        m, acc = carry
        if diagonal:
            key = lax.broadcasted_iota(I32, (tq, tq), 0)
            qry = lax.broadcasted_iota(I32, (tq, tq), 1)
            z = jnp.where(key <= qry, z, NEG)
        m_new = jnp.maximum(m, jnp.max(z, axis=0, keepdims=True))
        p = jnp.exp2(z - m_new)
        acc = jnp.exp2(m - m_new) * acc + _dot(vt_ref[u, :, pl.ds(j * tq, tq)], p.astype(BF16))
        return m_new, acc

    init = (jnp.full((1, tq), -jnp.inf, F32), jnp.zeros((HEAD_DIM + ONES_ROWS, tq), F32))
    for i_static in range(nq):
        @pl.when(i == i_static)
        def _(i_static=i_static):
            carry = [init] * FOX_HPS
            z_next = [qk(u, 0) for u in range(FOX_HPS)]
            for j in range(i_static + 1):
                for u in range(FOX_HPS):
                    z = z_next[u]
                    if j < i_static:
                        z_next[u] = qk(u, j + 1)
                    carry[u] = step(u, j, z, carry[u], j == i_static)
            for u, sl in enumerate(heads):
                _, acc = carry[u]
                out = (acc[0:HEAD_DIM] * (1.0 / acc[HEAD_DIM:HEAD_DIM + 1])).T
                o_ref[:, sl] = (out * g_ref[:, sl].astype(F32)).astype(o_ref.dtype)


def _fox_attention(hmain, small, batch, seq):
    tq = FOX_TQ
    nq = seq // tq
    width = FOX_HPS * HEAD_DIM
    ng = FOX_W // width
    return pl.pallas_call(
        functools.partial(_fox_kernel, nq=nq),
        grid=(batch, ng, nq),
        in_specs=[pl.BlockSpec((tq, width), lambda b, h, i: (b * nq + i, EV_FQ * ng + h)),
                  pl.BlockSpec((seq, width), lambda b, h, i: (b, EV_FK * ng + h)),
                  pl.BlockSpec((seq, width), lambda b, h, i: (b, EV_FV * ng + h)),
                  pl.BlockSpec((tq, width), lambda b, h, i: (b * nq + i, EV_FG * ng + h)),
                  pl.BlockSpec((seq, LANES), lambda b, h, i: (b, 0))],
        out_specs=pl.BlockSpec((tq, width), lambda b, h, i: (b * nq + i, h)),
        out_shape=jax.ShapeDtypeStruct((batch * seq, FOX_W), BF16),
        scratch_shapes=[pltpu.VMEM((FOX_HPS, seq, 2 * HEAD_DIM), BF16),
                        pltpu.VMEM((FOX_HPS, HEAD_DIM + ONES_ROWS, seq), BF16)],
        compiler_params=_params(("parallel", "parallel", "arbitrary")),
        name="fox_attention",
    )(hmain, hmain, hmain, hmain, small)


DSA_TQ = 256


def _key_to_f32(key):
    return pltpu.bitcast(key ^ ((key >> 31) & 0x7FFFFFFF), F32)


def _dsa_kernel(iq_ref, small_ref, smallt_ref, q_ref, k_ref, v_ref, g_ref, o_ref,
                score_ref, vt_ref, lim_ref, m_ref, acc_ref, *, topk, nq):
    i = pl.program_id(1)
    tq = DSA_TQ
    nchunk = i + 1
    half = tq // 2
    key0 = lax.broadcasted_iota(I32, (tq, tq), 0)
    qry = i * tq + lax.broadcasted_iota(I32, (tq, tq), 1)

    @pl.when(i == 0)
    def _():
        for hh in range(DSA_HEADS):
            sl = slice(hh * HEAD_DIM, (hh + 1) * HEAD_DIM)
            vt_ref[hh, 0:HEAD_DIM, :] = v_ref[:, sl].astype(F32).T.astype(BF16)
            vt_ref[hh, HEAD_DIM:, :] = jnp.ones((ONES_ROWS, v_ref.shape[0]), BF16)

    def chunk_off(c):
        return pl.multiple_of(c * tq, tq)

    def score_chunk(c, _):
        off = chunk_off(c)
        ikc = small_ref[pl.ds(off, tq), SM_IK:SM_IK + IDX_DIM].astype(BF16)

        def logits(hh):
            return _dot_nt(ikc, iq_ref[:, hh * IDX_DIM:(hh + 1) * IDX_DIM])

        acc = jnp.zeros((tq, tq), F32)
        nxt = logits(0)
        for hh in range(IDX_HEADS):
            cur = nxt
            if hh + 1 < IDX_HEADS:
                nxt = logits(hh + 1)
            w = smallt_ref[0, SM_IW + hh:SM_IW + hh + 1, pl.ds(pl.multiple_of(i * tq, tq), tq)]
            acc = acc + w * jnp.maximum(cur, 0.0)
        score_ref[pl.ds(off, tq), :] = jnp.where(off + key0 <= qry, acc, NEG)
        return 0

    lax.fori_loop(0, nchunk, score_chunk, 0)

    def count(pred):
        def body(c, tot):
            off = chunk_off(c)
            hit = jnp.where(pred(score_ref[pl.ds(off, tq), :], off + key0), 1.0, 0.0)
            return tot + jnp.sum(hit.reshape(tq // 8, 8, tq), axis=0)
        tot = lax.fori_loop(0, nchunk, body, jnp.zeros((8, tq), F32))
        return jnp.sum(tot, axis=0, keepdims=True)

    def thr_step(it, res, nlive):
        cand = res + jnp.left_shift(jnp.int32(1), 31 - it)
        cand_f = _key_to_f32(cand)
        tot = jnp.zeros((8, tq), F32)
        for c in range(nlive):
            hit = jnp.where(score_ref[c * tq:(c + 1) * tq, :] >= cand_f, 1.0, 0.0)
            tot = tot + jnp.sum(hit.reshape(tq // 8, 8, tq), axis=0)
        cnt = jnp.sum(tot, axis=0, keepdims=True)
        return jnp.where(cnt >= topk, cand, res)

    for i_static in range(nq):
        @pl.when(i == i_static)
        def _(i_static=i_static):
            key = lax.fori_loop(0, 32, functools.partial(thr_step, nlive=i_static + 1),
                                jnp.full((1, tq), INT_MIN, I32))
            lim_ref[...] = key

    thr_key = lim_ref[...]
    thr = jnp.where(thr_key == INT_MIN, -jnp.inf, _key_to_f32(thr_key))
    n_gt = count(lambda sc, ki: sc > thr)
    n_ge = count(lambda sc, ki: sc >= thr)
    need = topk - n_gt

    lim_ref[...] = jnp.full((1, tq), 2 ** IDX_BITS, I32)

    @pl.when(jnp.max(n_ge) > topk)
    def _():
        def lim_step(it, res):
            cand = res + jnp.left_shift(jnp.int32(1), IDX_BITS - 1 - it)
            cnt = count(lambda sc, ki: (sc == thr) & (ki < cand))
            return jnp.where(cnt <= need, cand, res)
        lim_ref[...] = lax.fori_loop(0, IDX_BITS, lim_step, jnp.zeros((1, tq), I32))

    lim = lim_ref[...]

    m_ref[...] = jnp.full(m_ref.shape, -jnp.inf, F32)
    acc_ref[...] = jnp.zeros(acc_ref.shape, F32)

    def attn_chunk(c, _):
        off = chunk_off(c)
        sc = score_ref[pl.ds(off, tq), :]
        ki = off + key0
        sel = ((sc > thr) | ((sc == thr) & (ki < lim))) & (ki <= qry)
        bias = jnp.where(sel, 0.0, NEG)

        def qk(hh):
            sl = slice(hh * HEAD_DIM, (hh + 1) * HEAD_DIM)
            return _dot_nt(k_ref[pl.ds(off, tq), sl], q_ref[:, sl])

        heads = range(DSA_HEADS)
        zs = [bias + qk(hh) for hh in heads]
        ms = [m_ref[hh:hh + 1, :] for hh in heads]
        m_news = [jnp.maximum(ms[hh], jnp.max(zs[hh], axis=0, keepdims=True)) for hh in heads]
        ps = [jnp.exp2(zs[hh] - m_news[hh]).astype(BF16) for hh in heads]
        pvs = [_dot(vt_ref[hh, :, pl.ds(off, tq)], ps[hh]) for hh in heads]
        for hh in heads:
            acc_ref[hh] = jnp.exp2(ms[hh] - m_news[hh]) * acc_ref[hh] + pvs[hh]
            m_ref[hh:hh + 1, :] = m_news[hh]
        return 0

    lax.fori_loop(0, nchunk, attn_chunk, 0)
    for hh in range(DSA_HEADS):
        sl = slice(hh * HEAD_DIM, (hh + 1) * HEAD_DIM)
        out = (acc_ref[hh, 0:HEAD_DIM, :] * (1.0 / acc_ref[hh, HEAD_DIM:HEAD_DIM + 1, :])).T
        o_ref[:, sl] = (out * g_ref[:, sl].astype(F32)).astype(o_ref.dtype)


def _dsa_attention(hmain, small, small_t, batch, seq, topk):
    tq = DSA_TQ
    nq = seq // tq
    wide = lambda t: pl.BlockSpec((tq, DSA_W), lambda b, i: (b * nq + i, t))
    full = lambda t: pl.BlockSpec((seq, DSA_W), lambda b, i: (b, t))
    return pl.pallas_call(
        functools.partial(_dsa_kernel, topk=topk, nq=nq),
        grid=(batch, nq),
        in_specs=[wide(EV_IQ),
                  pl.BlockSpec((seq, LANES), lambda b, i: (b, 0)),
                  pl.BlockSpec((1, LANES, seq), lambda b, i: (b, 0, 0)),
                  wide(EV_DQ), full(EV_DK), full(EV_DV), wide(EV_DG)],
        out_specs=pl.BlockSpec((tq, DSA_W), lambda b, i: (b * nq + i, 0)),
        out_shape=jax.ShapeDtypeStruct((batch * seq, DSA_W), BF16),
        scratch_shapes=[pltpu.VMEM((seq, tq), F32),
                        pltpu.VMEM((DSA_HEADS, HEAD_DIM + ONES_ROWS, seq), BF16),
                        pltpu.VMEM((1, tq), I32),
                        pltpu.VMEM((DSA_HEADS, tq), F32),
                        pltpu.VMEM((DSA_HEADS, HEAD_DIM + ONES_ROWS, tq), F32)],
        compiler_params=_params(("parallel", "arbitrary")),
        name="dsa_attention",
    )(hmain, small, small_t, hmain, hmain, hmain, hmain)


def _out_proj_kernel(*refs, ny, normed):
    y_refs, w_ref, x_ref = refs[:ny], refs[ny], refs[ny + 1]
    if normed:
        g_ref, o_ref, n_ref, w16_ref = refs[ny + 2:]
    else:
        o_ref, w16_ref = refs[ny + 2:]

    @pl.when(pl.program_id(0) == 0)
    def _():
        w16_ref[...] = w_ref[0].astype(BF16)

    acc = x_ref[...]
    off = 0
    for y_ref in y_refs:
        kdim = y_ref.shape[1]
        acc = acc + _dot(y_ref[...], w16_ref[off:off + kdim, :])
        off += kdim
    o_ref[...] = acc
    if normed:
        y = acc * lax.rsqrt(jnp.mean(acc * acc, axis=-1, keepdims=True) + EPS)
        n_ref[...] = (y * g_ref[...]).astype(n_ref.dtype)


def _out_proj(ys, w_stack, layer, x2, next_gain=None, tm=512):
    m, d = x2.shape
    kdim = w_stack.shape[1]
    row = pl.BlockSpec((tm, d), lambda i: (i, 0))
    normed = next_gain is not None
    extra = ([next_gain.reshape(1, d)], [pl.BlockSpec((1, d), lambda i: (0, 0))]) if normed else ([], [])
    out = pl.pallas_call(
        functools.partial(_out_proj_kernel, ny=len(ys), normed=normed),
        grid=(m // tm,),
        in_specs=[pl.BlockSpec((tm, y.shape[1]), lambda i: (i, 0)) for y in ys]
                 + [pl.BlockSpec((1, kdim, d), lambda i: (layer, 0, 0), pipeline_mode=pl.Buffered(1)), row] + extra[1],
        out_specs=[row, row] if normed else [row],
        out_shape=[jax.ShapeDtypeStruct((m, d), F32)] + ([jax.ShapeDtypeStruct((m, d), BF16)] if normed else []),
        scratch_shapes=[pltpu.VMEM((kdim, d), BF16)],
        compiler_params=_params(("arbitrary",)),
        name="out_proj",
    )(*ys, w_stack, x2, *extra[0])
    return (out[0], out[1]) if normed else (out[0], None)


OD_TN = 512
OD_NQ = SWA_W // OD_TN


def _odd_kinds(j, aux, nslab):
    gain_ref, c64_ref, s64_ref = aux
    qk = lambda idx: (lambda a: _rope(_half_rms(a, gain_ref[idx]), c64_ref[...], s64_ref[...], SWA_HEAD_DIM))
    nk = SWA_KV_W // LANES
    return [
        (j < OD_NQ, lambda: [(qk(0), 0, nslab)]),
        (j == OD_NQ, lambda: [(qk(1), 0, nk), (lambda a: a, nk, nslab)]),
        (j > OD_NQ, lambda: [(lambda a: a * jax.nn.sigmoid(a), 0, nslab)]),
    ]


def _odd_proj(xn, w, layer, gains, c64, s64, seq):
    gspec = pl.BlockSpec((2, 1, LANES), lambda t: (0, 0, 0))
    return _skewed_proj(xn, w, layer, False, lambda j: j, (gains, c64, s64), [gspec, None, None],
                        _odd_kinds, seq, OD_TN, w.shape[2] // OD_TN, "odd_proj")


def _swa_kernel(sink_ref, q_ref, kvp_ref, kvc_ref, *rest):
    *g_refs, o_ref = rest
    n = pl.program_id(1)
    w = SWA_WINDOW
    dh = SWA_HEAD_DIM
    group = SWA_HEADS // SWA_KV_HEADS
    gw = group * w
    kb = lax.broadcasted_iota(I32, (2 * w, gw), 0)
    qi = lax.broadcasted_iota(I32, (2 * w, gw), 1) & (w - 1)
    valid = (kb > qi) & (kb <= qi + w) & ((n > 0) | (kb >= w))
    bias = jnp.where(valid, 0.0, NEG)
    head_of_lane = lax.broadcasted_iota(I32, (1, gw), 1) // w
    ones = jnp.ones((ONES_ROWS, 2 * w), BF16)

    def band(col):
        sl = slice(col, col + dh)
        return jnp.concatenate([kvp_ref[:, sl], kvc_ref[:, sl]], axis=0)

    for kv in range(SWA_KV_HEADS):
        h0 = kv * group
        qg = jnp.concatenate([q_ref[:, (h0 + t) * dh:(h0 + t + 1) * dh] for t in range(group)], axis=0)
        z = bias + _dot_nt(band(kv * dh), qg)
        sink = jnp.zeros((1, gw), F32)
        for t in range(group):
            sink = jnp.where(head_of_lane == t, sink_ref[h0 + t] * LOG2E, sink)
        m = jnp.maximum(jnp.max(z, axis=0, keepdims=True), sink)
        vt = jnp.concatenate([band(SWA_KV_W + kv * dh).astype(F32).T.astype(BF16), ones], axis=0)
        pv = _dot(vt, jnp.exp2(z - m).astype(BF16))
        den = pv[dh:dh + 1] + jnp.exp2(sink - m)
        ot = (pv[0:dh] * (1.0 / den)).T
        o = jnp.concatenate([ot[t * w:(t + 1) * w] for t in range(group)], axis=1)
        sl = slice(h0 * dh, (h0 + group) * dh)
        o_ref[:, sl] = (o * g_refs[kv][...].astype(F32)).astype(o_ref.dtype)


def _swa_attention(hodd, sinks, batch, seq):
    w = SWA_WINDOW
    nb = seq // w
    kvw = 2 * SWA_KV_W
    kv_col = SWA_W // kvw
    gate = lambda kv: pl.BlockSpec((w, kvw), lambda b, n: (b * nb + n, kv_col + 1 + kv))
    return pl.pallas_call(
        _swa_kernel,
        grid=(batch, nb),
        in_specs=[pl.BlockSpec(memory_space=pltpu.SMEM),
                  pl.BlockSpec((w, SWA_W), lambda b, n: (b * nb + n, 0)),
                  pl.BlockSpec((w, kvw), lambda b, n: (b * nb + jnp.maximum(n - 1, 0), kv_col)),
                  pl.BlockSpec((w, kvw), lambda b, n: (b * nb + n, kv_col))]
                 + [gate(kv) for kv in range(SWA_KV_HEADS)],
        out_specs=pl.BlockSpec((w, SWA_W), lambda b, n: (b * nb + n, 0)),
        out_shape=jax.ShapeDtypeStruct((batch * seq, SWA_W), BF16),
        compiler_params=_params(("parallel", "arbitrary")),
        name="swa_attention",
    )(sinks, hodd, hodd, hodd, *([hodd] * SWA_KV_HEADS))


def _rope_tables(seq, dim):
    inv = 1.0 / (ROPE_THETA ** (jnp.arange(0, dim, 2, dtype=F32) / dim))
    ang = jnp.arange(seq, dtype=F32)[:, None] * inv[None, :]
    cos, sin = jnp.cos(ang), jnp.sin(ang)
    reps = LANES // dim
    return (jnp.tile(jnp.concatenate([cos, cos], -1), (1, reps)),
            jnp.tile(jnp.concatenate([-sin, sin], -1), (1, reps)))


def _cols(w, start, size):
    return lax.slice_in_dim(w, start, start + size, axis=1)


def _even_layer(x2, xn, next_gain, batch, seq, w_in_stack, j, b_f, g_fox, g_dsa, g_kidx, w_out_stack, tabs, topk):
    c128, s128, c64, s64 = tabs
    ncol = w_in_stack.shape[2]
    row0 = j * ncol
    fgt_row = row0 + 4 * FOX_W
    idx_row = fgt_row + FOX_HEADS + 4 * DSA_W + IDX_W
    pad = LANES - (IDX_DIM + IDX_HEADS + FOX_HEADS)
    wt = jnp.swapaxes(w_in_stack, 1, 2).reshape(-1, w_in_stack.shape[1])
    qscale = HEAD_DIM ** -0.5 * LOG2E
    one = jnp.ones((LANES,), F32)
    gains = jnp.stack([g_fox[0] * qscale, g_fox[1], one, one,
                       g_dsa[0] * qscale, g_dsa[1], one, one, one]).reshape(EV_IQ + 1, 1, LANES)
    gk = jnp.concatenate([g_kidx, jnp.zeros((LANES - IDX_DIM,), F32)]).reshape(1, LANES)
    bf = jnp.concatenate([jnp.zeros((SM_CF,), F32), b_f, jnp.zeros((pad,), F32)]).reshape(1, LANES)

    hmain = _even_proj(xn, wt, row0, gains, c128, s128, c64, s64, seq)
    small, small_t = _even_small(xn, wt, idx_row, fgt_row, gk, bf, c64, s64, batch, seq)
    ya = _fox_attention(hmain, small, batch, seq)
    yb = _dsa_attention(hmain, small, small_t, batch, seq, topk)
    return _out_proj([ya, yb], w_out_stack, j, x2, next_gain)


def _odd_layer(x2, xn, next_gain, batch, seq, w_in_stack, j, g_qk, sinks, w_out_stack, tabs):
    _, _, c64, s64 = tabs
    qscale = jnp.array([[SWA_HEAD_DIM ** -0.5 * LOG2E], [1.0]], F32)
    gains = jnp.tile(g_qk * qscale, (1, LANES // SWA_HEAD_DIM)).reshape(2, 1, LANES)
    hodd = _odd_proj(xn, w_in_stack, j, gains, c64, s64, seq)
    y = _swa_attention(hodd, sinks, batch, seq)
    return _out_proj([y], w_out_stack, j, x2, next_gain)


def kernel(x, norm_even, w_in_even, b_f_even, g_qk_fox, g_qk_dsa, g_kidx, w_out_even,
           norm_odd, w_in_odd, g_qk_swa, sinks, w_out_odd):
    batch, seq, d = x.shape
    depth = norm_even.shape[0] + norm_odd.shape[0]
    topk = min(IDX_TOPK_MAX, seq // 4)
    tabs = _rope_tables(seq, HEAD_DIM) + _rope_tables(seq, IDX_DIM)
    x2 = x.reshape(batch * seq, d)
    pre_gain = lambda layer: (norm_even, norm_odd)[layer % 2][layer // 2]
    xn = _rmsnorm(x2, pre_gain(0))
    for layer in range(depth):
        j = layer // 2
        next_gain = pre_gain(layer + 1) if layer + 1 < depth else None
        if layer % 2 == 0:
            x2, xn = _even_layer(x2, xn, next_gain, batch, seq, w_in_even, j, b_f_even[j], g_qk_fox[j],
                                 g_qk_dsa[j], g_kidx[j], w_out_even, tabs, topk)
        else:
            x2, xn = _odd_layer(x2, xn, next_gain, batch, seq, w_in_odd, j, g_qk_swa[j], sinks[j],
                                w_out_odd, tabs)
    return x2.reshape(batch, seq, d)
```

```python
import functools

import jax
import jax.numpy as jnp
from jax import lax
from jax.experimental import pallas as pl
from jax.experimental.pallas import tpu as pltpu

F32 = jnp.float32
BF16 = jnp.bfloat16
I32 = jnp.int32

D_MODEL = 2048
HEAD_DIM = 128
FOX_HEADS = 8
DSA_HEADS = 8
IDX_HEADS = 16
IDX_DIM = 64
IDX_TOPK_MAX = 256
SWA_HEADS = 32
SWA_KV_HEADS = 4
SWA_HEAD_DIM = 64
SWA_WINDOW = 128
ROPE_THETA = 10000.0
EPS = 1e-6
NEG = -1e30

FOX_W = FOX_HEADS * HEAD_DIM
DSA_W = DSA_HEADS * HEAD_DIM
IDX_W = IDX_HEADS * IDX_DIM
SWA_W = SWA_HEADS * SWA_HEAD_DIM
SWA_KV_W = SWA_KV_HEADS * SWA_HEAD_DIM

LANES = 128
VMEM_LIMIT = 56 * 2 ** 20

SM_IK = 0
SM_IW = IDX_DIM
SM_CF = IDX_DIM + IDX_HEADS

LOG2E = 1.4426950408889634
INT_MIN = -2 ** 31
IDX_BITS = 12


def _params(sem):
    return pltpu.CompilerParams(dimension_semantics=sem, vmem_limit_bytes=VMEM_LIMIT)


def _dot(a, b):
    return jnp.dot(a, b, preferred_element_type=F32)


def _dot_nt(a, b):
    return lax.dot_general(a, b, (((1,), (1,)), ((), ())), preferred_element_type=F32)


def _lane_iota(shape):
    return lax.broadcasted_iota(I32, shape, 1)


def _rope(y, cos, sin, dim):
    if dim == LANES:
        rot = pltpu.roll(y, LANES // 2, 1)
    else:
        half = dim // 2
        first = (_lane_iota(y.shape) & half) == 0
        rot = jnp.where(first, pltpu.roll(y, LANES - half, 1), pltpu.roll(y, half, 1))
    return y * cos + rot * sin


def _head_rms(a, gain):
    return a * lax.rsqrt(jnp.mean(a * a, axis=-1, keepdims=True) + EPS) * gain


def _half_rms(a, gain):
    lo = _lane_iota(a.shape) < 64
    sq = a * a
    ms_lo = jnp.sum(jnp.where(lo, sq, 0.0), axis=-1, keepdims=True) * (1.0 / 64)
    ms_hi = jnp.sum(jnp.where(lo, 0.0, sq), axis=-1, keepdims=True) * (1.0 / 64)
    return a * lax.rsqrt(jnp.where(lo, ms_lo, ms_hi) + EPS) * gain


def _rmsnorm_kernel(x_ref, g_ref, o_ref):
    x = x_ref[...]
    y = x * lax.rsqrt(jnp.mean(x * x, axis=-1, keepdims=True) + EPS)
    o_ref[...] = (y * g_ref[...]).astype(o_ref.dtype)


def _rmsnorm(x2, g, tm=512):
    m, d = x2.shape
    return pl.pallas_call(
        _rmsnorm_kernel,
        grid=(m // tm,),
        in_specs=[pl.BlockSpec((tm, d), lambda i: (i, 0)), pl.BlockSpec((1, d), lambda i: (0, 0))],
        out_specs=pl.BlockSpec((tm, d), lambda i: (i, 0)),
        out_shape=jax.ShapeDtypeStruct((m, d), BF16),
        compiler_params=_params(("parallel",)),
        name="rmsnorm",
    )(x2, g.reshape(1, d))


def _skewed_proj_kernel(xn_ref, w_ref, *rest, kinds, transposed, nj, tn):
    *aux, o_ref, acc_ref = rest
    t = pl.program_id(0)
    last = pl.num_programs(0) - 1
    j = (t + nj - 1) % nj

    def finish(parts):
        for fn, lo, hi in parts:
            for h in range(lo, hi):
                sl = slice(h * LANES, (h + 1) * LANES)
                o_ref[:, sl] = fn(acc_ref[:, sl]).astype(o_ref.dtype)

    def multiply():
        w = (w_ref[...] if transposed else w_ref[0]).astype(BF16)
        acc_ref[...] = _dot_nt(xn_ref[...], w) if transposed else _dot(xn_ref[...], w)

    @pl.when(t == 0)
    def _():
        multiply()

    nslab = tn // LANES
    for cond, parts in kinds(j, aux, nslab):
        @pl.when((t > 0) & (t < last) & cond)
        def _(parts=parts):
            finish(parts())
            multiply()

    for cond, parts in kinds(nj - 1, aux, nslab):
        if cond:
            @pl.when(t == last)
            def _(parts=parts):
                finish(parts())


def _skewed_proj(xn, w_stack, layer, transposed, locate, aux, aux_specs, kinds, seq, tn, nj, name, tm=1024):
    m, d = xn.shape
    ntiles = (m // tm) * nj
    nrow = seq // tm
    cur = lambda t: jnp.minimum(t, ntiles - 1)
    prev = lambda t: jnp.maximum(t - 1, 0)
    tab = pl.BlockSpec((tm, LANES), lambda t: ((prev(t) // nj) % nrow, 0))
    specs = [tab if s is None else s for s in aux_specs]
    if transposed:
        wspec = pl.BlockSpec((pl.Element(tn), pl.Element(d)), lambda t: (locate(cur(t) % nj), 0))
    else:
        wspec = pl.BlockSpec((1, d, tn), lambda t: (layer, 0, locate(cur(t) % nj)))
    return pl.pallas_call(
        functools.partial(_skewed_proj_kernel, kinds=kinds, transposed=transposed, nj=nj, tn=tn),
        grid=(ntiles + 1,),
        in_specs=[pl.BlockSpec((tm, d), lambda t: (cur(t) // nj, 0)), wspec] + specs,
        out_specs=pl.BlockSpec((tm, tn), lambda t: (prev(t) // nj, prev(t) % nj)),
        out_shape=jax.ShapeDtypeStruct((m, nj * tn), BF16),
        scratch_shapes=[pltpu.VMEM((tm, tn), F32)],
        compiler_params=_params(("arbitrary",)),
        name=name,
    )(xn, w_stack, *aux)


EV_TN = 1024
EV_FQ, EV_FK, EV_FV, EV_FG, EV_DQ, EV_DK, EV_DV, EV_DG, EV_IQ = range(9)
EV_NFOX = 4


def _even_kinds(j, aux, nslab):
    gain_ref, c128_ref, s128_ref, c64_ref, s64_ref = aux
    gain = lambda: gain_ref[jnp.minimum(j, EV_IQ)]
    norm = lambda: [(lambda a: _head_rms(a, gain()), 0, nslab)]
    norm_rope = lambda: [(lambda a: _rope(_head_rms(a, gain()), c128_ref[...], s128_ref[...], HEAD_DIM), 0, nslab)]
    cast = lambda: [(lambda a: a, 0, nslab)]
    silu = lambda: [(lambda a: a * jax.nn.sigmoid(a), 0, nslab)]
    rope64 = lambda: [(lambda a: _rope(a, c64_ref[...], s64_ref[...], IDX_DIM), 0, nslab)]
    return [
        ((j == EV_FQ) | (j == EV_FK), norm),
        ((j == EV_DQ) | (j == EV_DK), norm_rope),
        ((j == EV_FV) | (j == EV_DV), cast),
        ((j == EV_FG) | (j == EV_DG), silu),
        (j == EV_IQ, rope64),
    ]


def _even_proj(xn, wt, row0, gains, c128, s128, c64, s64, seq):
    gspec = pl.BlockSpec(gains.shape, lambda t: (0, 0, 0))
    row = lambda j: (row0 // 8 + j * (EV_TN // 8) + (j + EV_NFOX) // (2 * EV_NFOX)) * 8
    return _skewed_proj(xn, wt, None, True, row, (gains, c128, s128, c64, s64),
                        [gspec, None, None, None, None], _even_kinds, seq, EV_TN, EV_IQ + 1, "even_proj")


def _even_small_kernel(xn_ref, wi_ref, wf_ref, gk_ref, bf_ref, c64_ref, s64_ref, o_ref, ot_ref):
    d = xn_ref.shape[1]
    pad = jnp.zeros((LANES - wi_ref.shape[0] - wf_ref.shape[0], d), F32)
    wt = jnp.concatenate([wi_ref[...], wf_ref[...], pad], axis=0).astype(BF16)
    h = _dot_nt(xn_ref[...], wt)
    lane = _lane_iota(h.shape)
    is_ik = lane < SM_IW
    ms = jnp.sum(jnp.where(is_ik, h * h, 0.0), axis=-1, keepdims=True) * (1.0 / IDX_DIM)
    ik = _rope(h * lax.rsqrt(ms + EPS) * gk_ref[...], c64_ref[...], s64_ref[...], IDX_DIM)
    iw = h * (IDX_HEADS ** -0.5 * IDX_DIM ** -0.5)
    c = jax.nn.log_sigmoid(h + bf_ref[...])
    row = lax.broadcasted_iota(I32, h.shape, 0)
    d = 1
    while d < h.shape[0]:
        c = c + jnp.where(row >= d, pltpu.roll(c, d, 0), 0.0)
        d *= 2
    out = jnp.where(is_ik, ik, jnp.where(lane < SM_CF, iw, c))
    o_ref[...] = out
    ot_ref[0] = out.T


def _even_small(xn, wt, idx_row, fgt_row, gk, bf, c64, s64, batch, seq):
    d = xn.shape[1]
    vec = pl.BlockSpec((1, LANES), lambda b: (0, 0))
    tab = pl.BlockSpec((seq, LANES), lambda b: (0, 0))
    return pl.pallas_call(
        _even_small_kernel,
        grid=(batch,),
        in_specs=[pl.BlockSpec((seq, d), lambda b: (b, 0)),
                  pl.BlockSpec((pl.Element(IDX_DIM + IDX_HEADS), pl.Element(d)), lambda b: (idx_row, 0)),
                  pl.BlockSpec((pl.Element(FOX_HEADS), pl.Element(d)), lambda b: (fgt_row, 0)),
                  vec, vec, tab, tab],
        out_specs=[pl.BlockSpec((seq, LANES), lambda b: (b, 0)),
                   pl.BlockSpec((1, LANES, seq), lambda b: (b, 0, 0))],
        out_shape=[jax.ShapeDtypeStruct((batch * seq, LANES), F32),
                   jax.ShapeDtypeStruct((batch, LANES, seq), F32)],
        compiler_params=_params(("parallel",)),
        name="even_small",
    )(xn, wt, wt, gk, bf, c64, s64)


FOX_TQ = 256
FOX_HPS = 8
ONES_ROWS = 16


def _fox_kernel(q_ref, k_ref, v_ref, g_ref, small_ref, o_ref, ka_ref, vt_ref, *, nq):
    hg = pl.program_id(1)
    i = pl.program_id(2)
    tq = FOX_TQ
    seq = k_ref.shape[0]
    heads = [slice(u * HEAD_DIM, (u + 1) * HEAD_DIM) for u in range(FOX_HPS)]
    lane = _lane_iota((seq, LANES))

    @pl.when(i == 0)
    def _():
        sm = small_ref[...]
        for u, sl in enumerate(heads):
            vt_ref[u, 0:HEAD_DIM, :] = v_ref[:, sl].astype(F32).T.astype(BF16)
            vt_ref[u, HEAD_DIM:, :] = jnp.ones((ONES_ROWS, seq), BF16)
            ck = jnp.sum(jnp.where(lane == SM_CF + hg * FOX_HPS + u, sm, 0.0), axis=1, keepdims=True)
            neg = jnp.broadcast_to(ck * -LOG2E, (seq, LANES))
            hi = neg.astype(BF16).astype(F32)
            mid = (neg - hi).astype(BF16).astype(F32)
            lo = neg - hi - mid
            extra = jnp.where(lane == 0, hi, jnp.where(lane == 1, mid, jnp.where(lane == 2, lo, 0.0)))
            ka_ref[u, :, 0:HEAD_DIM] = k_ref[:, sl]
            ka_ref[u, :, HEAD_DIM:] = extra.astype(BF16)

    ones3 = jnp.where(_lane_iota((tq, LANES)) < 3, 1.0, 0.0).astype(BF16)
    qa = [jnp.concatenate([q_ref[:, sl], ones3], axis=1) for sl in heads]

    def qk(u, j):
        return _dot_nt(ka_ref[u, pl.ds(j * tq, tq), :], qa[u])

    def step(u, j, z, carry, diagonal):
        m, acc = carry
        if diagonal:
            key = lax.broadcasted_iota(I32, (tq, tq), 0)
            qry = lax.broadcasted_iota(I32, (tq, tq), 1)
            z = jnp.where(key <= qry, z, NEG)
        m_new = jnp.maximum(m, jnp.max(z, axis=0, keepdims=True))
        p = jnp.exp2(z - m_new)
        acc = jnp.exp2(m - m_new) * acc + _dot(vt_ref[u, :, pl.ds(j * tq, tq)], p.astype(BF16))
        return m_new, acc

    init = (jnp.full((1, tq), -jnp.inf, F32), jnp.zeros((HEAD_DIM + ONES_ROWS, tq), F32))
    for i_static in range(nq):
        @pl.when(i == i_static)
        def _(i_static=i_static):
            carry = [init] * FOX_HPS
            z_next = [qk(u, 0) for u in range(FOX_HPS)]
            for j in range(i_static + 1):
                for u in range(FOX_HPS):
                    z = z_next[u]
                    if j < i_static:
                        z_next[u] = qk(u, j + 1)
                    carry[u] = step(u, j, z, carry[u], j == i_static)
            for u, sl in enumerate(heads):
                _, acc = carry[u]
                out = (acc[0:HEAD_DIM] * (1.0 / acc[HEAD_DIM:HEAD_DIM + 1])).T
                o_ref[:, sl] = (out * g_ref[:, sl].astype(F32)).astype(o_ref.dtype)


def _fox_attention(hmain, small, batch, seq):
    tq = FOX_TQ
    nq = seq // tq
    width = FOX_HPS * HEAD_DIM
    ng = FOX_W // width
    return pl.pallas_call(
        functools.partial(_fox_kernel, nq=nq),
        grid=(batch, ng, nq),
        in_specs=[pl.BlockSpec((tq, width), lambda b, h, i: (b * nq + i, EV_FQ * ng + h)),
                  pl.BlockSpec((seq, width), lambda b, h, i: (b, EV_FK * ng + h)),
                  pl.BlockSpec((seq, width), lambda b, h, i: (b, EV_FV * ng + h)),
                  pl.BlockSpec((tq, width), lambda b, h, i: (b * nq + i, EV_FG * ng + h)),
                  pl.BlockSpec((seq, LANES), lambda b, h, i: (b, 0))],
        out_specs=pl.BlockSpec((tq, width), lambda b, h, i: (b * nq + i, h)),
        out_shape=jax.ShapeDtypeStruct((batch * seq, FOX_W), BF16),
        scratch_shapes=[pltpu.VMEM((FOX_HPS, seq, 2 * HEAD_DIM), BF16),
                        pltpu.VMEM((FOX_HPS, HEAD_DIM + ONES_ROWS, seq), BF16)],
        compiler_params=_params(("parallel", "parallel", "arbitrary")),
        name="fox_attention",
    )(hmain, hmain, hmain, hmain, small)


DSA_TQ = 256


def _key_to_f32(key):
    return pltpu.bitcast(key ^ ((key >> 31) & 0x7FFFFFFF), F32)


def _dsa_kernel(iq_ref, iqn_ref, small_ref, smallt_ref, q_ref, k_ref, v_ref, g_ref, o_ref,
                score_ref, vt_ref, lim_ref, m_ref, acc_ref, *, topk, nq):
    i = pl.program_id(1)
    tq = DSA_TQ
    nchunk = i + 1
    slot = i % 2
    key0 = lax.broadcasted_iota(I32, (tq, tq), 0)
    lane0 = lax.broadcasted_iota(I32, (tq, tq), 1)
    qry = i * tq + lane0

    @pl.when(i == 0)
    def _():
        for hh in range(DSA_HEADS):
            sl = slice(hh * HEAD_DIM, (hh + 1) * HEAD_DIM)
            vt_ref[hh, 0:HEAD_DIM, :] = v_ref[:, sl].astype(F32).T.astype(BF16)
            vt_ref[hh, HEAD_DIM:, :] = jnp.ones((ONES_ROWS, v_ref.shape[0]), BF16)

    def chunk_off(c):
        return pl.multiple_of(c * tq, tq)

    def score_chunk(c, blk, q_idx_ref, dst):
        off = chunk_off(c)
        qoff = pl.multiple_of(blk * tq, tq)
        ikc = small_ref[pl.ds(off, tq), SM_IK:SM_IK + IDX_DIM].astype(BF16)

        def logits(hh):
            return _dot_nt(ikc, q_idx_ref[:, hh * IDX_DIM:(hh + 1) * IDX_DIM])

        acc = jnp.zeros((tq, tq), F32)
        nxt = logits(0)
        for hh in range(IDX_HEADS):
            cur = nxt
            if hh + 1 < IDX_HEADS:
                nxt = logits(hh + 1)
            w = smallt_ref[0, SM_IW + hh:SM_IW + hh + 1, pl.ds(qoff, tq)]
            acc = acc + w * jnp.maximum(cur, 0.0)
        score_ref[dst, pl.ds(off, tq), :] = jnp.where(off + key0 <= qoff + lane0, acc, NEG)

    @pl.when(i == 0)
    def _():
        score_chunk(0, 0, iq_ref, 0)

    def count(pred):
        def body(c, tot):
            off = chunk_off(c)
            hit = jnp.where(pred(score_ref[slot, pl.ds(off, tq), :], off + key0), 1.0, 0.0)
            return tot + jnp.sum(hit.reshape(tq // 8, 8, tq), axis=0)
        tot = lax.fori_loop(0, nchunk, body, jnp.zeros((8, tq), F32))
        return jnp.sum(tot, axis=0, keepdims=True)

    def thr_step(it, res, nlive):
        cand = res + jnp.left_shift(jnp.int32(1), 31 - it)
        cand_f = _key_to_f32(cand)
        tot = jnp.zeros((8, tq), F32)
        for c in range(nlive):
            hit = jnp.where(score_ref[(nlive - 1) % 2, c * tq:(c + 1) * tq, :] >= cand_f, 1.0, 0.0)
            tot = tot + jnp.sum(hit.reshape(tq // 8, 8, tq), axis=0)
        cnt = jnp.sum(tot, axis=0, keepdims=True)
        return jnp.where(cnt >= topk, cand, res)

    for i_static in range(nq):
        @pl.when(i == i_static)
        def _(i_static=i_static):
            key = lax.fori_loop(0, 32, functools.partial(thr_step, nlive=i_static + 1),
                                jnp.full((1, tq), INT_MIN, I32))
            lim_ref[...] = key

    thr_key = lim_ref[...]
    thr = jnp.where(thr_key == INT_MIN, -jnp.inf, _key_to_f32(thr_key))
    n_gt = count(lambda sc, ki: sc > thr)
    n_ge = count(lambda sc, ki: sc >= thr)
    need = topk - n_gt

    lim_ref[...] = jnp.full((1, tq), 2 ** IDX_BITS, I32)

    @pl.when(jnp.max(n_ge) > topk)
    def _():
        def lim_step(it, res):
            cand = res + jnp.left_shift(jnp.int32(1), IDX_BITS - 1 - it)
            cnt = count(lambda sc, ki: (sc == thr) & (ki < cand))
            return jnp.where(cnt <= need, cand, res)
        lim_ref[...] = lax.fori_loop(0, IDX_BITS, lim_step, jnp.zeros((1, tq), I32))

    lim = lim_ref[...]

    m_ref[...] = jnp.full(m_ref.shape, -jnp.inf, F32)
    acc_ref[...] = jnp.zeros(acc_ref.shape, F32)

    def attn_chunk(c, _, ahead):
        off = chunk_off(c)
        sc = score_ref[slot, pl.ds(off, tq), :]
        ki = off + key0
        sel = ((sc > thr) | ((sc == thr) & (ki < lim))) & (ki <= qry)
        bias = jnp.where(sel, 0.0, NEG)

        def qk(hh):
            sl = slice(hh * HEAD_DIM, (hh + 1) * HEAD_DIM)
            return _dot_nt(k_ref[pl.ds(off, tq), sl], q_ref[:, sl])

        heads = range(DSA_HEADS)
        zs = [bias + qk(hh) for hh in heads]
        ms = [m_ref[hh:hh + 1, :] for hh in heads]
        m_news = [jnp.maximum(ms[hh], jnp.max(zs[hh], axis=0, keepdims=True)) for hh in heads]
        ps = [jnp.exp2(zs[hh] - m_news[hh]).astype(BF16) for hh in heads]
        pvs = [_dot(vt_ref[hh, :, pl.ds(off, tq)], ps[hh]) for hh in heads]
        for hh in heads:
            acc_ref[hh] = jnp.exp2(ms[hh] - m_news[hh]) * acc_ref[hh] + pvs[hh]
            m_ref[hh:hh + 1, :] = m_news[hh]
        if ahead:
            score_chunk(c, i + 1, iqn_ref, 1 - slot)
        return 0

    @pl.when(i + 1 < nq)
    def _():
        lax.fori_loop(0, nchunk, functools.partial(attn_chunk, ahead=True), 0)
        score_chunk(nchunk, i + 1, iqn_ref, 1 - slot)

    @pl.when(i + 1 == nq)
    def _():
        lax.fori_loop(0, nchunk, functools.partial(attn_chunk, ahead=False), 0)

    for hh in range(DSA_HEADS):
        sl = slice(hh * HEAD_DIM, (hh + 1) * HEAD_DIM)
        out = (acc_ref[hh, 0:HEAD_DIM, :] * (1.0 / acc_ref[hh, HEAD_DIM:HEAD_DIM + 1, :])).T
        o_ref[:, sl] = (out * g_ref[:, sl].astype(F32)).astype(o_ref.dtype)


def _dsa_attention(hmain, small, small_t, batch, seq, topk):
    tq = DSA_TQ
    nq = seq // tq
    wide = lambda t: pl.BlockSpec((tq, DSA_W), lambda b, i: (b * nq + i, t))
    full = lambda t: pl.BlockSpec((seq, DSA_W), lambda b, i: (b, t))
    return pl.pallas_call(
        functools.partial(_dsa_kernel, topk=topk, nq=nq),
        grid=(batch, nq),
        in_specs=[wide(EV_IQ),
                  pl.BlockSpec((tq, DSA_W), lambda b, i: (b * nq + jnp.minimum(i + 1, nq - 1), EV_IQ)),
                  pl.BlockSpec((seq, LANES), lambda b, i: (b, 0)),
                  pl.BlockSpec((1, LANES, seq), lambda b, i: (b, 0, 0)),
                  wide(EV_DQ), full(EV_DK), full(EV_DV), wide(EV_DG)],
        out_specs=pl.BlockSpec((tq, DSA_W), lambda b, i: (b * nq + i, 0)),
        out_shape=jax.ShapeDtypeStruct((batch * seq, DSA_W), BF16),
        scratch_shapes=[pltpu.VMEM((2, seq, tq), F32),
                        pltpu.VMEM((DSA_HEADS, HEAD_DIM + ONES_ROWS, seq), BF16),
                        pltpu.VMEM((1, tq), I32),
                        pltpu.VMEM((DSA_HEADS, tq), F32),
                        pltpu.VMEM((DSA_HEADS, HEAD_DIM + ONES_ROWS, tq), F32)],
        compiler_params=_params(("parallel", "arbitrary")),
        name="dsa_attention",
    )(hmain, hmain, small, small_t, hmain, hmain, hmain, hmain)


def _out_proj_kernel(*refs, ny, normed):
    y_refs, w_ref, x_ref = refs[:ny], refs[ny], refs[ny + 1]
    if normed:
        g_ref, o_ref, n_ref, w16_ref = refs[ny + 2:]
    else:
        o_ref, w16_ref = refs[ny + 2:]

    @pl.when(pl.program_id(0) == 0)
    def _():
        w16_ref[...] = w_ref[0].astype(BF16)

    acc = x_ref[...]
    off = 0
    for y_ref in y_refs:
        kdim = y_ref.shape[1]
        acc = acc + _dot(y_ref[...], w16_ref[off:off + kdim, :])
        off += kdim
    o_ref[...] = acc
    if normed:
        y = acc * lax.rsqrt(jnp.mean(acc * acc, axis=-1, keepdims=True) + EPS)
        n_ref[...] = (y * g_ref[...]).astype(n_ref.dtype)


def _out_proj(ys, w_stack, layer, x2, next_gain=None, tm=512):
    m, d = x2.shape
    kdim = w_stack.shape[1]
    row = pl.BlockSpec((tm, d), lambda i: (i, 0))
    normed = next_gain is not None
    extra = ([next_gain.reshape(1, d)], [pl.BlockSpec((1, d), lambda i: (0, 0))]) if normed else ([], [])
    out = pl.pallas_call(
        functools.partial(_out_proj_kernel, ny=len(ys), normed=normed),
        grid=(m // tm,),
        in_specs=[pl.BlockSpec((tm, y.shape[1]), lambda i: (i, 0)) for y in ys]
                 + [pl.BlockSpec((1, kdim, d), lambda i: (layer, 0, 0), pipeline_mode=pl.Buffered(1)), row] + extra[1],
        out_specs=[row, row] if normed else [row],
        out_shape=[jax.ShapeDtypeStruct((m, d), F32)] + ([jax.ShapeDtypeStruct((m, d), BF16)] if normed else []),
        scratch_shapes=[pltpu.VMEM((kdim, d), BF16)],
        compiler_params=_params(("arbitrary",)),
        name="out_proj",
    )(*ys, w_stack, x2, *extra[0])
    return (out[0], out[1]) if normed else (out[0], None)


OD_TN = 512
OD_NQ = SWA_W // OD_TN


def _odd_kinds(j, aux, nslab):
    gain_ref, c64_ref, s64_ref = aux
    qk = lambda idx: (lambda a: _rope(_half_rms(a, gain_ref[idx]), c64_ref[...], s64_ref[...], SWA_HEAD_DIM))
    nk = SWA_KV_W // LANES
    return [
        (j < OD_NQ, lambda: [(qk(0), 0, nslab)]),
        (j == OD_NQ, lambda: [(qk(1), 0, nk), (lambda a: a, nk, nslab)]),
        (j > OD_NQ, lambda: [(lambda a: a * jax.nn.sigmoid(a), 0, nslab)]),
    ]


def _odd_proj(xn, w, layer, gains, c64, s64, seq):
    gspec = pl.BlockSpec((2, 1, LANES), lambda t: (0, 0, 0))
    return _skewed_proj(xn, w, layer, False, lambda j: j, (gains, c64, s64), [gspec, None, None],
                        _odd_kinds, seq, OD_TN, w.shape[2] // OD_TN, "odd_proj")


def _swa_kernel(sink_ref, q_ref, kvp_ref, kvc_ref, *rest):
    *g_refs, o_ref = rest
    n = pl.program_id(1)
    w = SWA_WINDOW
    dh = SWA_HEAD_DIM
    group = SWA_HEADS // SWA_KV_HEADS
    gw = group * w
    kb = lax.broadcasted_iota(I32, (2 * w, gw), 0)
    qi = lax.broadcasted_iota(I32, (2 * w, gw), 1) & (w - 1)
    valid = (kb > qi) & (kb <= qi + w) & ((n > 0) | (kb >= w))
    bias = jnp.where(valid, 0.0, NEG)
    head_of_lane = lax.broadcasted_iota(I32, (1, gw), 1) // w
    ones = jnp.ones((ONES_ROWS, 2 * w), BF16)

    def band(col):
        sl = slice(col, col + dh)
        return jnp.concatenate([kvp_ref[:, sl], kvc_ref[:, sl]], axis=0)

    for kv in range(SWA_KV_HEADS):
        h0 = kv * group
        qg = jnp.concatenate([q_ref[:, (h0 + t) * dh:(h0 + t + 1) * dh] for t in range(group)], axis=0)
        z = bias + _dot_nt(band(kv * dh), qg)
        sink = jnp.zeros((1, gw), F32)
        for t in range(group):
            sink = jnp.where(head_of_lane == t, sink_ref[h0 + t] * LOG2E, sink)
        m = jnp.maximum(jnp.max(z, axis=0, keepdims=True), sink)
        vt = jnp.concatenate([band(SWA_KV_W + kv * dh).astype(F32).T.astype(BF16), ones], axis=0)
        pv = _dot(vt, jnp.exp2(z - m).astype(BF16))
        den = pv[dh:dh + 1] + jnp.exp2(sink - m)
        ot = (pv[0:dh] * (1.0 / den)).T
        o = jnp.concatenate([ot[t * w:(t + 1) * w] for t in range(group)], axis=1)
        sl = slice(h0 * dh, (h0 + group) * dh)
        o_ref[:, sl] = (o * g_refs[kv][...].astype(F32)).astype(o_ref.dtype)


def _swa_attention(hodd, sinks, batch, seq):
    w = SWA_WINDOW
    nb = seq // w
    kvw = 2 * SWA_KV_W
    kv_col = SWA_W // kvw
    gate = lambda kv: pl.BlockSpec((w, kvw), lambda b, n: (b * nb + n, kv_col + 1 + kv))
    return pl.pallas_call(
        _swa_kernel,
        grid=(batch, nb),
        in_specs=[pl.BlockSpec(memory_space=pltpu.SMEM),
                  pl.BlockSpec((w, SWA_W), lambda b, n: (b * nb + n, 0)),
                  pl.BlockSpec((w, kvw), lambda b, n: (b * nb + jnp.maximum(n - 1, 0), kv_col)),
                  pl.BlockSpec((w, kvw), lambda b, n: (b * nb + n, kv_col))]
                 + [gate(kv) for kv in range(SWA_KV_HEADS)],
        out_specs=pl.BlockSpec((w, SWA_W), lambda b, n: (b * nb + n, 0)),
        out_shape=jax.ShapeDtypeStruct((batch * seq, SWA_W), BF16),
        compiler_params=_params(("parallel", "arbitrary")),
        name="swa_attention",
    )(sinks, hodd, hodd, hodd, *([hodd] * SWA_KV_HEADS))


def _rope_tables(seq, dim):
    inv = 1.0 / (ROPE_THETA ** (jnp.arange(0, dim, 2, dtype=F32) / dim))
    ang = jnp.arange(seq, dtype=F32)[:, None] * inv[None, :]
    cos, sin = jnp.cos(ang), jnp.sin(ang)
    reps = LANES // dim
    return (jnp.tile(jnp.concatenate([cos, cos], -1), (1, reps)),
            jnp.tile(jnp.concatenate([-sin, sin], -1), (1, reps)))


def _cols(w, start, size):
    return lax.slice_in_dim(w, start, start + size, axis=1)


def _even_layer(x2, xn, next_gain, batch, seq, w_in_stack, j, b_f, g_fox, g_dsa, g_kidx, w_out_stack, tabs, topk):
    c128, s128, c64, s64 = tabs
    ncol = w_in_stack.shape[2]
    row0 = j * ncol
    fgt_row = row0 + 4 * FOX_W
    idx_row = fgt_row + FOX_HEADS + 4 * DSA_W + IDX_W
    pad = LANES - (IDX_DIM + IDX_HEADS + FOX_HEADS)
    wt = jnp.swapaxes(w_in_stack, 1, 2).reshape(-1, w_in_stack.shape[1])
    qscale = HEAD_DIM ** -0.5 * LOG2E
    one = jnp.ones((LANES,), F32)
    gains = jnp.stack([g_fox[0] * qscale, g_fox[1], one, one,
                       g_dsa[0] * qscale, g_dsa[1], one, one, one]).reshape(EV_IQ + 1, 1, LANES)
    gk = jnp.concatenate([g_kidx, jnp.zeros((LANES - IDX_DIM,), F32)]).reshape(1, LANES)
    bf = jnp.concatenate([jnp.zeros((SM_CF,), F32), b_f, jnp.zeros((pad,), F32)]).reshape(1, LANES)

    hmain = _even_proj(xn, wt, row0, gains, c128, s128, c64, s64, seq)
    small, small_t = _even_small(xn, wt, idx_row, fgt_row, gk, bf, c64, s64, batch, seq)
    ya = _fox_attention(hmain, small, batch, seq)
    yb = _dsa_attention(hmain, small, small_t, batch, seq, topk)
    return _out_proj([ya, yb], w_out_stack, j, x2, next_gain)


def _odd_layer(x2, xn, next_gain, batch, seq, w_in_stack, j, g_qk, sinks, w_out_stack, tabs):
    _, _, c64, s64 = tabs
    qscale = jnp.array([[SWA_HEAD_DIM ** -0.5 * LOG2E], [1.0]], F32)
    gains = jnp.tile(g_qk * qscale, (1, LANES // SWA_HEAD_DIM)).reshape(2, 1, LANES)
    hodd = _odd_proj(xn, w_in_stack, j, gains, c64, s64, seq)
    y = _swa_attention(hodd, sinks, batch, seq)
    return _out_proj([y], w_out_stack, j, x2, next_gain)


def kernel(x, norm_even, w_in_even, b_f_even, g_qk_fox, g_qk_dsa, g_kidx, w_out_even,
           norm_odd, w_in_odd, g_qk_swa, sinks, w_out_odd):
    batch, seq, d = x.shape
    depth = norm_even.shape[0] + norm_odd.shape[0]
    topk = min(IDX_TOPK_MAX, seq // 4)
    tabs = _rope_tables(seq, HEAD_DIM) + _rope_tables(seq, IDX_DIM)
    x2 = x.reshape(batch * seq, d)
    pre_gain = lambda layer: (norm_even, norm_odd)[layer % 2][layer // 2]
    xn = _rmsnorm(x2, pre_gain(0))
    for layer in range(depth):
        j = layer // 2
        next_gain = pre_gain(layer + 1) if layer + 1 < depth else None
        if layer % 2 == 0:
            x2, xn = _even_layer(x2, xn, next_gain, batch, seq, w_in_even, j, b_f_even[j], g_qk_fox[j],
                                 g_qk_dsa[j], g_kidx[j], w_out_even, tabs, topk)
        else:
            x2, xn = _odd_layer(x2, xn, next_gain, batch, seq, w_in_odd, j, g_qk_swa[j], sinks[j],
                                w_out_odd, tabs)
    return x2.reshape(batch, seq, d)
```

```python
import functools

import jax
import jax.numpy as jnp
from jax import lax
from jax.experimental import pallas as pl
from jax.experimental.pallas import tpu as pltpu

F32 = jnp.float32
BF16 = jnp.bfloat16
I32 = jnp.int32

D_MODEL = 2048
HEAD_DIM = 128
FOX_HEADS = 8
DSA_HEADS = 8
IDX_HEADS = 16
IDX_DIM = 64
IDX_TOPK_MAX = 256
SWA_HEADS = 32
SWA_KV_HEADS = 4
SWA_HEAD_DIM = 64
SWA_WINDOW = 128
ROPE_THETA = 10000.0
EPS = 1e-6
NEG = -1e30

FOX_W = FOX_HEADS * HEAD_DIM
DSA_W = DSA_HEADS * HEAD_DIM
IDX_W = IDX_HEADS * IDX_DIM
SWA_W = SWA_HEADS * SWA_HEAD_DIM
SWA_KV_W = SWA_KV_HEADS * SWA_HEAD_DIM

LANES = 128
VMEM_LIMIT = 56 * 2 ** 20

SM_IK = 0
SM_IW = IDX_DIM
SM_CF = IDX_DIM + IDX_HEADS

LOG2E = 1.4426950408889634
INT_MIN = -2 ** 31
IDX_BITS = 12


def _params(sem):
    return pltpu.CompilerParams(dimension_semantics=sem, vmem_limit_bytes=VMEM_LIMIT)


def _dot(a, b):
    return jnp.dot(a, b, preferred_element_type=F32)


def _dot_nt(a, b):
    return lax.dot_general(a, b, (((1,), (1,)), ((), ())), preferred_element_type=F32)


def _lane_iota(shape):
    return lax.broadcasted_iota(I32, shape, 1)


def _rope(y, cos, sin, dim):
    if dim == LANES:
        rot = pltpu.roll(y, LANES // 2, 1)
    else:
        half = dim // 2
        first = (_lane_iota(y.shape) & half) == 0
        rot = jnp.where(first, pltpu.roll(y, LANES - half, 1), pltpu.roll(y, half, 1))
    return y * cos + rot * sin


def _head_rms(a, gain):
    return a * lax.rsqrt(jnp.mean(a * a, axis=-1, keepdims=True) + EPS) * gain


def _half_rms(a, gain):
    lo = _lane_iota(a.shape) < 64
    sq = a * a
    ms_lo = jnp.sum(jnp.where(lo, sq, 0.0), axis=-1, keepdims=True) * (1.0 / 64)
    ms_hi = jnp.sum(jnp.where(lo, 0.0, sq), axis=-1, keepdims=True) * (1.0 / 64)
    return a * lax.rsqrt(jnp.where(lo, ms_lo, ms_hi) + EPS) * gain


def _skewed_proj_kernel(xn_ref, w_ref, *rest, kinds, transposed, nj, tn):
    *aux, o_ref, acc_ref = rest
    t = pl.program_id(0)
    last = pl.num_programs(0) - 1
    j = (t + nj - 1) % nj

    def finish(parts):
        for fn, lo, hi in parts:
            for h in range(lo, hi):
                sl = slice(h * LANES, (h + 1) * LANES)
                o_ref[:, sl] = fn(acc_ref[:, sl]).astype(o_ref.dtype)

    def multiply():
        w = (w_ref[...] if transposed else w_ref[0]).astype(BF16)
        acc_ref[...] = _dot_nt(xn_ref[...], w) if transposed else _dot(xn_ref[...], w)

    @pl.when(t == 0)
    def _():
        multiply()

    nslab = tn // LANES
    for cond, parts in kinds(j, aux, nslab):
        @pl.when((t > 0) & (t < last) & cond)
        def _(parts=parts):
            finish(parts())
            multiply()

    for cond, parts in kinds(nj - 1, aux, nslab):
        if cond:
            @pl.when(t == last)
            def _(parts=parts):
                finish(parts())


def _skewed_proj(xn, w_stack, layer, transposed, locate, aux, aux_specs, kinds, seq, tn, nj, name, tm=1024):
    m, d = xn.shape
    ntiles = (m // tm) * nj
    nrow = seq // tm
    cur = lambda t: jnp.minimum(t, ntiles - 1)
    prev = lambda t: jnp.maximum(t - 1, 0)
    tab = pl.BlockSpec((tm, LANES), lambda t: ((prev(t) // nj) % nrow, 0))
    specs = [tab if s is None else s for s in aux_specs]
    if transposed:
        wspec = pl.BlockSpec((pl.Element(tn), pl.Element(d)), lambda t: (locate(cur(t) % nj), 0))
    else:
        wspec = pl.BlockSpec((1, d, tn), lambda t: (layer, 0, locate(cur(t) % nj)))
    return pl.pallas_call(
        functools.partial(_skewed_proj_kernel, kinds=kinds, transposed=transposed, nj=nj, tn=tn),
        grid=(ntiles + 1,),
        in_specs=[pl.BlockSpec((tm, d), lambda t: (cur(t) // nj, 0)), wspec] + specs,
        out_specs=pl.BlockSpec((tm, tn), lambda t: (prev(t) // nj, prev(t) % nj)),
        out_shape=jax.ShapeDtypeStruct((m, nj * tn), BF16),
        scratch_shapes=[pltpu.VMEM((tm, tn), F32)],
        compiler_params=_params(("arbitrary",)),
        name=name,
    )(xn, w_stack, *aux)


EV_TN = 1024
EV_FQ, EV_FK, EV_FV, EV_FG, EV_DQ, EV_DK, EV_DV, EV_DG, EV_IQ = range(9)
EV_NFOX = 4


def _even_kinds(j, aux, nslab):
    gain_ref, c128_ref, s128_ref, c64_ref, s64_ref = aux
    gain = lambda: gain_ref[jnp.minimum(j, EV_IQ)]
    norm = lambda: [(lambda a: _head_rms(a, gain()), 0, nslab)]
    norm_rope = lambda: [(lambda a: _rope(_head_rms(a, gain()), c128_ref[...], s128_ref[...], HEAD_DIM), 0, nslab)]
    cast = lambda: [(lambda a: a, 0, nslab)]
    silu = lambda: [(lambda a: a * jax.nn.sigmoid(a), 0, nslab)]
    rope64 = lambda: [(lambda a: _rope(a, c64_ref[...], s64_ref[...], IDX_DIM), 0, nslab)]
    return [
        ((j == EV_FQ) | (j == EV_FK), norm),
        ((j == EV_DQ) | (j == EV_DK), norm_rope),
        ((j == EV_FV) | (j == EV_DV), cast),
        ((j == EV_FG) | (j == EV_DG), silu),
        (j == EV_IQ, rope64),
    ]


def _even_proj(xn, wt, row0, gains, c128, s128, c64, s64, seq):
    gspec = pl.BlockSpec(gains.shape, lambda t: (0, 0, 0))
    row = lambda j: (row0 // 8 + j * (EV_TN // 8) + (j + EV_NFOX) // (2 * EV_NFOX)) * 8
    return _skewed_proj(xn, wt, None, True, row, (gains, c128, s128, c64, s64),
                        [gspec, None, None, None, None], _even_kinds, seq, EV_TN, EV_IQ + 1, "even_proj")


ES_TM = 512


def _even_small_kernel(*refs, normalize):
    if normalize:
        x_ref, g_ref, wi_ref, wf_ref, gk_ref, bf_ref, c64_ref, s64_ref, o_ref, ot_ref, xn_ref, carry_ref = refs
        x = x_ref[...]
        xn = (x * lax.rsqrt(jnp.mean(x * x, axis=-1, keepdims=True) + EPS) * g_ref[...]).astype(BF16)
        xn_ref[...] = xn
    else:
        xn_ref, wi_ref, wf_ref, gk_ref, bf_ref, c64_ref, s64_ref, o_ref, ot_ref, carry_ref = refs
        xn = xn_ref[...]
    pad = jnp.zeros((LANES - wi_ref.shape[0] - wf_ref.shape[0], xn.shape[1]), F32)
    wt = jnp.concatenate([wi_ref[...], wf_ref[...], pad], axis=0).astype(BF16)
    h = _dot_nt(xn, wt)
    lane = _lane_iota(h.shape)
    is_ik = lane < SM_IW
    ms = jnp.sum(jnp.where(is_ik, h * h, 0.0), axis=-1, keepdims=True) * (1.0 / IDX_DIM)
    ik = _rope(h * lax.rsqrt(ms + EPS) * gk_ref[...], c64_ref[...], s64_ref[...], IDX_DIM)
    iw = h * (IDX_HEADS ** -0.5 * IDX_DIM ** -0.5)
    c = jax.nn.log_sigmoid(h + bf_ref[...])
    row = lax.broadcasted_iota(I32, h.shape, 0)
    step = 1
    while step < h.shape[0]:
        c = c + jnp.where(row >= step, pltpu.roll(c, step, 0), 0.0)
        step *= 2

    @pl.when(pl.program_id(1) == 0)
    def _():
        carry_ref[...] = jnp.zeros(carry_ref.shape, F32)

    c = c + carry_ref[...]
    carry_ref[...] = c[h.shape[0] - 1:, :]
    out = jnp.where(is_ik, ik, jnp.where(lane < SM_CF, iw, c))
    o_ref[...] = out
    ot_ref[0] = out.T


def _even_small(x_or_xn, norm_gain, wt, idx_row, fgt_row, gk, bf, c64, s64, batch, seq):
    m, d = x_or_xn.shape
    tm = ES_TM
    nr = seq // tm
    normalize = norm_gain is not None
    vec = pl.BlockSpec((1, LANES), lambda b, r: (0, 0))
    tab = pl.BlockSpec((tm, LANES), lambda b, r: (r, 0))
    rows = pl.BlockSpec((tm, d), lambda b, r: (b * nr + r, 0))
    head = ([x_or_xn, norm_gain.reshape(1, d)], [rows, pl.BlockSpec((1, d), lambda b, r: (0, 0))]) if normalize \
        else ([x_or_xn], [rows])
    out = pl.pallas_call(
        functools.partial(_even_small_kernel, normalize=normalize),
        grid=(batch, nr),
        in_specs=head[1] + [pl.BlockSpec((pl.Element(IDX_DIM + IDX_HEADS), pl.Element(d)), lambda b, r: (idx_row, 0)),
                            pl.BlockSpec((pl.Element(FOX_HEADS), pl.Element(d)), lambda b, r: (fgt_row, 0)),
                            vec, vec, tab, tab],
        out_specs=[pl.BlockSpec((tm, LANES), lambda b, r: (b * nr + r, 0)),
                   pl.BlockSpec((1, LANES, tm), lambda b, r: (b, 0, r))] + ([rows] if normalize else []),
        out_shape=[jax.ShapeDtypeStruct((m, LANES), F32),
                   jax.ShapeDtypeStruct((batch, LANES, seq), F32)]
                  + ([jax.ShapeDtypeStruct((m, d), BF16)] if normalize else []),
        scratch_shapes=[pltpu.VMEM((1, LANES), F32)],
        compiler_params=_params(("parallel", "arbitrary")),
        name="even_small",
    )(*head[0], wt, wt, gk, bf, c64, s64)
    return (out[0], out[1], out[2]) if normalize else (out[0], out[1], x_or_xn)


FOX_TQ = 256
FOX_HPS = 8
ONES_ROWS = 16


def _fox_kernel(q_ref, k_ref, v_ref, g_ref, small_ref, o_ref, ka_ref, vt_ref, *, nq):
    hg = pl.program_id(1)
    i = pl.program_id(2)
    tq = FOX_TQ
    seq = k_ref.shape[0]
    heads = [slice(u * HEAD_DIM, (u + 1) * HEAD_DIM) for u in range(FOX_HPS)]
    lane = _lane_iota((seq, LANES))

    @pl.when(i == 0)
    def _():
        sm = small_ref[...]
        for u, sl in enumerate(heads):
            vt_ref[u, 0:HEAD_DIM, :] = v_ref[:, sl].astype(F32).T.astype(BF16)
            vt_ref[u, HEAD_DIM:, :] = jnp.ones((ONES_ROWS, seq), BF16)
            ck = jnp.sum(jnp.where(lane == SM_CF + hg * FOX_HPS + u, sm, 0.0), axis=1, keepdims=True)
            neg = jnp.broadcast_to(ck * -LOG2E, (seq, LANES))
            hi = neg.astype(BF16).astype(F32)
            mid = (neg - hi).astype(BF16).astype(F32)
            lo = neg - hi - mid
            extra = jnp.where(lane == 0, hi, jnp.where(lane == 1, mid, jnp.where(lane == 2, lo, 0.0)))
            ka_ref[u, :, 0:HEAD_DIM] = k_ref[:, sl]
            ka_ref[u, :, HEAD_DIM:] = extra.astype(BF16)

    ones3 = jnp.where(_lane_iota((tq, LANES)) < 3, 1.0, 0.0).astype(BF16)
    qa = [jnp.concatenate([q_ref[:, sl], ones3], axis=1) for sl in heads]

    def qk(u, j):
        return _dot_nt(ka_ref[u, pl.ds(j * tq, tq), :], qa[u])

    def step(u, j, z, carry, diagonal):
        m, acc = carry
        if diagonal:
            key = lax.broadcasted_iota(I32, (tq, tq), 0)
            qry = lax.broadcasted_iota(I32, (tq, tq), 1)
            z = jnp.where(key <= qry, z, NEG)
        m_new = jnp.maximum(m, jnp.max(z, axis=0, keepdims=True))
        p = jnp.exp2(z - m_new)
        acc = jnp.exp2(m - m_new) * acc + _dot(vt_ref[u, :, pl.ds(j * tq, tq)], p.astype(BF16))
        return m_new, acc

    init = (jnp.full((1, tq), -jnp.inf, F32), jnp.zeros((HEAD_DIM + ONES_ROWS, tq), F32))
    for i_static in range(nq):
        @pl.when(i == i_static)
        def _(i_static=i_static):
            carry = [init] * FOX_HPS
            z_next = [qk(u, 0) for u in range(FOX_HPS)]
            for j in range(i_static + 1):
                for u in range(FOX_HPS):
                    z = z_next[u]
                    if j < i_static:
                        z_next[u] = qk(u, j + 1)
                    carry[u] = step(u, j, z, carry[u], j == i_static)
            for u, sl in enumerate(heads):
                _, acc = carry[u]
                out = (acc[0:HEAD_DIM] * (1.0 / acc[HEAD_DIM:HEAD_DIM + 1])).T
                o_ref[:, sl] = (out * g_ref[:, sl].astype(F32)).astype(o_ref.dtype)


def _fox_attention(hmain, small, batch, seq):
    tq = FOX_TQ
    nq = seq // tq
    width = FOX_HPS * HEAD_DIM
    ng = FOX_W // width
    return pl.pallas_call(
        functools.partial(_fox_kernel, nq=nq),
        grid=(batch, ng, nq),
        in_specs=[pl.BlockSpec((tq, width), lambda b, h, i: (b * nq + i, EV_FQ * ng + h)),
                  pl.BlockSpec((seq, width), lambda b, h, i: (b, EV_FK * ng + h)),
                  pl.BlockSpec((seq, width), lambda b, h, i: (b, EV_FV * ng + h)),
                  pl.BlockSpec((tq, width), lambda b, h, i: (b * nq + i, EV_FG * ng + h)),
                  pl.BlockSpec((seq, LANES), lambda b, h, i: (b, 0))],
        out_specs=pl.BlockSpec((tq, width), lambda b, h, i: (b * nq + i, h)),
        out_shape=jax.ShapeDtypeStruct((batch * seq, FOX_W), BF16),
        scratch_shapes=[pltpu.VMEM((FOX_HPS, seq, 2 * HEAD_DIM), BF16),
                        pltpu.VMEM((FOX_HPS, HEAD_DIM + ONES_ROWS, seq), BF16)],
        compiler_params=_params(("parallel", "parallel", "arbitrary")),
        name="fox_attention",
    )(hmain, hmain, hmain, hmain, small)


DSA_TQ = 256


def _key_to_f32(key):
    return pltpu.bitcast(key ^ ((key >> 31) & 0x7FFFFFFF), F32)


def _dsa_kernel(iq_ref, iqn_ref, small_ref, smallt_ref, q_ref, k_ref, v_ref, g_ref, o_ref,
                score_ref, vt_ref, lim_ref, m_ref, acc_ref, *, topk, nq):
    i = pl.program_id(1)
    tq = DSA_TQ
    nchunk = i + 1
    slot = i % 2
    key0 = lax.broadcasted_iota(I32, (tq, tq), 0)
    lane0 = lax.broadcasted_iota(I32, (tq, tq), 1)
    qry = i * tq + lane0

    @pl.when(i == 0)
    def _():
        for hh in range(DSA_HEADS):
            sl = slice(hh * HEAD_DIM, (hh + 1) * HEAD_DIM)
            vt_ref[hh, 0:HEAD_DIM, :] = v_ref[:, sl].astype(F32).T.astype(BF16)
            vt_ref[hh, HEAD_DIM:, :] = jnp.ones((ONES_ROWS, v_ref.shape[0]), BF16)

    def chunk_off(c):
        return pl.multiple_of(c * tq, tq)

    def score_chunk(c, blk, q_idx_ref, dst):
        off = chunk_off(c)
        qoff = pl.multiple_of(blk * tq, tq)
        ikc = small_ref[pl.ds(off, tq), SM_IK:SM_IK + IDX_DIM].astype(BF16)

        def logits(hh):
            return _dot_nt(ikc, q_idx_ref[:, hh * IDX_DIM:(hh + 1) * IDX_DIM])

        acc = jnp.zeros((tq, tq), F32)
        nxt = logits(0)
        for hh in range(IDX_HEADS):
            cur = nxt
            if hh + 1 < IDX_HEADS:
                nxt = logits(hh + 1)
            w = smallt_ref[0, SM_IW + hh:SM_IW + hh + 1, pl.ds(qoff, tq)]
            acc = acc + w * jnp.maximum(cur, 0.0)
        score_ref[dst, pl.ds(off, tq), :] = jnp.where(off + key0 <= qoff + lane0, acc, NEG)

    @pl.when(i == 0)
    def _():
        score_chunk(0, 0, iq_ref, 0)

    def count(pred):
        def body(c, tot):
            off = chunk_off(c)
            hit = jnp.where(pred(score_ref[slot, pl.ds(off, tq), :], off + key0), 1.0, 0.0)
            return tot + jnp.sum(hit.reshape(tq // 8, 8, tq), axis=0)
        tot = lax.fori_loop(0, nchunk, body, jnp.zeros((8, tq), F32))
        return jnp.sum(tot, axis=0, keepdims=True)

    def thr_step(it, res, nlive):
        cand = res + jnp.left_shift(jnp.int32(1), 31 - it)
        cand_f = _key_to_f32(cand)
        tot = jnp.zeros((8, tq), F32)
        for c in range(nlive):
            hit = jnp.where(score_ref[(nlive - 1) % 2, c * tq:(c + 1) * tq, :] >= cand_f, 1.0, 0.0)
            tot = tot + jnp.sum(hit.reshape(tq // 8, 8, tq), axis=0)
        cnt = jnp.sum(tot, axis=0, keepdims=True)
        return jnp.where(cnt >= topk, cand, res)

    for i_static in range(nq):
        @pl.when(i == i_static)
        def _(i_static=i_static):
            key = lax.fori_loop(0, 32, functools.partial(thr_step, nlive=i_static + 1),
                                jnp.full((1, tq), INT_MIN, I32))
            lim_ref[...] = key

    thr_key = lim_ref[...]
    thr = jnp.where(thr_key == INT_MIN, -jnp.inf, _key_to_f32(thr_key))
    n_gt = count(lambda sc, ki: sc > thr)
    n_ge = count(lambda sc, ki: sc >= thr)
    need = topk - n_gt

    lim_ref[...] = jnp.full((1, tq), 2 ** IDX_BITS, I32)

    @pl.when(jnp.max(n_ge) > topk)
    def _():
        def lim_step(it, res):
            cand = res + jnp.left_shift(jnp.int32(1), IDX_BITS - 1 - it)
            cnt = count(lambda sc, ki: (sc == thr) & (ki < cand))
            return jnp.where(cnt <= need, cand, res)
        lim_ref[...] = lax.fori_loop(0, IDX_BITS, lim_step, jnp.zeros((1, tq), I32))

    lim = lim_ref[...]

    m_ref[...] = jnp.full(m_ref.shape, -jnp.inf, F32)
    acc_ref[...] = jnp.zeros(acc_ref.shape, F32)

    def attn_chunk(c, _, ahead):
        off = chunk_off(c)
        sc = score_ref[slot, pl.ds(off, tq), :]
        ki = off + key0
        sel = ((sc > thr) | ((sc == thr) & (ki < lim))) & (ki <= qry)
        bias = jnp.where(sel, 0.0, NEG)

        def qk(hh):
            sl = slice(hh * HEAD_DIM, (hh + 1) * HEAD_DIM)
            return _dot_nt(k_ref[pl.ds(off, tq), sl], q_ref[:, sl])

        heads = range(DSA_HEADS)
        zs = [bias + qk(hh) for hh in heads]
        ms = [m_ref[hh:hh + 1, :] for hh in heads]
        m_news = [jnp.maximum(ms[hh], jnp.max(zs[hh], axis=0, keepdims=True)) for hh in heads]
        ps = [jnp.exp2(zs[hh] - m_news[hh]).astype(BF16) for hh in heads]
        pvs = [_dot(vt_ref[hh, :, pl.ds(off, tq)], ps[hh]) for hh in heads]
        for hh in heads:
            acc_ref[hh] = jnp.exp2(ms[hh] - m_news[hh]) * acc_ref[hh] + pvs[hh]
            m_ref[hh:hh + 1, :] = m_news[hh]
        if ahead:
            score_chunk(c, i + 1, iqn_ref, 1 - slot)
        return 0

    @pl.when(i + 1 < nq)
    def _():
        lax.fori_loop(0, nchunk, functools.partial(attn_chunk, ahead=True), 0)
        score_chunk(nchunk, i + 1, iqn_ref, 1 - slot)

    @pl.when(i + 1 == nq)
    def _():
        lax.fori_loop(0, nchunk, functools.partial(attn_chunk, ahead=False), 0)

    for hh in range(DSA_HEADS):
        sl = slice(hh * HEAD_DIM, (hh + 1) * HEAD_DIM)
        out = (acc_ref[hh, 0:HEAD_DIM, :] * (1.0 / acc_ref[hh, HEAD_DIM:HEAD_DIM + 1, :])).T
        o_ref[:, sl] = (out * g_ref[:, sl].astype(F32)).astype(o_ref.dtype)


def _dsa_attention(hmain, small, small_t, batch, seq, topk):
    tq = DSA_TQ
    nq = seq // tq
    wide = lambda t: pl.BlockSpec((tq, DSA_W), lambda b, i: (b * nq + i, t))
    full = lambda t: pl.BlockSpec((seq, DSA_W), lambda b, i: (b, t))
    return pl.pallas_call(
        functools.partial(_dsa_kernel, topk=topk, nq=nq),
        grid=(batch, nq),
        in_specs=[wide(EV_IQ),
                  pl.BlockSpec((tq, DSA_W), lambda b, i: (b * nq + jnp.minimum(i + 1, nq - 1), EV_IQ)),
                  pl.BlockSpec((seq, LANES), lambda b, i: (b, 0)),
                  pl.BlockSpec((1, LANES, seq), lambda b, i: (b, 0, 0)),
                  wide(EV_DQ), full(EV_DK), full(EV_DV), wide(EV_DG)],
        out_specs=pl.BlockSpec((tq, DSA_W), lambda b, i: (b * nq + i, 0)),
        out_shape=jax.ShapeDtypeStruct((batch * seq, DSA_W), BF16),
        scratch_shapes=[pltpu.VMEM((2, seq, tq), F32),
                        pltpu.VMEM((DSA_HEADS, HEAD_DIM + ONES_ROWS, seq), BF16),
                        pltpu.VMEM((1, tq), I32),
                        pltpu.VMEM((DSA_HEADS, tq), F32),
                        pltpu.VMEM((DSA_HEADS, HEAD_DIM + ONES_ROWS, tq), F32)],
        compiler_params=_params(("parallel", "arbitrary")),
        name="dsa_attention",
    )(hmain, hmain, small, small_t, hmain, hmain, hmain, hmain)


def _out_proj_kernel(*refs, ny, normed):
    y_refs, w_ref, x_ref = refs[:ny], refs[ny], refs[ny + 1]
    if normed:
        g_ref, o_ref, n_ref, w16_ref = refs[ny + 2:]
    else:
        o_ref, w16_ref = refs[ny + 2:]

    @pl.when(pl.program_id(0) == 0)
    def _():
        w16_ref[...] = w_ref[0].astype(BF16)

    acc = x_ref[...]
    off = 0
    for y_ref in y_refs:
        kdim = y_ref.shape[1]
        acc = acc + _dot(y_ref[...], w16_ref[off:off + kdim, :])
        off += kdim
    o_ref[...] = acc
    if normed:
        y = acc * lax.rsqrt(jnp.mean(acc * acc, axis=-1, keepdims=True) + EPS)
        n_ref[...] = (y * g_ref[...]).astype(n_ref.dtype)


def _out_proj(ys, w_stack, layer, x2, next_gain=None, tm=512):
    m, d = x2.shape
    kdim = w_stack.shape[1]
    row = pl.BlockSpec((tm, d), lambda i: (i, 0))
    normed = next_gain is not None
    extra = ([next_gain.reshape(1, d)], [pl.BlockSpec((1, d), lambda i: (0, 0))]) if normed else ([], [])
    out = pl.pallas_call(
        functools.partial(_out_proj_kernel, ny=len(ys), normed=normed),
        grid=(m // tm,),
        in_specs=[pl.BlockSpec((tm, y.shape[1]), lambda i: (i, 0)) for y in ys]
                 + [pl.BlockSpec((1, kdim, d), lambda i: (layer, 0, 0), pipeline_mode=pl.Buffered(1)), row] + extra[1],
        out_specs=[row, row] if normed else [row],
        out_shape=[jax.ShapeDtypeStruct((m, d), F32)] + ([jax.ShapeDtypeStruct((m, d), BF16)] if normed else []),
        scratch_shapes=[pltpu.VMEM((kdim, d), BF16)],
        compiler_params=_params(("arbitrary",)),
        name="out_proj",
    )(*ys, w_stack, x2, *extra[0])
    return (out[0], out[1]) if normed else (out[0], None)


OD_TN = 512
OD_NQ = SWA_W // OD_TN


def _odd_kinds(j, aux, nslab):
    gain_ref, c64_ref, s64_ref = aux
    qk = lambda idx: (lambda a: _rope(_half_rms(a, gain_ref[idx]), c64_ref[...], s64_ref[...], SWA_HEAD_DIM))
    nk = SWA_KV_W // LANES
    return [
        (j < OD_NQ, lambda: [(qk(0), 0, nslab)]),
        (j == OD_NQ, lambda: [(qk(1), 0, nk), (lambda a: a, nk, nslab)]),
        (j > OD_NQ, lambda: [(lambda a: a * jax.nn.sigmoid(a), 0, nslab)]),
    ]


def _odd_proj(xn, w, layer, gains, c64, s64, seq):
    gspec = pl.BlockSpec((2, 1, LANES), lambda t: (0, 0, 0))
    return _skewed_proj(xn, w, layer, False, lambda j: j, (gains, c64, s64), [gspec, None, None],
                        _odd_kinds, seq, OD_TN, w.shape[2] // OD_TN, "odd_proj")


def _swa_kernel(sink_ref, q_ref, kvp_ref, kvc_ref, *rest):
    *g_refs, o_ref = rest
    n = pl.program_id(1)
    w = SWA_WINDOW
    dh = SWA_HEAD_DIM
    group = SWA_HEADS // SWA_KV_HEADS
    gw = group * w
    kb = lax.broadcasted_iota(I32, (2 * w, gw), 0)
    qi = lax.broadcasted_iota(I32, (2 * w, gw), 1) & (w - 1)
    valid = (kb > qi) & (kb <= qi + w) & ((n > 0) | (kb >= w))
    bias = jnp.where(valid, 0.0, NEG)
    head_of_lane = lax.broadcasted_iota(I32, (1, gw), 1) // w
    ones = jnp.ones((ONES_ROWS, 2 * w), BF16)

    def band(col):
        sl = slice(col, col + dh)
        return jnp.concatenate([kvp_ref[:, sl], kvc_ref[:, sl]], axis=0)

    for kv in range(SWA_KV_HEADS):
        h0 = kv * group
        qg = jnp.concatenate([q_ref[:, (h0 + t) * dh:(h0 + t + 1) * dh] for t in range(group)], axis=0)
        z = bias + _dot_nt(band(kv * dh), qg)
        sink = jnp.zeros((1, gw), F32)
        for t in range(group):
            sink = jnp.where(head_of_lane == t, sink_ref[h0 + t] * LOG2E, sink)
        m = jnp.maximum(jnp.max(z, axis=0, keepdims=True), sink)
        vt = jnp.concatenate([band(SWA_KV_W + kv * dh).astype(F32).T.astype(BF16), ones], axis=0)
        pv = _dot(vt, jnp.exp2(z - m).astype(BF16))
        den = pv[dh:dh + 1] + jnp.exp2(sink - m)
        ot = (pv[0:dh] * (1.0 / den)).T
        o = jnp.concatenate([ot[t * w:(t + 1) * w] for t in range(group)], axis=1)
        sl = slice(h0 * dh, (h0 + group) * dh)
        o_ref[:, sl] = (o * g_refs[kv][...].astype(F32)).astype(o_ref.dtype)


def _swa_attention(hodd, sinks, batch, seq):
    w = SWA_WINDOW
    nb = seq // w
    kvw = 2 * SWA_KV_W
    kv_col = SWA_W // kvw
    gate = lambda kv: pl.BlockSpec((w, kvw), lambda b, n: (b * nb + n, kv_col + 1 + kv))
    return pl.pallas_call(
        _swa_kernel,
        grid=(batch, nb),
        in_specs=[pl.BlockSpec(memory_space=pltpu.SMEM),
                  pl.BlockSpec((w, SWA_W), lambda b, n: (b * nb + n, 0)),
                  pl.BlockSpec((w, kvw), lambda b, n: (b * nb + jnp.maximum(n - 1, 0), kv_col)),
                  pl.BlockSpec((w, kvw), lambda b, n: (b * nb + n, kv_col))]
                 + [gate(kv) for kv in range(SWA_KV_HEADS)],
        out_specs=pl.BlockSpec((w, SWA_W), lambda b, n: (b * nb + n, 0)),
        out_shape=jax.ShapeDtypeStruct((batch * seq, SWA_W), BF16),
        compiler_params=_params(("parallel", "arbitrary")),
        name="swa_attention",
    )(sinks, hodd, hodd, hodd, *([hodd] * SWA_KV_HEADS))


def _rope_tables(seq, dim):
    inv = 1.0 / (ROPE_THETA ** (jnp.arange(0, dim, 2, dtype=F32) / dim))
    ang = jnp.arange(seq, dtype=F32)[:, None] * inv[None, :]
    cos, sin = jnp.cos(ang), jnp.sin(ang)
    reps = LANES // dim
    return (jnp.tile(jnp.concatenate([cos, cos], -1), (1, reps)),
            jnp.tile(jnp.concatenate([-sin, sin], -1), (1, reps)))


def _cols(w, start, size):
    return lax.slice_in_dim(w, start, start + size, axis=1)


def _even_layer(x2, xn, pre_gain, next_gain, batch, seq, w_in_stack, j, b_f, g_fox, g_dsa, g_kidx, w_out_stack,
                tabs, topk):
    c128, s128, c64, s64 = tabs
    ncol = w_in_stack.shape[2]
    row0 = j * ncol
    fgt_row = row0 + 4 * FOX_W
    idx_row = fgt_row + FOX_HEADS + 4 * DSA_W + IDX_W
    pad = LANES - (IDX_DIM + IDX_HEADS + FOX_HEADS)
    wt = jnp.swapaxes(w_in_stack, 1, 2).reshape(-1, w_in_stack.shape[1])
    qscale = HEAD_DIM ** -0.5 * LOG2E
    one = jnp.ones((LANES,), F32)
    gains = jnp.stack([g_fox[0] * qscale, g_fox[1], one, one,
                       g_dsa[0] * qscale, g_dsa[1], one, one, one]).reshape(EV_IQ + 1, 1, LANES)
    gk = jnp.concatenate([g_kidx, jnp.zeros((LANES - IDX_DIM,), F32)]).reshape(1, LANES)
    bf = jnp.concatenate([jnp.zeros((SM_CF,), F32), b_f, jnp.zeros((pad,), F32)]).reshape(1, LANES)

    small, small_t, xn = _even_small(x2 if xn is None else xn, pre_gain if xn is None else None,
                                     wt, idx_row, fgt_row, gk, bf, c64, s64, batch, seq)
    hmain = _even_proj(xn, wt, row0, gains, c128, s128, c64, s64, seq)
    ya = _fox_attention(hmain, small, batch, seq)
    yb = _dsa_attention(hmain, small, small_t, batch, seq, topk)
    return _out_proj([ya, yb], w_out_stack, j, x2, next_gain)


def _odd_layer(x2, xn, next_gain, batch, seq, w_in_stack, j, g_qk, sinks, w_out_stack, tabs):
    _, _, c64, s64 = tabs
    qscale = jnp.array([[SWA_HEAD_DIM ** -0.5 * LOG2E], [1.0]], F32)
    gains = jnp.tile(g_qk * qscale, (1, LANES // SWA_HEAD_DIM)).reshape(2, 1, LANES)
    hodd = _odd_proj(xn, w_in_stack, j, gains, c64, s64, seq)
    y = _swa_attention(hodd, sinks, batch, seq)
    return _out_proj([y], w_out_stack, j, x2, next_gain)


def kernel(x, norm_even, w_in_even, b_f_even, g_qk_fox, g_qk_dsa, g_kidx, w_out_even,
           norm_odd, w_in_odd, g_qk_swa, sinks, w_out_odd):
    batch, seq, d = x.shape
    depth = norm_even.shape[0] + norm_odd.shape[0]
    topk = min(IDX_TOPK_MAX, seq // 4)
    tabs = _rope_tables(seq, HEAD_DIM) + _rope_tables(seq, IDX_DIM)
    x2 = x.reshape(batch * seq, d)
    pre_gain = lambda layer: (norm_even, norm_odd)[layer % 2][layer // 2]
    xn = None
    for layer in range(depth):
        j = layer // 2
        next_gain = pre_gain(layer + 1) if layer + 1 < depth else None
        if layer % 2 == 0:
            x2, xn = _even_layer(x2, xn, pre_gain(layer), next_gain, batch, seq, w_in_even, j, b_f_even[j], g_qk_fox[j],
                                 g_qk_dsa[j], g_kidx[j], w_out_even, tabs, topk)
        else:
            x2, xn = _odd_layer(x2, xn, next_gain, batch, seq, w_in_odd, j, g_qk_swa[j], sinks[j],
                                w_out_odd, tabs)
    return x2.reshape(batch, seq, d)
```

```python
import functools

import jax
import jax.numpy as jnp
from jax import lax
from jax.experimental import pallas as pl
from jax.experimental.pallas import tpu as pltpu

F32 = jnp.float32
BF16 = jnp.bfloat16
I32 = jnp.int32

HEAD_DIM = 128
FOX_HEADS = 8
DSA_HEADS = 8
IDX_HEADS = 16
IDX_DIM = 64
IDX_TOPK_MAX = 256
SWA_HEADS = 32
SWA_KV_HEADS = 4
SWA_HEAD_DIM = 64
SWA_WINDOW = 128
ROPE_THETA = 10000.0
EPS = 1e-6
NEG = -1e30

FOX_W = FOX_HEADS * HEAD_DIM
DSA_W = DSA_HEADS * HEAD_DIM
IDX_W = IDX_HEADS * IDX_DIM
SWA_W = SWA_HEADS * SWA_HEAD_DIM
SWA_KV_W = SWA_KV_HEADS * SWA_HEAD_DIM

LANES = 128
VMEM_LIMIT = 56 * 2 ** 20

SM_IK = 0
SM_IW = IDX_DIM
SM_CF = IDX_DIM + IDX_HEADS

LOG2E = 1.4426950408889634
INT_MIN = -2 ** 31
IDX_BITS = 12


def _params(sem):
    return pltpu.CompilerParams(dimension_semantics=sem, vmem_limit_bytes=VMEM_LIMIT)


def _dot(a, b):
    return jnp.dot(a, b, preferred_element_type=F32)


def _dot_nt(a, b):
    return lax.dot_general(a, b, (((1,), (1,)), ((), ())), preferred_element_type=F32)


def _lane_iota(shape):
    return lax.broadcasted_iota(I32, shape, 1)


def _rope(y, cos, sin, dim):
    if dim == LANES:
        rot = pltpu.roll(y, LANES // 2, 1)
    else:
        half = dim // 2
        first = (_lane_iota(y.shape) & half) == 0
        rot = jnp.where(first, pltpu.roll(y, LANES - half, 1), pltpu.roll(y, half, 1))
    return y * cos + rot * sin


def _head_rms(a, gain):
    return a * lax.rsqrt(jnp.mean(a * a, axis=-1, keepdims=True) + EPS) * gain


def _half_rms(a, gain):
    lo = _lane_iota(a.shape) < 64
    sq = a * a
    ms_lo = jnp.sum(jnp.where(lo, sq, 0.0), axis=-1, keepdims=True) * (1.0 / 64)
    ms_hi = jnp.sum(jnp.where(lo, 0.0, sq), axis=-1, keepdims=True) * (1.0 / 64)
    return a * lax.rsqrt(jnp.where(lo, ms_lo, ms_hi) + EPS) * gain


def _skewed_proj_kernel(xn_ref, w_ref, *rest, kinds, transposed, nj, tn):
    *aux, o_ref, acc_ref = rest
    t = pl.program_id(0)
    last = pl.num_programs(0) - 1
    j = (t + nj - 1) % nj

    def finish(parts):
        for fn, lo, hi in parts:
            for h in range(lo, hi):
                sl = slice(h * LANES, (h + 1) * LANES)
                o_ref[:, sl] = fn(acc_ref[:, sl]).astype(o_ref.dtype)

    def multiply():
        w = (w_ref[...] if transposed else w_ref[0]).astype(BF16)
        acc_ref[...] = _dot_nt(xn_ref[...], w) if transposed else _dot(xn_ref[...], w)

    @pl.when(t == 0)
    def _():
        multiply()

    nslab = tn // LANES
    for cond, parts in kinds(j, aux, nslab):
        @pl.when((t > 0) & (t < last) & cond)
        def _(parts=parts):
            finish(parts())
            multiply()

    for cond, parts in kinds(nj - 1, aux, nslab):
        if cond:
            @pl.when(t == last)
            def _(parts=parts):
                finish(parts())


def _skewed_proj(xn, w_stack, layer, transposed, locate, aux, aux_specs, kinds, seq, tn, nj, name, tm=1024):
    m, d = xn.shape
    ntiles = (m // tm) * nj
    nrow = seq // tm
    cur = lambda t: jnp.minimum(t, ntiles - 1)
    prev = lambda t: jnp.maximum(t - 1, 0)
    tab = pl.BlockSpec((tm, LANES), lambda t: ((prev(t) // nj) % nrow, 0))
    specs = [tab if s is None else s for s in aux_specs]
    if transposed:
        wspec = pl.BlockSpec((pl.Element(tn), pl.Element(d)), lambda t: (locate(cur(t) % nj), 0))
    else:
        wspec = pl.BlockSpec((1, d, tn), lambda t: (layer, 0, locate(cur(t) % nj)))
    return pl.pallas_call(
        functools.partial(_skewed_proj_kernel, kinds=kinds, transposed=transposed, nj=nj, tn=tn),
        grid=(ntiles + 1,),
        in_specs=[pl.BlockSpec((tm, d), lambda t: (cur(t) // nj, 0)), wspec] + specs,
        out_specs=pl.BlockSpec((tm, tn), lambda t: (prev(t) // nj, prev(t) % nj)),
        out_shape=jax.ShapeDtypeStruct((m, nj * tn), BF16),
        scratch_shapes=[pltpu.VMEM((tm, tn), F32)],
        compiler_params=_params(("arbitrary",)),
        name=name,
    )(xn, w_stack, *aux)


EV_TN = 1024
EV_FQ, EV_FK, EV_FV, EV_FG, EV_DQ, EV_DK, EV_DV, EV_DG, EV_IQ = range(9)
EV_NFOX = 4


def _even_kinds(j, aux, nslab):
    gain_ref, c128_ref, s128_ref, c64_ref, s64_ref = aux
    gain = lambda: gain_ref[jnp.minimum(j, EV_IQ)]
    norm = lambda: [(lambda a: _head_rms(a, gain()), 0, nslab)]
    norm_rope = lambda: [(lambda a: _rope(_head_rms(a, gain()), c128_ref[...], s128_ref[...], HEAD_DIM), 0, nslab)]
    cast = lambda: [(lambda a: a, 0, nslab)]
    silu = lambda: [(lambda a: a * jax.nn.sigmoid(a), 0, nslab)]
    rope64 = lambda: [(lambda a: _rope(a, c64_ref[...], s64_ref[...], IDX_DIM), 0, nslab)]
    return [
        ((j == EV_FQ) | (j == EV_FK), norm),
        ((j == EV_DQ) | (j == EV_DK), norm_rope),
        ((j == EV_FV) | (j == EV_DV), cast),
        ((j == EV_FG) | (j == EV_DG), silu),
        (j == EV_IQ, rope64),
    ]


def _even_proj(xn, wt, row0, gains, c128, s128, c64, s64, seq):
    gspec = pl.BlockSpec(gains.shape, lambda t: (0, 0, 0))
    row = lambda j: (row0 // 8 + j * (EV_TN // 8) + (j + EV_NFOX) // (2 * EV_NFOX)) * 8
    return _skewed_proj(xn, wt, None, True, row, (gains, c128, s128, c64, s64),
                        [gspec, None, None, None, None], _even_kinds, seq, EV_TN, EV_IQ + 1, "even_proj")


ES_TM = 512


def _even_small_kernel(*refs, normalize):
    if normalize:
        x_ref, g_ref, wi_ref, wf_ref, gk_ref, bf_ref, c64_ref, s64_ref, o_ref, ot_ref, xn_ref, carry_ref = refs
        x = x_ref[...]
        xn = (x * lax.rsqrt(jnp.mean(x * x, axis=-1, keepdims=True) + EPS) * g_ref[...]).astype(BF16)
        xn_ref[...] = xn
    else:
        xn_ref, wi_ref, wf_ref, gk_ref, bf_ref, c64_ref, s64_ref, o_ref, ot_ref, carry_ref = refs
        xn = xn_ref[...]
    pad = jnp.zeros((LANES - wi_ref.shape[0] - wf_ref.shape[0], xn.shape[1]), F32)
    wt = jnp.concatenate([wi_ref[...], wf_ref[...], pad], axis=0).astype(BF16)
    h = _dot_nt(xn, wt)
    lane = _lane_iota(h.shape)
    is_ik = lane < SM_IW
    ms = jnp.sum(jnp.where(is_ik, h * h, 0.0), axis=-1, keepdims=True) * (1.0 / IDX_DIM)
    ik = _rope(h * lax.rsqrt(ms + EPS) * gk_ref[...], c64_ref[...], s64_ref[...], IDX_DIM)
    iw = h * (IDX_HEADS ** -0.5 * IDX_DIM ** -0.5)
    c = jax.nn.log_sigmoid(h + bf_ref[...])
    row = lax.broadcasted_iota(I32, h.shape, 0)
    step = 1
    while step < h.shape[0]:
        c = c + jnp.where(row >= step, pltpu.roll(c, step, 0), 0.0)
        step *= 2

    @pl.when(pl.program_id(1) == 0)
    def _():
        carry_ref[...] = jnp.zeros(carry_ref.shape, F32)

    c = c + carry_ref[...]
    carry_ref[...] = c[h.shape[0] - 1:, :]
    out = jnp.where(is_ik, ik, jnp.where(lane < SM_CF, iw, c))
    o_ref[...] = out
    ot_ref[0] = out.T


def _even_small(x_or_xn, norm_gain, wt, idx_row, fgt_row, gk, bf, c64, s64, batch, seq):
    m, d = x_or_xn.shape
    tm = ES_TM
    nr = seq // tm
    normalize = norm_gain is not None
    vec = pl.BlockSpec((1, LANES), lambda b, r: (0, 0))
    tab = pl.BlockSpec((tm, LANES), lambda b, r: (r, 0))
    rows = pl.BlockSpec((tm, d), lambda b, r: (b * nr + r, 0))
    head = ([x_or_xn, norm_gain.reshape(1, d)], [rows, pl.BlockSpec((1, d), lambda b, r: (0, 0))]) if normalize \
        else ([x_or_xn], [rows])
    out = pl.pallas_call(
        functools.partial(_even_small_kernel, normalize=normalize),
        grid=(batch, nr),
        in_specs=head[1] + [pl.BlockSpec((pl.Element(IDX_DIM + IDX_HEADS), pl.Element(d)), lambda b, r: (idx_row, 0)),
                            pl.BlockSpec((pl.Element(FOX_HEADS), pl.Element(d)), lambda b, r: (fgt_row, 0)),
                            vec, vec, tab, tab],
        out_specs=[pl.BlockSpec((tm, LANES), lambda b, r: (b * nr + r, 0)),
                   pl.BlockSpec((1, LANES, tm), lambda b, r: (b, 0, r))] + ([rows] if normalize else []),
        out_shape=[jax.ShapeDtypeStruct((m, LANES), F32),
                   jax.ShapeDtypeStruct((batch, LANES, seq), F32)]
                  + ([jax.ShapeDtypeStruct((m, d), BF16)] if normalize else []),
        scratch_shapes=[pltpu.VMEM((1, LANES), F32)],
        compiler_params=_params(("parallel", "arbitrary")),
        name="even_small",
    )(*head[0], wt, wt, gk, bf, c64, s64)
    return (out[0], out[1], out[2]) if normalize else (out[0], out[1], x_or_xn)


FOX_TQ = 256
FOX_HPS = 8
ONES_ROWS = 16


def _fox_kernel(q_ref, k_ref, v_ref, g_ref, small_ref, o_ref, ka_ref, vt_ref, *, nq):
    hg = pl.program_id(1)
    i = pl.program_id(2)
    tq = FOX_TQ
    seq = k_ref.shape[0]
    heads = [slice(u * HEAD_DIM, (u + 1) * HEAD_DIM) for u in range(FOX_HPS)]
    lane = _lane_iota((seq, LANES))

    @pl.when(i == 0)
    def _():
        sm = small_ref[...]
        for u, sl in enumerate(heads):
            vt_ref[u, 0:HEAD_DIM, :] = v_ref[:, sl].astype(F32).T.astype(BF16)
            vt_ref[u, HEAD_DIM:, :] = jnp.ones((ONES_ROWS, seq), BF16)
            ck = jnp.sum(jnp.where(lane == SM_CF + hg * FOX_HPS + u, sm, 0.0), axis=1, keepdims=True)
            neg = jnp.broadcast_to(ck * -LOG2E, (seq, LANES))
            hi = neg.astype(BF16).astype(F32)
            mid = (neg - hi).astype(BF16).astype(F32)
            lo = neg - hi - mid
            extra = jnp.where(lane == 0, hi, jnp.where(lane == 1, mid, jnp.where(lane == 2, lo, 0.0)))
            ka_ref[u, :, 0:HEAD_DIM] = k_ref[:, sl]
            ka_ref[u, :, HEAD_DIM:] = extra.astype(BF16)

    ones3 = jnp.where(_lane_iota((tq, LANES)) < 3, 1.0, 0.0).astype(BF16)
    qa = [jnp.concatenate([q_ref[:, sl], ones3], axis=1) for sl in heads]

    def qk(u, j):
        return _dot_nt(ka_ref[u, pl.ds(j * tq, tq), :], qa[u])

    def step(u, j, z, carry, diagonal):
        m, acc = carry
        if diagonal:
            key = lax.broadcasted_iota(I32, (tq, tq), 0)
            qry = lax.broadcasted_iota(I32, (tq, tq), 1)
            z = jnp.where(key <= qry, z, NEG)
        m_new = jnp.maximum(m, jnp.max(z, axis=0, keepdims=True))
        p = jnp.exp2(z - m_new)
        acc = jnp.exp2(m - m_new) * acc + _dot(vt_ref[u, :, pl.ds(j * tq, tq)], p.astype(BF16))
        return m_new, acc

    init = (jnp.full((1, tq), -jnp.inf, F32), jnp.zeros((HEAD_DIM + ONES_ROWS, tq), F32))
    for i_static in range(nq):
        @pl.when(i == i_static)
        def _(i_static=i_static):
            carry = [init] * FOX_HPS
            z_next = [qk(u, 0) for u in range(FOX_HPS)]
            for j in range(i_static + 1):
                for u in range(FOX_HPS):
                    z = z_next[u]
                    if j < i_static:
                        z_next[u] = qk(u, j + 1)
                    carry[u] = step(u, j, z, carry[u], j == i_static)
            for u, sl in enumerate(heads):
                _, acc = carry[u]
                out = (acc[0:HEAD_DIM] * (1.0 / acc[HEAD_DIM:HEAD_DIM + 1])).T
                o_ref[:, sl] = (out * g_ref[:, sl].astype(F32)).astype(o_ref.dtype)


def _fox_attention(hmain, small, batch, seq):
    tq = FOX_TQ
    nq = seq // tq
    width = FOX_HPS * HEAD_DIM
    ng = FOX_W // width
    return pl.pallas_call(
        functools.partial(_fox_kernel, nq=nq),
        grid=(batch, ng, nq),
        in_specs=[pl.BlockSpec((tq, width), lambda b, h, i: (b * nq + i, EV_FQ * ng + h)),
                  pl.BlockSpec((seq, width), lambda b, h, i: (b, EV_FK * ng + h)),
                  pl.BlockSpec((seq, width), lambda b, h, i: (b, EV_FV * ng + h)),
                  pl.BlockSpec((tq, width), lambda b, h, i: (b * nq + i, EV_FG * ng + h)),
                  pl.BlockSpec((seq, LANES), lambda b, h, i: (b, 0))],
        out_specs=pl.BlockSpec((tq, width), lambda b, h, i: (b * nq + i, h)),
        out_shape=jax.ShapeDtypeStruct((batch * seq, FOX_W), BF16),
        scratch_shapes=[pltpu.VMEM((FOX_HPS, seq, 2 * HEAD_DIM), BF16),
                        pltpu.VMEM((FOX_HPS, HEAD_DIM + ONES_ROWS, seq), BF16)],
        compiler_params=_params(("parallel", "parallel", "arbitrary")),
        name="fox_attention",
    )(hmain, hmain, hmain, hmain, small)


DSA_TQ = 256


def _key_to_f32(key):
    return pltpu.bitcast(key ^ ((key >> 31) & 0x7FFFFFFF), F32)


def _dsa_kernel(iq_ref, iqn_ref, small_ref, smallt_ref, q_ref, k_ref, v_ref, g_ref, o_ref,
                score_ref, vt_ref, lim_ref, m_ref, acc_ref, *, topk, nq):
    i = pl.program_id(1)
    tq = DSA_TQ
    nchunk = i + 1
    slot = i % 2
    key0 = lax.broadcasted_iota(I32, (tq, tq), 0)
    lane0 = lax.broadcasted_iota(I32, (tq, tq), 1)
    qry = i * tq + lane0

    @pl.when(i == 0)
    def _():
        for hh in range(DSA_HEADS):
            sl = slice(hh * HEAD_DIM, (hh + 1) * HEAD_DIM)
            vt_ref[hh, 0:HEAD_DIM, :] = v_ref[:, sl].astype(F32).T.astype(BF16)
            vt_ref[hh, HEAD_DIM:, :] = jnp.ones((ONES_ROWS, v_ref.shape[0]), BF16)

    def chunk_off(c):
        return pl.multiple_of(c * tq, tq)

    def score_chunk(c, blk, q_idx_ref, dst):
        off = chunk_off(c)
        qoff = pl.multiple_of(blk * tq, tq)
        ikc = small_ref[pl.ds(off, tq), SM_IK:SM_IK + IDX_DIM].astype(BF16)

        def logits(hh):
            return _dot_nt(ikc, q_idx_ref[:, hh * IDX_DIM:(hh + 1) * IDX_DIM])

        acc = jnp.zeros((tq, tq), F32)
        nxt = logits(0)
        for hh in range(IDX_HEADS):
            cur = nxt
            if hh + 1 < IDX_HEADS:
                nxt = logits(hh + 1)
            w = smallt_ref[0, SM_IW + hh:SM_IW + hh + 1, pl.ds(qoff, tq)]
            acc = acc + w * jnp.maximum(cur, 0.0)
        score_ref[dst, pl.ds(off, tq), :] = jnp.where(off + key0 <= qoff + lane0, acc, NEG)

    @pl.when(i == 0)
    def _():
        score_chunk(0, 0, iq_ref, 0)

    def count(pred):
        def body(c, tot):
            off = chunk_off(c)
            hit = jnp.where(pred(score_ref[slot, pl.ds(off, tq), :], off + key0), 1.0, 0.0)
            return tot + jnp.sum(hit.reshape(tq // 8, 8, tq), axis=0)
        tot = lax.fori_loop(0, nchunk, body, jnp.zeros((8, tq), F32))
        return jnp.sum(tot, axis=0, keepdims=True)

    def thr_step(it, res, nlive):
        cand = res + jnp.left_shift(jnp.int32(1), 31 - it)
        cand_f = _key_to_f32(cand)
        tot = jnp.zeros((8, tq), F32)
        for c in range(nlive):
            hit = jnp.where(score_ref[(nlive - 1) % 2, c * tq:(c + 1) * tq, :] >= cand_f, 1.0, 0.0)
            tot = tot + jnp.sum(hit.reshape(tq // 8, 8, tq), axis=0)
        cnt = jnp.sum(tot, axis=0, keepdims=True)
        return jnp.where(cnt >= topk, cand, res)

    for i_static in range(nq):
        @pl.when(i == i_static)
        def _(i_static=i_static):
            key = lax.fori_loop(0, 32, functools.partial(thr_step, nlive=i_static + 1),
                                jnp.full((1, tq), INT_MIN, I32))
            lim_ref[...] = key

    thr_key = lim_ref[...]
    thr = jnp.where(thr_key == INT_MIN, -jnp.inf, _key_to_f32(thr_key))
    n_gt = count(lambda sc, ki: sc > thr)
    n_ge = count(lambda sc, ki: sc >= thr)
    need = topk - n_gt

    lim_ref[...] = jnp.full((1, tq), 2 ** IDX_BITS, I32)

    @pl.when(jnp.max(n_ge) > topk)
    def _():
        def lim_step(it, res):
            cand = res + jnp.left_shift(jnp.int32(1), IDX_BITS - 1 - it)
            cnt = count(lambda sc, ki: (sc == thr) & (ki < cand))
            return jnp.where(cnt <= need, cand, res)
        lim_ref[...] = lax.fori_loop(0, IDX_BITS, lim_step, jnp.zeros((1, tq), I32))

    lim = lim_ref[...]

    m_ref[...] = jnp.full(m_ref.shape, -jnp.inf, F32)
    acc_ref[...] = jnp.zeros(acc_ref.shape, F32)

    def attn_chunk(c, _, ahead):
        off = chunk_off(c)
        sc = score_ref[slot, pl.ds(off, tq), :]
        ki = off + key0
        sel = ((sc > thr) | ((sc == thr) & (ki < lim))) & (ki <= qry)
        bias = jnp.where(sel, 0.0, NEG)

        def qk(hh):
            sl = slice(hh * HEAD_DIM, (hh + 1) * HEAD_DIM)
            return _dot_nt(k_ref[pl.ds(off, tq), sl], q_ref[:, sl])

        heads = range(DSA_HEADS)
        zs = [bias + qk(hh) for hh in heads]
        if ahead:
            score_chunk(c, i + 1, iqn_ref, 1 - slot)
        ms = [m_ref[hh:hh + 1, :] for hh in heads]
        m_news = [jnp.maximum(ms[hh], jnp.max(zs[hh], axis=0, keepdims=True)) for hh in heads]
        ps = [jnp.exp2(zs[hh] - m_news[hh]).astype(BF16) for hh in heads]
        pvs = [_dot(vt_ref[hh, :, pl.ds(off, tq)], ps[hh]) for hh in heads]
        for hh in heads:
            acc_ref[hh] = jnp.exp2(ms[hh] - m_news[hh]) * acc_ref[hh] + pvs[hh]
            m_ref[hh:hh + 1, :] = m_news[hh]
        return 0

    @pl.when(i + 1 < nq)
    def _():
        lax.fori_loop(0, nchunk, functools.partial(attn_chunk, ahead=True), 0)
        score_chunk(nchunk, i + 1, iqn_ref, 1 - slot)

    @pl.when(i + 1 == nq)
    def _():
        lax.fori_loop(0, nchunk, functools.partial(attn_chunk, ahead=False), 0)

    for hh in range(DSA_HEADS):
        sl = slice(hh * HEAD_DIM, (hh + 1) * HEAD_DIM)
        out = (acc_ref[hh, 0:HEAD_DIM, :] * (1.0 / acc_ref[hh, HEAD_DIM:HEAD_DIM + 1, :])).T
        o_ref[:, sl] = (out * g_ref[:, sl].astype(F32)).astype(o_ref.dtype)


def _dsa_attention(hmain, small, small_t, batch, seq, topk):
    tq = DSA_TQ
    nq = seq // tq
    wide = lambda t: pl.BlockSpec((tq, DSA_W), lambda b, i: (b * nq + i, t))
    full = lambda t: pl.BlockSpec((seq, DSA_W), lambda b, i: (b, t))
    return pl.pallas_call(
        functools.partial(_dsa_kernel, topk=topk, nq=nq),
        grid=(batch, nq),
        in_specs=[wide(EV_IQ),
                  pl.BlockSpec((tq, DSA_W), lambda b, i: (b * nq + jnp.minimum(i + 1, nq - 1), EV_IQ)),
                  pl.BlockSpec((seq, LANES), lambda b, i: (b, 0)),
                  pl.BlockSpec((1, LANES, seq), lambda b, i: (b, 0, 0)),
                  wide(EV_DQ), full(EV_DK), full(EV_DV), wide(EV_DG)],
        out_specs=pl.BlockSpec((tq, DSA_W), lambda b, i: (b * nq + i, 0)),
        out_shape=jax.ShapeDtypeStruct((batch * seq, DSA_W), BF16),
        scratch_shapes=[pltpu.VMEM((2, seq, tq), F32),
                        pltpu.VMEM((DSA_HEADS, HEAD_DIM + ONES_ROWS, seq), BF16),
                        pltpu.VMEM((1, tq), I32),
                        pltpu.VMEM((DSA_HEADS, tq), F32),
                        pltpu.VMEM((DSA_HEADS, HEAD_DIM + ONES_ROWS, tq), F32)],
        compiler_params=_params(("parallel", "arbitrary")),
        name="dsa_attention",
    )(hmain, hmain, small, small_t, hmain, hmain, hmain, hmain)


def _out_proj_kernel(*refs, ny, normed):
    y_refs, w_ref, x_ref = refs[:ny], refs[ny], refs[ny + 1]
    if normed:
        g_ref, o_ref, n_ref, w16_ref = refs[ny + 2:]
    else:
        o_ref, w16_ref = refs[ny + 2:]

    @pl.when(pl.program_id(0) == 0)
    def _():
        w16_ref[...] = w_ref[0].astype(BF16)

    acc = x_ref[...]
    off = 0
    for y_ref in y_refs:
        kdim = y_ref.shape[1]
        acc = acc + _dot(y_ref[...], w16_ref[off:off + kdim, :])
        off += kdim
    o_ref[...] = acc
    if normed:
        y = acc * lax.rsqrt(jnp.mean(acc * acc, axis=-1, keepdims=True) + EPS)
        n_ref[...] = (y * g_ref[...]).astype(n_ref.dtype)


def _out_proj(ys, w_stack, layer, x2, next_gain=None, tm=512):
    m, d = x2.shape
    kdim = w_stack.shape[1]
    row = pl.BlockSpec((tm, d), lambda i: (i, 0))
    normed = next_gain is not None
    extra = ([next_gain.reshape(1, d)], [pl.BlockSpec((1, d), lambda i: (0, 0))]) if normed else ([], [])
    out = pl.pallas_call(
        functools.partial(_out_proj_kernel, ny=len(ys), normed=normed),
        grid=(m // tm,),
        in_specs=[pl.BlockSpec((tm, y.shape[1]), lambda i: (i, 0)) for y in ys]
                 + [pl.BlockSpec((1, kdim, d), lambda i: (layer, 0, 0), pipeline_mode=pl.Buffered(1)), row] + extra[1],
        out_specs=[row, row] if normed else [row],
        out_shape=[jax.ShapeDtypeStruct((m, d), F32)] + ([jax.ShapeDtypeStruct((m, d), BF16)] if normed else []),
        scratch_shapes=[pltpu.VMEM((kdim, d), BF16)],
        compiler_params=_params(("arbitrary",)),
        name="out_proj",
    )(*ys, w_stack, x2, *extra[0])
    return (out[0], out[1]) if normed else (out[0], None)


OD_TN = 512
OD_NQ = SWA_W // OD_TN


def _odd_kinds(j, aux, nslab):
    gain_ref, c64_ref, s64_ref = aux
    qk = lambda idx: (lambda a: _rope(_half_rms(a, gain_ref[idx]), c64_ref[...], s64_ref[...], SWA_HEAD_DIM))
    nk = SWA_KV_W // LANES
    return [
        (j < OD_NQ, lambda: [(qk(0), 0, nslab)]),
        (j == OD_NQ, lambda: [(qk(1), 0, nk), (lambda a: a, nk, nslab)]),
        (j > OD_NQ, lambda: [(lambda a: a * jax.nn.sigmoid(a), 0, nslab)]),
    ]


def _odd_proj(xn, w, layer, gains, c64, s64, seq):
    gspec = pl.BlockSpec((2, 1, LANES), lambda t: (0, 0, 0))
    return _skewed_proj(xn, w, layer, False, lambda j: j, (gains, c64, s64), [gspec, None, None],
                        _odd_kinds, seq, OD_TN, w.shape[2] // OD_TN, "odd_proj")


def _swa_kernel(sink_ref, q_ref, kvp_ref, kvc_ref, *rest):
    *g_refs, o_ref = rest
    n = pl.program_id(1)
    w = SWA_WINDOW
    dh = SWA_HEAD_DIM
    group = SWA_HEADS // SWA_KV_HEADS
    gw = group * w
    kb = lax.broadcasted_iota(I32, (2 * w, gw), 0)
    qi = lax.broadcasted_iota(I32, (2 * w, gw), 1) & (w - 1)
    valid = (kb > qi) & (kb <= qi + w) & ((n > 0) | (kb >= w))
    bias = jnp.where(valid, 0.0, NEG)
    head_of_lane = lax.broadcasted_iota(I32, (1, gw), 1) // w
    ones = jnp.ones((ONES_ROWS, 2 * w), BF16)

    def band(col):
        sl = slice(col, col + dh)
        return jnp.concatenate([kvp_ref[:, sl], kvc_ref[:, sl]], axis=0)

    def scores(kv):
        qg = jnp.concatenate([q_ref[:, (kv * group + t) * dh:(kv * group + t + 1) * dh] for t in range(group)],
                             axis=0)
        return _dot_nt(band(kv * dh), qg)

    s_next = scores(0)
    for kv in range(SWA_KV_HEADS):
        h0 = kv * group
        z = bias + s_next
        if kv + 1 < SWA_KV_HEADS:
            s_next = scores(kv + 1)
        sink = jnp.zeros((1, gw), F32)
        for t in range(group):
            sink = jnp.where(head_of_lane == t, sink_ref[h0 + t] * LOG2E, sink)
        m = jnp.maximum(jnp.max(z, axis=0, keepdims=True), sink)
        vt = jnp.concatenate([band(SWA_KV_W + kv * dh).astype(F32).T.astype(BF16), ones], axis=0)
        pv = _dot(vt, jnp.exp2(z - m).astype(BF16))
        den = pv[dh:dh + 1] + jnp.exp2(sink - m)
        ot = (pv[0:dh] * (1.0 / den)).T
        o = jnp.concatenate([ot[t * w:(t + 1) * w] for t in range(group)], axis=1)
        sl = slice(h0 * dh, (h0 + group) * dh)
        o_ref[:, sl] = (o * g_refs[kv][...].astype(F32)).astype(o_ref.dtype)


def _swa_attention(hodd, sinks, batch, seq):
    w = SWA_WINDOW
    nb = seq // w
    kvw = 2 * SWA_KV_W
    kv_col = SWA_W // kvw
    gate = lambda kv: pl.BlockSpec((w, kvw), lambda b, n: (b * nb + n, kv_col + 1 + kv))
    return pl.pallas_call(
        _swa_kernel,
        grid=(batch, nb),
        in_specs=[pl.BlockSpec(memory_space=pltpu.SMEM),
                  pl.BlockSpec((w, SWA_W), lambda b, n: (b * nb + n, 0)),
                  pl.BlockSpec((w, kvw), lambda b, n: (b * nb + jnp.maximum(n - 1, 0), kv_col)),
                  pl.BlockSpec((w, kvw), lambda b, n: (b * nb + n, kv_col))]
                 + [gate(kv) for kv in range(SWA_KV_HEADS)],
        out_specs=pl.BlockSpec((w, SWA_W), lambda b, n: (b * nb + n, 0)),
        out_shape=jax.ShapeDtypeStruct((batch * seq, SWA_W), BF16),
        compiler_params=_params(("parallel", "arbitrary")),
        name="swa_attention",
    )(sinks, hodd, hodd, hodd, *([hodd] * SWA_KV_HEADS))


def _rope_tables(seq, dim):
    inv = 1.0 / (ROPE_THETA ** (jnp.arange(0, dim, 2, dtype=F32) / dim))
    ang = jnp.arange(seq, dtype=F32)[:, None] * inv[None, :]
    cos, sin = jnp.cos(ang), jnp.sin(ang)
    reps = LANES // dim
    return (jnp.tile(jnp.concatenate([cos, cos], -1), (1, reps)),
            jnp.tile(jnp.concatenate([-sin, sin], -1), (1, reps)))


def _even_layer(x2, xn, pre_gain, next_gain, batch, seq, w_in_stack, j, b_f, g_fox, g_dsa, g_kidx, w_out_stack,
                tabs, topk):
    c128, s128, c64, s64 = tabs
    ncol = w_in_stack.shape[2]
    row0 = j * ncol
    fgt_row = row0 + 4 * FOX_W
    idx_row = fgt_row + FOX_HEADS + 4 * DSA_W + IDX_W
    pad = LANES - (IDX_DIM + IDX_HEADS + FOX_HEADS)
    wt = jnp.swapaxes(w_in_stack, 1, 2).reshape(-1, w_in_stack.shape[1])
    qscale = HEAD_DIM ** -0.5 * LOG2E
    one = jnp.ones((LANES,), F32)
    gains = jnp.stack([g_fox[0] * qscale, g_fox[1], one, one,
                       g_dsa[0] * qscale, g_dsa[1], one, one, one]).reshape(EV_IQ + 1, 1, LANES)
    gk = jnp.concatenate([g_kidx, jnp.zeros((LANES - IDX_DIM,), F32)]).reshape(1, LANES)
    bf = jnp.concatenate([jnp.zeros((SM_CF,), F32), b_f, jnp.zeros((pad,), F32)]).reshape(1, LANES)

    small, small_t, xn = _even_small(x2 if xn is None else xn, pre_gain if xn is None else None,
                                     wt, idx_row, fgt_row, gk, bf, c64, s64, batch, seq)
    hmain = _even_proj(xn, wt, row0, gains, c128, s128, c64, s64, seq)
    ya = _fox_attention(hmain, small, batch, seq)
    yb = _dsa_attention(hmain, small, small_t, batch, seq, topk)
    return _out_proj([ya, yb], w_out_stack, j, x2, next_gain)


def _odd_layer(x2, xn, next_gain, batch, seq, w_in_stack, j, g_qk, sinks, w_out_stack, tabs):
    _, _, c64, s64 = tabs
    qscale = jnp.array([[SWA_HEAD_DIM ** -0.5 * LOG2E], [1.0]], F32)
    gains = jnp.tile(g_qk * qscale, (1, LANES // SWA_HEAD_DIM)).reshape(2, 1, LANES)
    hodd = _odd_proj(xn, w_in_stack, j, gains, c64, s64, seq)
    y = _swa_attention(hodd, sinks, batch, seq)
    return _out_proj([y], w_out_stack, j, x2, next_gain)


def kernel(x, norm_even, w_in_even, b_f_even, g_qk_fox, g_qk_dsa, g_kidx, w_out_even,
           norm_odd, w_in_odd, g_qk_swa, sinks, w_out_odd):
    batch, seq, d = x.shape
    depth = norm_even.shape[0] + norm_odd.shape[0]
    topk = min(IDX_TOPK_MAX, seq // 4)
    tabs = _rope_tables(seq, HEAD_DIM) + _rope_tables(seq, IDX_DIM)
    x2 = x.reshape(batch * seq, d)
    pre_gain = lambda layer: (norm_even, norm_odd)[layer % 2][layer // 2]
    xn = None
    for layer in range(depth):
        j = layer // 2
        next_gain = pre_gain(layer + 1) if layer + 1 < depth else None
        if layer % 2 == 0:
            x2, xn = _even_layer(x2, xn, pre_gain(layer), next_gain, batch, seq, w_in_even, j, b_f_even[j], g_qk_fox[j],
                                 g_qk_dsa[j], g_kidx[j], w_out_even, tabs, topk)
        else:
            x2, xn = _odd_layer(x2, xn, next_gain, batch, seq, w_in_odd, j, g_qk_swa[j], sinks[j],
                                w_out_odd, tabs)
    return x2.reshape(batch, seq, d)
```

```python
import functools

import jax
import jax.numpy as jnp
from jax import lax
from jax.experimental import pallas as pl
from jax.experimental.pallas import tpu as pltpu

F32 = jnp.float32
BF16 = jnp.bfloat16
I32 = jnp.int32

HEAD_DIM = 128
FOX_HEADS = 8
DSA_HEADS = 8
IDX_HEADS = 16
IDX_DIM = 64
IDX_TOPK_MAX = 256
SWA_HEADS = 32
SWA_KV_HEADS = 4
SWA_HEAD_DIM = 64
SWA_WINDOW = 128
ROPE_THETA = 10000.0
EPS = 1e-6
NEG = -1e30

FOX_W = FOX_HEADS * HEAD_DIM
DSA_W = DSA_HEADS * HEAD_DIM
IDX_W = IDX_HEADS * IDX_DIM
SWA_W = SWA_HEADS * SWA_HEAD_DIM
SWA_KV_W = SWA_KV_HEADS * SWA_HEAD_DIM

LANES = 128
VMEM_LIMIT = 56 * 2 ** 20

SM_IK = 0
SM_IW = IDX_DIM
SM_CF = IDX_DIM + IDX_HEADS

LOG2E = 1.4426950408889634
INT_MIN = -2 ** 31


def _params(sem):
    return pltpu.CompilerParams(dimension_semantics=sem, vmem_limit_bytes=VMEM_LIMIT)


def _dot(a, b):
    return jnp.dot(a, b, preferred_element_type=F32)


def _dot_nt(a, b):
    return lax.dot_general(a, b, (((1,), (1,)), ((), ())), preferred_element_type=F32)


def _lane_iota(shape):
    return lax.broadcasted_iota(I32, shape, 1)


def _rope(y, cos, sin, dim):
    if dim == LANES:
        rot = pltpu.roll(y, LANES // 2, 1)
    else:
        half = dim // 2
        first = (_lane_iota(y.shape) & half) == 0
        rot = jnp.where(first, pltpu.roll(y, LANES - half, 1), pltpu.roll(y, half, 1))
    return y * cos + rot * sin


def _head_rms(a, gain):
    return a * lax.rsqrt(jnp.mean(a * a, axis=-1, keepdims=True) + EPS) * gain


def _half_rms(a, gain):
    lo = _lane_iota(a.shape) < 64
    sq = a * a
    ms_lo = jnp.sum(jnp.where(lo, sq, 0.0), axis=-1, keepdims=True) * (1.0 / 64)
    ms_hi = jnp.sum(jnp.where(lo, 0.0, sq), axis=-1, keepdims=True) * (1.0 / 64)
    return a * lax.rsqrt(jnp.where(lo, ms_lo, ms_hi) + EPS) * gain


def _skewed_proj_kernel(xn_ref, w_ref, *rest, kinds, transposed, nj, tn):
    *aux, o_ref, acc_ref = rest
    t = pl.program_id(0)
    last = pl.num_programs(0) - 1
    j = (t + nj - 1) % nj

    def finish(parts):
        for fn, lo, hi in parts:
            for h in range(lo, hi):
                sl = slice(h * LANES, (h + 1) * LANES)
                o_ref[:, sl] = fn(acc_ref[:, sl]).astype(o_ref.dtype)

    def multiply():
        w = (w_ref[...] if transposed else w_ref[0]).astype(BF16)
        acc_ref[...] = _dot_nt(xn_ref[...], w) if transposed else _dot(xn_ref[...], w)

    @pl.when(t == 0)
    def _():
        multiply()

    nslab = tn // LANES
    for cond, parts in kinds(j, aux, nslab):
        @pl.when((t > 0) & (t < last) & cond)
        def _(parts=parts):
            finish(parts())
            multiply()

    for cond, parts in kinds(nj - 1, aux, nslab):
        if cond:
            @pl.when(t == last)
            def _(parts=parts):
                finish(parts())


def _skewed_proj(xn, w_stack, layer, transposed, locate, aux, aux_specs, kinds, seq, tn, nj, name, tm=1024):
    m, d = xn.shape
    ntiles = (m // tm) * nj
    nrow = seq // tm
    cur = lambda t: jnp.minimum(t, ntiles - 1)
    prev = lambda t: jnp.maximum(t - 1, 0)
    tab = pl.BlockSpec((tm, LANES), lambda t: ((prev(t) // nj) % nrow, 0))
    specs = [tab if s is None else s for s in aux_specs]
    if transposed:
        wspec = pl.BlockSpec((pl.Element(tn), pl.Element(d)), lambda t: (locate(cur(t) % nj), 0))
    else:
        wspec = pl.BlockSpec((1, d, tn), lambda t: (layer, 0, locate(cur(t) % nj)))
    return pl.pallas_call(
        functools.partial(_skewed_proj_kernel, kinds=kinds, transposed=transposed, nj=nj, tn=tn),
        grid=(ntiles + 1,),
        in_specs=[pl.BlockSpec((tm, d), lambda t: (cur(t) // nj, 0)), wspec] + specs,
        out_specs=pl.BlockSpec((tm, tn), lambda t: (prev(t) // nj, prev(t) % nj)),
        out_shape=jax.ShapeDtypeStruct((m, nj * tn), BF16),
        scratch_shapes=[pltpu.VMEM((tm, tn), F32)],
        compiler_params=_params(("arbitrary",)),
        name=name,
    )(xn, w_stack, *aux)


EV_TN = 1024
EV_FQ, EV_FK, EV_FV, EV_FG, EV_DQ, EV_DK, EV_DV, EV_DG, EV_IQ = range(9)
EV_NFOX = 4


def _even_kinds(j, aux, nslab):
    gain_ref, c128_ref, s128_ref, c64_ref, s64_ref = aux
    gain = lambda: gain_ref[jnp.minimum(j, EV_IQ)]
    norm = lambda: [(lambda a: _head_rms(a, gain()), 0, nslab)]
    norm_rope = lambda: [(lambda a: _rope(_head_rms(a, gain()), c128_ref[...], s128_ref[...], HEAD_DIM), 0, nslab)]
    cast = lambda: [(lambda a: a, 0, nslab)]
    silu = lambda: [(lambda a: a * jax.nn.sigmoid(a), 0, nslab)]
    rope64 = lambda: [(lambda a: _rope(a, c64_ref[...], s64_ref[...], IDX_DIM), 0, nslab)]
    return [
        ((j == EV_FQ) | (j == EV_FK), norm),
        ((j == EV_DQ) | (j == EV_DK), norm_rope),
        ((j == EV_FV) | (j == EV_DV), cast),
        ((j == EV_FG) | (j == EV_DG), silu),
        (j == EV_IQ, rope64),
    ]


def _even_proj(xn, wt, row0, gains, c128, s128, c64, s64, seq):
    gspec = pl.BlockSpec(gains.shape, lambda t: (0, 0, 0))
    row = lambda j: (row0 // 8 + j * (EV_TN // 8) + (j + EV_NFOX) // (2 * EV_NFOX)) * 8
    return _skewed_proj(xn, wt, None, True, row, (gains, c128, s128, c64, s64),
                        [gspec, None, None, None, None], _even_kinds, seq, EV_TN, EV_IQ + 1, "even_proj")


ES_TM = 512


def _even_small_kernel(*refs, normalize):
    if normalize:
        x_ref, g_ref, wi_ref, wf_ref, gk_ref, bf_ref, c64_ref, s64_ref, o_ref, ot_ref, xn_ref, carry_ref = refs
        x = x_ref[...]
        xn = (x * lax.rsqrt(jnp.mean(x * x, axis=-1, keepdims=True) + EPS) * g_ref[...]).astype(BF16)
        xn_ref[...] = xn
    else:
        xn_ref, wi_ref, wf_ref, gk_ref, bf_ref, c64_ref, s64_ref, o_ref, ot_ref, carry_ref = refs
        xn = xn_ref[...]
    pad = jnp.zeros((LANES - wi_ref.shape[0] - wf_ref.shape[0], xn.shape[1]), F32)
    wt = jnp.concatenate([wi_ref[...], wf_ref[...], pad], axis=0).astype(BF16)
    h = _dot_nt(xn, wt)
    lane = _lane_iota(h.shape)
    is_ik = lane < SM_IW
    ms = jnp.sum(jnp.where(is_ik, h * h, 0.0), axis=-1, keepdims=True) * (1.0 / IDX_DIM)
    ik = _rope(h * lax.rsqrt(ms + EPS) * gk_ref[...], c64_ref[...], s64_ref[...], IDX_DIM)
    iw = h * (IDX_HEADS ** -0.5 * IDX_DIM ** -0.5)
    c = jax.nn.log_sigmoid(h + bf_ref[...])
    row = lax.broadcasted_iota(I32, h.shape, 0)
    step = 1
    while step < h.shape[0]:
        c = c + jnp.where(row >= step, pltpu.roll(c, step, 0), 0.0)
        step *= 2

    @pl.when(pl.program_id(1) == 0)
    def _():
        carry_ref[...] = jnp.zeros(carry_ref.shape, F32)

    c = c + carry_ref[...]
    carry_ref[...] = c[h.shape[0] - 1:, :]
    out = jnp.where(is_ik, ik, jnp.where(lane < SM_CF, iw, c))
    o_ref[...] = out
    ot_ref[0] = out.T


def _even_small(x_or_xn, norm_gain, wt, idx_row, fgt_row, gk, bf, c64, s64, batch, seq):
    m, d = x_or_xn.shape
    tm = ES_TM
    nr = seq // tm
    normalize = norm_gain is not None
    vec = pl.BlockSpec((1, LANES), lambda b, r: (0, 0))
    tab = pl.BlockSpec((tm, LANES), lambda b, r: (r, 0))
    rows = pl.BlockSpec((tm, d), lambda b, r: (b * nr + r, 0))
    head = ([x_or_xn, norm_gain.reshape(1, d)], [rows, pl.BlockSpec((1, d), lambda b, r: (0, 0))]) if normalize \
        else ([x_or_xn], [rows])
    out = pl.pallas_call(
        functools.partial(_even_small_kernel, normalize=normalize),
        grid=(batch, nr),
        in_specs=head[1] + [pl.BlockSpec((pl.Element(IDX_DIM + IDX_HEADS), pl.Element(d)), lambda b, r: (idx_row, 0)),
                            pl.BlockSpec((pl.Element(FOX_HEADS), pl.Element(d)), lambda b, r: (fgt_row, 0)),
                            vec, vec, tab, tab],
        out_specs=[pl.BlockSpec((tm, LANES), lambda b, r: (b * nr + r, 0)),
                   pl.BlockSpec((1, LANES, tm), lambda b, r: (b, 0, r))] + ([rows] if normalize else []),
        out_shape=[jax.ShapeDtypeStruct((m, LANES), F32),
                   jax.ShapeDtypeStruct((batch, LANES, seq), F32)]
                  + ([jax.ShapeDtypeStruct((m, d), BF16)] if normalize else []),
        scratch_shapes=[pltpu.VMEM((1, LANES), F32)],
        compiler_params=_params(("parallel", "arbitrary")),
        name="even_small",
    )(*head[0], wt, wt, gk, bf, c64, s64)
    return (out[0], out[1], out[2]) if normalize else (out[0], out[1], x_or_xn)


FOX_TQ = 256
FOX_HPS = 8
ONES_ROWS = 16


def _fox_kernel(q_ref, k_ref, v_ref, g_ref, small_ref, o_ref, ka_ref, vt_ref, *, nq):
    hg = pl.program_id(1)
    i = pl.program_id(2)
    tq = FOX_TQ
    seq = k_ref.shape[0]
    heads = [slice(u * HEAD_DIM, (u + 1) * HEAD_DIM) for u in range(FOX_HPS)]
    lane = _lane_iota((seq, LANES))

    @pl.when(i == 0)
    def _():
        sm = small_ref[...]
        for u, sl in enumerate(heads):
            vt_ref[u, 0:HEAD_DIM, :] = v_ref[:, sl].astype(F32).T.astype(BF16)
            vt_ref[u, HEAD_DIM:, :] = jnp.ones((ONES_ROWS, seq), BF16)
            ck = jnp.sum(jnp.where(lane == SM_CF + hg * FOX_HPS + u, sm, 0.0), axis=1, keepdims=True)
            neg = jnp.broadcast_to(ck * -LOG2E, (seq, LANES))
            hi = neg.astype(BF16).astype(F32)
            mid = (neg - hi).astype(BF16).astype(F32)
            lo = neg - hi - mid
            extra = jnp.where(lane == 0, hi, jnp.where(lane == 1, mid, jnp.where(lane == 2, lo, 0.0)))
            ka_ref[u, :, 0:HEAD_DIM] = k_ref[:, sl]
            ka_ref[u, :, HEAD_DIM:] = extra.astype(BF16)

    ones3 = jnp.where(_lane_iota((tq, LANES)) < 3, 1.0, 0.0).astype(BF16)
    qa = [jnp.concatenate([q_ref[:, sl], ones3], axis=1) for sl in heads]

    def qk(u, j):
        return _dot_nt(ka_ref[u, pl.ds(j * tq, tq), :], qa[u])

    def step(u, j, z, carry, diagonal):
        m, acc = carry
        if diagonal:
            key = lax.broadcasted_iota(I32, (tq, tq), 0)
            qry = lax.broadcasted_iota(I32, (tq, tq), 1)
            z = jnp.where(key <= qry, z, NEG)
        m_new = jnp.maximum(m, jnp.max(z, axis=0, keepdims=True))
        p = jnp.exp2(z - m_new)
        acc = jnp.exp2(m - m_new) * acc + _dot(vt_ref[u, :, pl.ds(j * tq, tq)], p.astype(BF16))
        return m_new, acc

    init = (jnp.full((1, tq), -jnp.inf, F32), jnp.zeros((HEAD_DIM + ONES_ROWS, tq), F32))
    for i_static in range(nq):
        @pl.when(i == i_static)
        def _(i_static=i_static):
            carry = [init] * FOX_HPS
            z_next = [qk(u, 0) for u in range(FOX_HPS)]
            for j in range(i_static + 1):
                for u in range(FOX_HPS):
                    z = z_next[u]
                    if j < i_static:
                        z_next[u] = qk(u, j + 1)
                    carry[u] = step(u, j, z, carry[u], j == i_static)
            for u, sl in enumerate(heads):
                _, acc = carry[u]
                out = (acc[0:HEAD_DIM] * (1.0 / acc[HEAD_DIM:HEAD_DIM + 1])).T
                o_ref[:, sl] = (out * g_ref[:, sl].astype(F32)).astype(o_ref.dtype)


def _fox_attention(hmain, small, batch, seq):
    tq = FOX_TQ
    nq = seq // tq
    width = FOX_HPS * HEAD_DIM
    ng = FOX_W // width
    return pl.pallas_call(
        functools.partial(_fox_kernel, nq=nq),
        grid=(batch, ng, nq),
        in_specs=[pl.BlockSpec((tq, width), lambda b, h, i: (b * nq + i, EV_FQ * ng + h)),
                  pl.BlockSpec((seq, width), lambda b, h, i: (b, EV_FK * ng + h)),
                  pl.BlockSpec((seq, width), lambda b, h, i: (b, EV_FV * ng + h)),
                  pl.BlockSpec((tq, width), lambda b, h, i: (b * nq + i, EV_FG * ng + h)),
                  pl.BlockSpec((seq, LANES), lambda b, h, i: (b, 0))],
        out_specs=pl.BlockSpec((tq, width), lambda b, h, i: (b * nq + i, h)),
        out_shape=jax.ShapeDtypeStruct((batch * seq, FOX_W), BF16),
        scratch_shapes=[pltpu.VMEM((FOX_HPS, seq, 2 * HEAD_DIM), BF16),
                        pltpu.VMEM((FOX_HPS, HEAD_DIM + ONES_ROWS, seq), BF16)],
        compiler_params=_params(("parallel", "parallel", "arbitrary")),
        name="fox_attention",
    )(hmain, hmain, hmain, hmain, small)


DSA_TQ = 256


def _key_to_f32(key):
    return pltpu.bitcast(key ^ ((key >> 31) & 0x7FFFFFFF), F32)


def _dsa_kernel(iq_ref, iqn_ref, small_ref, smallt_ref, q_ref, k_ref, v_ref, g_ref, o_ref,
                score_ref, vt_ref, lim_ref, m_ref, acc_ref, *, topk, nq):
    i = pl.program_id(1)
    tq = DSA_TQ
    nchunk = i + 1
    slot = i % 2
    key0 = lax.broadcasted_iota(I32, (tq, tq), 0)
    lane0 = lax.broadcasted_iota(I32, (tq, tq), 1)
    qry = i * tq + lane0

    @pl.when(i == 0)
    def _():
        for hh in range(DSA_HEADS):
            sl = slice(hh * HEAD_DIM, (hh + 1) * HEAD_DIM)
            vt_ref[hh, 0:HEAD_DIM, :] = v_ref[:, sl].astype(F32).T.astype(BF16)
            vt_ref[hh, HEAD_DIM:, :] = jnp.ones((ONES_ROWS, v_ref.shape[0]), BF16)

    def chunk_off(c):
        return pl.multiple_of(c * tq, tq)

    def score_chunk(c, blk, q_idx_ref, dst):
        off = chunk_off(c)
        qoff = pl.multiple_of(blk * tq, tq)
        ikc = small_ref[pl.ds(off, tq), SM_IK:SM_IK + IDX_DIM].astype(BF16)

        def logits(hh):
            return _dot_nt(ikc, q_idx_ref[:, hh * IDX_DIM:(hh + 1) * IDX_DIM])

        acc = jnp.zeros((tq, tq), F32)
        nxt = logits(0)
        for hh in range(IDX_HEADS):
            cur = nxt
            if hh + 1 < IDX_HEADS:
                nxt = logits(hh + 1)
            w = smallt_ref[0, SM_IW + hh:SM_IW + hh + 1, pl.ds(qoff, tq)]
            acc = acc + w * jnp.maximum(cur, 0.0)
        score_ref[dst, pl.ds(off, tq), :] = jnp.where(off + key0 <= qoff + lane0, acc, NEG)

    @pl.when(i == 0)
    def _():
        score_chunk(0, 0, iq_ref, 0)

    def count(*preds):
        def body(c, tots):
            off = chunk_off(c)
            sc = score_ref[slot, pl.ds(off, tq), :]
            hits = [jnp.where(pred(sc, off + key0), 1.0, 0.0) for pred in preds]
            return tuple(tot + jnp.sum(hit.reshape(tq // 8, 8, tq), axis=0)
                         for tot, hit in zip(tots, hits))
        tots = lax.fori_loop(0, nchunk, body, tuple(jnp.zeros((8, tq), F32) for _ in preds))
        return [jnp.sum(tot, axis=0, keepdims=True) for tot in tots]

    def thr_step(it, res, nlive):
        cand = res + jnp.left_shift(jnp.int32(1), 31 - it)
        cand_f = _key_to_f32(cand)
        tot = jnp.zeros((8, tq), F32)
        for c in range(nlive):
            hit = jnp.where(score_ref[(nlive - 1) % 2, c * tq:(c + 1) * tq, :] >= cand_f, 1.0, 0.0)
            tot = tot + jnp.sum(hit.reshape(tq // 8, 8, tq), axis=0)
        cnt = jnp.sum(tot, axis=0, keepdims=True)
        return jnp.where(cnt >= topk, cand, res)

    for i_static in range(nq):
        @pl.when(i == i_static)
        def _(i_static=i_static):
            key = lax.fori_loop(0, 32, functools.partial(thr_step, nlive=i_static + 1),
                                jnp.full((1, tq), INT_MIN, I32))
            lim_ref[...] = key

    thr_key = lim_ref[...]
    thr = jnp.where(thr_key == INT_MIN, -jnp.inf, _key_to_f32(thr_key))
    n_gt, n_ge = count(lambda sc, ki: sc > thr, lambda sc, ki: sc >= thr)
    need = topk - n_gt

    idx_bits = int(k_ref.shape[0]).bit_length()
    lim_ref[...] = jnp.full((1, tq), 2 ** idx_bits, I32)

    @pl.when(jnp.max(n_ge) > topk)
    def _():
        def lim_step(it, res):
            cand = res + jnp.left_shift(jnp.int32(1), idx_bits - 1 - it)
            (cnt,) = count(lambda sc, ki: (sc == thr) & (ki < cand))
            return jnp.where(cnt <= need, cand, res)
        lim_ref[...] = lax.fori_loop(0, idx_bits, lim_step, jnp.zeros((1, tq), I32))

    lim = lim_ref[...]

    m_ref[...] = jnp.full(m_ref.shape, -jnp.inf, F32)
    acc_ref[...] = jnp.zeros(acc_ref.shape, F32)

    def attn_chunk(c, _, ahead):
        off = chunk_off(c)
        sc = score_ref[slot, pl.ds(off, tq), :]
        ki = off + key0
        sel = ((sc > thr) | ((sc == thr) & (ki < lim))) & (ki <= qry)
        bias = jnp.where(sel, 0.0, NEG)

        def qk(hh):
            sl = slice(hh * HEAD_DIM, (hh + 1) * HEAD_DIM)
            return _dot_nt(k_ref[pl.ds(off, tq), sl], q_ref[:, sl])

        heads = range(DSA_HEADS)
        zs = [bias + qk(hh) for hh in heads]
        if ahead:
            score_chunk(c, i + 1, iqn_ref, 1 - slot)
        ms = [m_ref[hh:hh + 1, :] for hh in heads]
        m_news = [jnp.maximum(ms[hh], jnp.max(zs[hh], axis=0, keepdims=True)) for hh in heads]
        ps = [jnp.exp2(zs[hh] - m_news[hh]).astype(BF16) for hh in heads]
        pvs = [_dot(vt_ref[hh, :, pl.ds(off, tq)], ps[hh]) for hh in heads]
        for hh in heads:
            acc_ref[hh] = jnp.exp2(ms[hh] - m_news[hh]) * acc_ref[hh] + pvs[hh]
            m_ref[hh:hh + 1, :] = m_news[hh]
        return 0

    @pl.when(i + 1 < nq)
    def _():
        lax.fori_loop(0, nchunk, functools.partial(attn_chunk, ahead=True), 0)
        score_chunk(nchunk, i + 1, iqn_ref, 1 - slot)

    @pl.when(i + 1 == nq)
    def _():
        lax.fori_loop(0, nchunk, functools.partial(attn_chunk, ahead=False), 0)

    for hh in range(DSA_HEADS):
        sl = slice(hh * HEAD_DIM, (hh + 1) * HEAD_DIM)
        out = (acc_ref[hh, 0:HEAD_DIM, :] * (1.0 / acc_ref[hh, HEAD_DIM:HEAD_DIM + 1, :])).T
        o_ref[:, sl] = (out * g_ref[:, sl].astype(F32)).astype(o_ref.dtype)


def _dsa_attention(hmain, small, small_t, batch, seq, topk):
    tq = DSA_TQ
    nq = seq // tq
    wide = lambda t: pl.BlockSpec((tq, DSA_W), lambda b, i: (b * nq + i, t))
    full = lambda t: pl.BlockSpec((seq, DSA_W), lambda b, i: (b, t))
    return pl.pallas_call(
        functools.partial(_dsa_kernel, topk=topk, nq=nq),
        grid=(batch, nq),
        in_specs=[wide(EV_IQ),
                  pl.BlockSpec((tq, DSA_W), lambda b, i: (b * nq + jnp.minimum(i + 1, nq - 1), EV_IQ)),
                  pl.BlockSpec((seq, LANES), lambda b, i: (b, 0)),
                  pl.BlockSpec((1, LANES, seq), lambda b, i: (b, 0, 0)),
                  wide(EV_DQ), full(EV_DK), full(EV_DV), wide(EV_DG)],
        out_specs=pl.BlockSpec((tq, DSA_W), lambda b, i: (b * nq + i, 0)),
        out_shape=jax.ShapeDtypeStruct((batch * seq, DSA_W), BF16),
        scratch_shapes=[pltpu.VMEM((2, seq, tq), F32),
                        pltpu.VMEM((DSA_HEADS, HEAD_DIM + ONES_ROWS, seq), BF16),
                        pltpu.VMEM((1, tq), I32),
                        pltpu.VMEM((DSA_HEADS, tq), F32),
                        pltpu.VMEM((DSA_HEADS, HEAD_DIM + ONES_ROWS, tq), F32)],
        compiler_params=_params(("parallel", "arbitrary")),
        name="dsa_attention",
    )(hmain, hmain, small, small_t, hmain, hmain, hmain, hmain)


def _out_proj_kernel(*refs, ny, normed):
    y_refs, w_ref, x_ref = refs[:ny], refs[ny], refs[ny + 1]
    if normed:
        g_ref, o_ref, n_ref, w16_ref = refs[ny + 2:]
    else:
        o_ref, w16_ref = refs[ny + 2:]

    @pl.when(pl.program_id(0) == 0)
    def _():
        w16_ref[...] = w_ref[0].astype(BF16)

    acc = x_ref[...]
    off = 0
    for y_ref in y_refs:
        kdim = y_ref.shape[1]
        acc = acc + _dot(y_ref[...], w16_ref[off:off + kdim, :])
        off += kdim
    o_ref[...] = acc
    if normed:
        y = acc * lax.rsqrt(jnp.mean(acc * acc, axis=-1, keepdims=True) + EPS)
        n_ref[...] = (y * g_ref[...]).astype(n_ref.dtype)


def _out_proj(ys, w_stack, layer, x2, next_gain=None, tm=512):
    m, d = x2.shape
    kdim = w_stack.shape[1]
    row = pl.BlockSpec((tm, d), lambda i: (i, 0))
    normed = next_gain is not None
    extra = ([next_gain.reshape(1, d)], [pl.BlockSpec((1, d), lambda i: (0, 0))]) if normed else ([], [])
    out = pl.pallas_call(
        functools.partial(_out_proj_kernel, ny=len(ys), normed=normed),
        grid=(m // tm,),
        in_specs=[pl.BlockSpec((tm, y.shape[1]), lambda i: (i, 0)) for y in ys]
                 + [pl.BlockSpec((1, kdim, d), lambda i: (layer, 0, 0), pipeline_mode=pl.Buffered(1)), row] + extra[1],
        out_specs=[row, row] if normed else [row],
        out_shape=[jax.ShapeDtypeStruct((m, d), F32)] + ([jax.ShapeDtypeStruct((m, d), BF16)] if normed else []),
        scratch_shapes=[pltpu.VMEM((kdim, d), BF16)],
        compiler_params=_params(("arbitrary",)),
        name="out_proj",
    )(*ys, w_stack, x2, *extra[0])
    return (out[0], out[1]) if normed else (out[0], None)


OD_TN = 512
OD_NQ = SWA_W // OD_TN


def _odd_kinds(j, aux, nslab):
    gain_ref, c64_ref, s64_ref = aux
    qk = lambda idx: (lambda a: _rope(_half_rms(a, gain_ref[idx]), c64_ref[...], s64_ref[...], SWA_HEAD_DIM))
    nk = SWA_KV_W // LANES
    return [
        (j < OD_NQ, lambda: [(qk(0), 0, nslab)]),
        (j == OD_NQ, lambda: [(qk(1), 0, nk), (lambda a: a, nk, nslab)]),
        (j > OD_NQ, lambda: [(lambda a: a * jax.nn.sigmoid(a), 0, nslab)]),
    ]


def _odd_proj(xn, w, layer, gains, c64, s64, seq):
    gspec = pl.BlockSpec((2, 1, LANES), lambda t: (0, 0, 0))
    return _skewed_proj(xn, w, layer, False, lambda j: j, (gains, c64, s64), [gspec, None, None],
                        _odd_kinds, seq, OD_TN, w.shape[2] // OD_TN, "odd_proj")


def _swa_kernel(sink_ref, q_ref, kvp_ref, kvc_ref, *rest):
    *g_refs, o_ref = rest
    n = pl.program_id(1)
    w = SWA_WINDOW
    dh = SWA_HEAD_DIM
    group = SWA_HEADS // SWA_KV_HEADS
    gw = group * w
    kb = lax.broadcasted_iota(I32, (2 * w, gw), 0)
    qi = lax.broadcasted_iota(I32, (2 * w, gw), 1) & (w - 1)
    valid = (kb > qi) & (kb <= qi + w) & ((n > 0) | (kb >= w))
    bias = jnp.where(valid, 0.0, NEG)
    head_of_lane = lax.broadcasted_iota(I32, (1, gw), 1) // w
    ones = jnp.ones((ONES_ROWS, 2 * w), BF16)

    def band(col):
        sl = slice(col, col + dh)
        return jnp.concatenate([kvp_ref[:, sl], kvc_ref[:, sl]], axis=0)

    def scores(kv):
        qg = jnp.concatenate([q_ref[:, (kv * group + t) * dh:(kv * group + t + 1) * dh] for t in range(group)],
                             axis=0)
        return _dot_nt(band(kv * dh), qg)

    s_next = scores(0)
    for kv in range(SWA_KV_HEADS):
        h0 = kv * group
        z = bias + s_next
        if kv + 1 < SWA_KV_HEADS:
            s_next = scores(kv + 1)
        sink = jnp.zeros((1, gw), F32)
        for t in range(group):
            sink = jnp.where(head_of_lane == t, sink_ref[h0 + t] * LOG2E, sink)
        m = jnp.maximum(jnp.max(z, axis=0, keepdims=True), sink)
        vt = jnp.concatenate([band(SWA_KV_W + kv * dh).astype(F32).T.astype(BF16), ones], axis=0)
        pv = _dot(vt, jnp.exp2(z - m).astype(BF16))
        den = pv[dh:dh + 1] + jnp.exp2(sink - m)
        ot = (pv[0:dh] * (1.0 / den)).T
        o = jnp.concatenate([ot[t * w:(t + 1) * w] for t in range(group)], axis=1)
        sl = slice(h0 * dh, (h0 + group) * dh)
        o_ref[:, sl] = (o * g_refs[kv][...].astype(F32)).astype(o_ref.dtype)


def _swa_attention(hodd, sinks, batch, seq):
    w = SWA_WINDOW
    nb = seq // w
    kvw = 2 * SWA_KV_W
    kv_col = SWA_W // kvw
    gate = lambda kv: pl.BlockSpec((w, kvw), lambda b, n: (b * nb + n, kv_col + 1 + kv))
    return pl.pallas_call(
        _swa_kernel,
        grid=(batch, nb),
        in_specs=[pl.BlockSpec(memory_space=pltpu.SMEM),
                  pl.BlockSpec((w, SWA_W), lambda b, n: (b * nb + n, 0)),
                  pl.BlockSpec((w, kvw), lambda b, n: (b * nb + jnp.maximum(n - 1, 0), kv_col)),
                  pl.BlockSpec((w, kvw), lambda b, n: (b * nb + n, kv_col))]
                 + [gate(kv) for kv in range(SWA_KV_HEADS)],
        out_specs=pl.BlockSpec((w, SWA_W), lambda b, n: (b * nb + n, 0)),
        out_shape=jax.ShapeDtypeStruct((batch * seq, SWA_W), BF16),
        compiler_params=_params(("parallel", "arbitrary")),
        name="swa_attention",
    )(sinks, hodd, hodd, hodd, *([hodd] * SWA_KV_HEADS))


def _rope_tables(seq, dim):
    inv = 1.0 / (ROPE_THETA ** (jnp.arange(0, dim, 2, dtype=F32) / dim))
    ang = jnp.arange(seq, dtype=F32)[:, None] * inv[None, :]
    cos, sin = jnp.cos(ang), jnp.sin(ang)
    reps = LANES // dim
    return (jnp.tile(jnp.concatenate([cos, cos], -1), (1, reps)),
            jnp.tile(jnp.concatenate([-sin, sin], -1), (1, reps)))


def _even_layer(x2, xn, pre_gain, next_gain, batch, seq, w_in_stack, j, b_f, g_fox, g_dsa, g_kidx, w_out_stack,
                tabs, topk):
    c128, s128, c64, s64 = tabs
    ncol = w_in_stack.shape[2]
    row0 = j * ncol
    fgt_row = row0 + 4 * FOX_W
    idx_row = fgt_row + FOX_HEADS + 4 * DSA_W + IDX_W
    pad = LANES - (IDX_DIM + IDX_HEADS + FOX_HEADS)
    wt = jnp.swapaxes(w_in_stack, 1, 2).reshape(-1, w_in_stack.shape[1])
    qscale = HEAD_DIM ** -0.5 * LOG2E
    one = jnp.ones((LANES,), F32)
    gains = jnp.stack([g_fox[0] * qscale, g_fox[1], one, one,
                       g_dsa[0] * qscale, g_dsa[1], one, one, one]).reshape(EV_IQ + 1, 1, LANES)
    gk = jnp.concatenate([g_kidx, jnp.zeros((LANES - IDX_DIM,), F32)]).reshape(1, LANES)
    bf = jnp.concatenate([jnp.zeros((SM_CF,), F32), b_f, jnp.zeros((pad,), F32)]).reshape(1, LANES)

    small, small_t, xn = _even_small(x2 if xn is None else xn, pre_gain if xn is None else None,
                                     wt, idx_row, fgt_row, gk, bf, c64, s64, batch, seq)
    hmain = _even_proj(xn, wt, row0, gains, c128, s128, c64, s64, seq)
    ya = _fox_attention(hmain, small, batch, seq)
    yb = _dsa_attention(hmain, small, small_t, batch, seq, topk)
    return _out_proj([ya, yb], w_out_stack, j, x2, next_gain)


def _odd_layer(x2, xn, next_gain, batch, seq, w_in_stack, j, g_qk, sinks, w_out_stack, tabs):
    _, _, c64, s64 = tabs
    qscale = jnp.array([[SWA_HEAD_DIM ** -0.5 * LOG2E], [1.0]], F32)
    gains = jnp.tile(g_qk * qscale, (1, LANES // SWA_HEAD_DIM)).reshape(2, 1, LANES)
    hodd = _odd_proj(xn, w_in_stack, j, gains, c64, s64, seq)
    y = _swa_attention(hodd, sinks, batch, seq)
    return _out_proj([y], w_out_stack, j, x2, next_gain)


def kernel(x, norm_even, w_in_even, b_f_even, g_qk_fox, g_qk_dsa, g_kidx, w_out_even,
           norm_odd, w_in_odd, g_qk_swa, sinks, w_out_odd):
    batch, seq, d = x.shape
    depth = norm_even.shape[0] + norm_odd.shape[0]
    topk = min(IDX_TOPK_MAX, seq // 4)
    tabs = _rope_tables(seq, HEAD_DIM) + _rope_tables(seq, IDX_DIM)
    x2 = x.reshape(batch * seq, d)
    pre_gain = lambda layer: (norm_even, norm_odd)[layer % 2][layer // 2]
    xn = None
    for layer in range(depth):
        j = layer // 2
        next_gain = pre_gain(layer + 1) if layer + 1 < depth else None
        if layer % 2 == 0:
            x2, xn = _even_layer(x2, xn, pre_gain(layer), next_gain, batch, seq, w_in_even, j, b_f_even[j], g_qk_fox[j],
                                 g_qk_dsa[j], g_kidx[j], w_out_even, tabs, topk)
        else:
            x2, xn = _odd_layer(x2, xn, next_gain, batch, seq, w_in_odd, j, g_qk_swa[j], sinks[j],
                                w_out_odd, tabs)
    return x2.reshape(batch, seq, d)
```

```python
import functools

import jax
import jax.numpy as jnp
from jax import lax
from jax.experimental import pallas as pl
from jax.experimental.pallas import tpu as pltpu

F32 = jnp.float32
BF16 = jnp.bfloat16
I32 = jnp.int32

HEAD_DIM = 128
FOX_HEADS = 8
DSA_HEADS = 8
IDX_HEADS = 16
IDX_DIM = 64
IDX_TOPK_MAX = 256
SWA_HEADS = 32
SWA_KV_HEADS = 4
SWA_HEAD_DIM = 64
SWA_WINDOW = 128
ROPE_THETA = 10000.0
EPS = 1e-6
NEG = -1e30

FOX_W = FOX_HEADS * HEAD_DIM
DSA_W = DSA_HEADS * HEAD_DIM
IDX_W = IDX_HEADS * IDX_DIM
SWA_W = SWA_HEADS * SWA_HEAD_DIM
SWA_KV_W = SWA_KV_HEADS * SWA_HEAD_DIM

LANES = 128
VMEM_LIMIT = 56 * 2 ** 20

SM_IK = 0
SM_IW = IDX_DIM
SM_CF = IDX_DIM + IDX_HEADS

LOG2E = 1.4426950408889634
INT_MIN = -2 ** 31


def _params(sem):
    return pltpu.CompilerParams(dimension_semantics=sem, vmem_limit_bytes=VMEM_LIMIT)


def _dot(a, b):
    return jnp.dot(a, b, preferred_element_type=F32)


def _dot_nt(a, b):
    return lax.dot_general(a, b, (((1,), (1,)), ((), ())), preferred_element_type=F32)


def _lane_iota(shape):
    return lax.broadcasted_iota(I32, shape, 1)


def _rope(y, cos, sin, dim):
    if dim == LANES:
        rot = pltpu.roll(y, LANES // 2, 1)
    else:
        half = dim // 2
        first = (_lane_iota(y.shape) & half) == 0
        rot = jnp.where(first, pltpu.roll(y, LANES - half, 1), pltpu.roll(y, half, 1))
    return y * cos + rot * sin


def _head_rms(a, gain):
    return a * lax.rsqrt(jnp.mean(a * a, axis=-1, keepdims=True) + EPS) * gain


def _half_rms(a, gain):
    lo = _lane_iota(a.shape) < 64
    sq = a * a
    ms_lo = jnp.sum(jnp.where(lo, sq, 0.0), axis=-1, keepdims=True) * (1.0 / 64)
    ms_hi = jnp.sum(jnp.where(lo, 0.0, sq), axis=-1, keepdims=True) * (1.0 / 64)
    return a * lax.rsqrt(jnp.where(lo, ms_lo, ms_hi) + EPS) * gain


def _skewed_proj_kernel(xn_ref, w_ref, *rest, kinds, transposed, nj, tn):
    *aux, o_ref, acc_ref = rest
    t = pl.program_id(0)
    last = pl.num_programs(0) - 1
    j = (t + nj - 1) % nj

    def finish(parts):
        for fn, lo, hi in parts:
            for h in range(lo, hi):
                sl = slice(h * LANES, (h + 1) * LANES)
                o_ref[:, sl] = fn(acc_ref[:, sl]).astype(o_ref.dtype)

    def multiply():
        w = (w_ref[...] if transposed else w_ref[0]).astype(BF16)
        acc_ref[...] = _dot_nt(xn_ref[...], w) if transposed else _dot(xn_ref[...], w)

    @pl.when(t == 0)
    def _():
        multiply()

    nslab = tn // LANES
    for cond, parts in kinds(j, aux, nslab):
        @pl.when((t > 0) & (t < last) & cond)
        def _(parts=parts):
            finish(parts())
            multiply()

    for cond, parts in kinds(nj - 1, aux, nslab):
        if cond:
            @pl.when(t == last)
            def _(parts=parts):
                finish(parts())


def _skewed_proj(xn, w_stack, layer, transposed, locate, aux, aux_specs, kinds, seq, tn, nj, name, tm=1024):
    m, d = xn.shape
    ntiles = (m // tm) * nj
    nrow = seq // tm
    cur = lambda t: jnp.minimum(t, ntiles - 1)
    prev = lambda t: jnp.maximum(t - 1, 0)
    tab = pl.BlockSpec((tm, LANES), lambda t: ((prev(t) // nj) % nrow, 0))
    specs = [tab if s is None else s for s in aux_specs]
    if transposed:
        wspec = pl.BlockSpec((pl.Element(tn), pl.Element(d)), lambda t: (locate(cur(t) % nj), 0))
    else:
        wspec = pl.BlockSpec((1, d, tn), lambda t: (layer, 0, locate(cur(t) % nj)))
    return pl.pallas_call(
        functools.partial(_skewed_proj_kernel, kinds=kinds, transposed=transposed, nj=nj, tn=tn),
        grid=(ntiles + 1,),
        in_specs=[pl.BlockSpec((tm, d), lambda t: (cur(t) // nj, 0)), wspec] + specs,
        out_specs=pl.BlockSpec((tm, tn), lambda t: (prev(t) // nj, prev(t) % nj)),
        out_shape=jax.ShapeDtypeStruct((m, nj * tn), BF16),
        scratch_shapes=[pltpu.VMEM((tm, tn), F32)],
        compiler_params=_params(("arbitrary",)),
        name=name,
    )(xn, w_stack, *aux)


EV_TN = 1024
EV_FQ, EV_FK, EV_FV, EV_FG, EV_DQ, EV_DK, EV_DV, EV_DG, EV_IQ = range(9)
EV_NFOX = 4


def _even_kinds(j, aux, nslab):
    gain_ref, c128_ref, s128_ref, c64_ref, s64_ref = aux
    gain = lambda: gain_ref[jnp.minimum(j, EV_IQ)]
    norm = lambda: [(lambda a: _head_rms(a, gain()), 0, nslab)]
    norm_rope = lambda: [(lambda a: _rope(_head_rms(a, gain()), c128_ref[...], s128_ref[...], HEAD_DIM), 0, nslab)]
    cast = lambda: [(lambda a: a, 0, nslab)]
    silu = lambda: [(lambda a: a * jax.nn.sigmoid(a), 0, nslab)]
    rope64 = lambda: [(lambda a: _rope(a, c64_ref[...], s64_ref[...], IDX_DIM), 0, nslab)]
    return [
        ((j == EV_FQ) | (j == EV_FK), norm),
        ((j == EV_DQ) | (j == EV_DK), norm_rope),
        ((j == EV_FV) | (j == EV_DV), cast),
        ((j == EV_FG) | (j == EV_DG), silu),
        (j == EV_IQ, rope64),
    ]


def _even_proj(xn, wt, row0, gains, c128, s128, c64, s64, seq):
    gspec = pl.BlockSpec(gains.shape, lambda t: (0, 0, 0))
    row = lambda j: (row0 // 8 + j * (EV_TN // 8) + (j + EV_NFOX) // (2 * EV_NFOX)) * 8
    return _skewed_proj(xn, wt, None, True, row, (gains, c128, s128, c64, s64),
                        [gspec, None, None, None, None], _even_kinds, seq, EV_TN, EV_IQ + 1, "even_proj")


ES_TM = 512


def _even_small_kernel(*refs, normalize):
    if normalize:
        x_ref, g_ref, wi_ref, wf_ref, gk_ref, bf_ref, c64_ref, s64_ref, o_ref, ot_ref, xn_ref, carry_ref = refs
        x = x_ref[...]
        xn = (x * lax.rsqrt(jnp.mean(x * x, axis=-1, keepdims=True) + EPS) * g_ref[...]).astype(BF16)
        xn_ref[...] = xn
    else:
        xn_ref, wi_ref, wf_ref, gk_ref, bf_ref, c64_ref, s64_ref, o_ref, ot_ref, carry_ref = refs
        xn = xn_ref[...]
    pad = jnp.zeros((LANES - wi_ref.shape[0] - wf_ref.shape[0], xn.shape[1]), F32)
    wt = jnp.concatenate([wi_ref[...], wf_ref[...], pad], axis=0).astype(BF16)
    h = _dot_nt(xn, wt)
    lane = _lane_iota(h.shape)
    is_ik = lane < SM_IW
    ms = jnp.sum(jnp.where(is_ik, h * h, 0.0), axis=-1, keepdims=True) * (1.0 / IDX_DIM)
    ik = _rope(h * lax.rsqrt(ms + EPS) * gk_ref[...], c64_ref[...], s64_ref[...], IDX_DIM)
    iw = h * (IDX_HEADS ** -0.5 * IDX_DIM ** -0.5)
    c = jax.nn.log_sigmoid(h + bf_ref[...])
    row = lax.broadcasted_iota(I32, h.shape, 0)
    step = 1
    while step < h.shape[0]:
        c = c + jnp.where(row >= step, pltpu.roll(c, step, 0), 0.0)
        step *= 2

    @pl.when(pl.program_id(1) == 0)
    def _():
        carry_ref[...] = jnp.zeros(carry_ref.shape, F32)

    c = c + carry_ref[...]
    carry_ref[...] = c[h.shape[0] - 1:, :]
    out = jnp.where(is_ik, ik, jnp.where(lane < SM_CF, iw, c))
    o_ref[...] = out
    ot_ref[0] = out.T


def _even_small(x_or_xn, norm_gain, wt, idx_row, fgt_row, gk, bf, c64, s64, batch, seq):
    m, d = x_or_xn.shape
    tm = ES_TM
    nr = seq // tm
    normalize = norm_gain is not None
    vec = pl.BlockSpec((1, LANES), lambda b, r: (0, 0))
    tab = pl.BlockSpec((tm, LANES), lambda b, r: (r, 0))
    rows = pl.BlockSpec((tm, d), lambda b, r: (b * nr + r, 0))
    head = ([x_or_xn, norm_gain.reshape(1, d)], [rows, pl.BlockSpec((1, d), lambda b, r: (0, 0))]) if normalize \
        else ([x_or_xn], [rows])
    out = pl.pallas_call(
        functools.partial(_even_small_kernel, normalize=normalize),
        grid=(batch, nr),
        in_specs=head[1] + [pl.BlockSpec((pl.Element(IDX_DIM + IDX_HEADS), pl.Element(d)), lambda b, r: (idx_row, 0)),
                            pl.BlockSpec((pl.Element(FOX_HEADS), pl.Element(d)), lambda b, r: (fgt_row, 0)),
                            vec, vec, tab, tab],
        out_specs=[pl.BlockSpec((tm, LANES), lambda b, r: (b * nr + r, 0)),
                   pl.BlockSpec((1, LANES, tm), lambda b, r: (b, 0, r))] + ([rows] if normalize else []),
        out_shape=[jax.ShapeDtypeStruct((m, LANES), F32),
                   jax.ShapeDtypeStruct((batch, LANES, seq), F32)]
                  + ([jax.ShapeDtypeStruct((m, d), BF16)] if normalize else []),
        scratch_shapes=[pltpu.VMEM((1, LANES), F32)],
        compiler_params=_params(("parallel", "arbitrary")),
        name="even_small",
    )(*head[0], wt, wt, gk, bf, c64, s64)
    return (out[0], out[1], out[2]) if normalize else (out[0], out[1], x_or_xn)


FOX_TQ = 256
FOX_HPS = 8
ONES_ROWS = 16


def _fox_kernel(q_ref, k_ref, v_ref, g_ref, small_ref, o_ref, ka_ref, vt_ref, *, nq):
    hg = pl.program_id(1)
    i = pl.program_id(2)
    tq = FOX_TQ
    seq = k_ref.shape[0]
    heads = [slice(u * HEAD_DIM, (u + 1) * HEAD_DIM) for u in range(FOX_HPS)]
    lane = _lane_iota((seq, LANES))

    @pl.when(i == 0)
    def _():
        sm = small_ref[...]
        for u, sl in enumerate(heads):
            vt_ref[u, 0:HEAD_DIM, :] = v_ref[:, sl].astype(F32).T.astype(BF16)
            vt_ref[u, HEAD_DIM:, :] = jnp.ones((ONES_ROWS, seq), BF16)
            ck = jnp.sum(jnp.where(lane == SM_CF + hg * FOX_HPS + u, sm, 0.0), axis=1, keepdims=True)
            neg = jnp.broadcast_to(ck * -LOG2E, (seq, LANES))
            hi = neg.astype(BF16).astype(F32)
            mid = (neg - hi).astype(BF16).astype(F32)
            lo = neg - hi - mid
            extra = jnp.where(lane == 0, hi, jnp.where(lane == 1, mid, jnp.where(lane == 2, lo, 0.0)))
            ka_ref[u, :, 0:HEAD_DIM] = k_ref[:, sl]
            ka_ref[u, :, HEAD_DIM:] = extra.astype(BF16)

    ones3 = jnp.where(_lane_iota((tq, LANES)) < 3, 1.0, 0.0).astype(BF16)
    qa = [jnp.concatenate([q_ref[:, sl], ones3], axis=1) for sl in heads]

    def qk(u, j):
        return _dot_nt(ka_ref[u, pl.ds(j * tq, tq), :], qa[u])

    def step(u, j, z, carry, diagonal):
        m, acc = carry
        if diagonal:
            key = lax.broadcasted_iota(I32, (tq, tq), 0)
            qry = lax.broadcasted_iota(I32, (tq, tq), 1)
            z = jnp.where(key <= qry, z, NEG)
        m_new = jnp.maximum(m, jnp.max(z, axis=0, keepdims=True))
        p = jnp.exp2(z - m_new)
        acc = jnp.exp2(m - m_new) * acc + _dot(vt_ref[u, :, pl.ds(j * tq, tq)], p.astype(BF16))
        return m_new, acc

    init = (jnp.full((1, tq), -jnp.inf, F32), jnp.zeros((HEAD_DIM + ONES_ROWS, tq), F32))
    for i_static in range(nq):
        @pl.when(i == i_static)
        def _(i_static=i_static):
            carry = [init] * FOX_HPS
            z_next = [qk(u, 0) for u in range(FOX_HPS)]
            for j in range(i_static + 1):
                for u in range(FOX_HPS):
                    z = z_next[u]
                    if j < i_static:
                        z_next[u] = qk(u, j + 1)
                    carry[u] = step(u, j, z, carry[u], j == i_static)
            for u, sl in enumerate(heads):
                _, acc = carry[u]
                out = (acc[0:HEAD_DIM] * (1.0 / acc[HEAD_DIM:HEAD_DIM + 1])).T
                o_ref[:, sl] = (out * g_ref[:, sl].astype(F32)).astype(o_ref.dtype)


def _fox_attention(hmain, small, batch, seq):
    tq = FOX_TQ
    nq = seq // tq
    width = FOX_HPS * HEAD_DIM
    ng = FOX_W // width
    return pl.pallas_call(
        functools.partial(_fox_kernel, nq=nq),
        grid=(batch, ng, nq),
        in_specs=[pl.BlockSpec((tq, width), lambda b, h, i: (b * nq + i, EV_FQ * ng + h)),
                  pl.BlockSpec((seq, width), lambda b, h, i: (b, EV_FK * ng + h)),
                  pl.BlockSpec((seq, width), lambda b, h, i: (b, EV_FV * ng + h)),
                  pl.BlockSpec((tq, width), lambda b, h, i: (b * nq + i, EV_FG * ng + h)),
                  pl.BlockSpec((seq, LANES), lambda b, h, i: (b, 0))],
        out_specs=pl.BlockSpec((tq, width), lambda b, h, i: (b * nq + i, h)),
        out_shape=jax.ShapeDtypeStruct((batch * seq, FOX_W), BF16),
        scratch_shapes=[pltpu.VMEM((FOX_HPS, seq, 2 * HEAD_DIM), BF16),
                        pltpu.VMEM((FOX_HPS, HEAD_DIM + ONES_ROWS, seq), BF16)],
        compiler_params=_params(("parallel", "parallel", "arbitrary")),
        name="fox_attention",
    )(hmain, hmain, hmain, hmain, small)


DSA_TQ = 256


def _key_to_f32(key):
    return pltpu.bitcast(key ^ ((key >> 31) & 0x7FFFFFFF), F32)


def _dsa_kernel(iq_ref, iqn_ref, small_ref, smallt_ref, q_ref, k_ref, v_ref, g_ref, o_ref,
                score_ref, vt_ref, lim_ref, m_ref, acc_ref, *, topk, nq):
    i = pl.program_id(1)
    tq = DSA_TQ
    nchunk = i + 1
    slot = i % 2
    key0 = lax.broadcasted_iota(I32, (tq, tq), 0)
    lane0 = lax.broadcasted_iota(I32, (tq, tq), 1)
    qry = i * tq + lane0

    @pl.when(i == 0)
    def _():
        for hh in range(DSA_HEADS):
            sl = slice(hh * HEAD_DIM, (hh + 1) * HEAD_DIM)
            vt_ref[hh, 0:HEAD_DIM, :] = v_ref[:, sl].astype(F32).T.astype(BF16)
            vt_ref[hh, HEAD_DIM:, :] = jnp.ones((ONES_ROWS, v_ref.shape[0]), BF16)

    def chunk_off(c):
        return pl.multiple_of(c * tq, tq)

    def score_chunk(c, blk, q_idx_ref, dst):
        off = chunk_off(c)
        qoff = pl.multiple_of(blk * tq, tq)
        ikc = small_ref[pl.ds(off, tq), SM_IK:SM_IK + IDX_DIM].astype(BF16)

        def logits(hh):
            return _dot_nt(ikc, q_idx_ref[:, hh * IDX_DIM:(hh + 1) * IDX_DIM])

        acc = jnp.zeros((tq, tq), F32)
        nxt = logits(0)
        for hh in range(IDX_HEADS):
            cur = nxt
            if hh + 1 < IDX_HEADS:
                nxt = logits(hh + 1)
            w = smallt_ref[0, SM_IW + hh:SM_IW + hh + 1, pl.ds(qoff, tq)]
            acc = acc + w * jnp.maximum(cur, 0.0)
        score_ref[dst, pl.ds(off, tq), :] = jnp.where(off + key0 <= qoff + lane0, acc, NEG)

    @pl.when(i == 0)
    def _():
        score_chunk(0, 0, iq_ref, 0)

    def count(*preds):
        def body(c, tots):
            off = chunk_off(c)
            sc = score_ref[slot, pl.ds(off, tq), :]
            hits = [jnp.where(pred(sc, off + key0), 1.0, 0.0) for pred in preds]
            return tuple(tot + jnp.sum(hit.reshape(tq // 8, 8, tq), axis=0)
                         for tot, hit in zip(tots, hits))
        tots = lax.fori_loop(0, nchunk, body, tuple(jnp.zeros((8, tq), F32) for _ in preds))
        return [jnp.sum(tot, axis=0, keepdims=True) for tot in tots]

    def thr_step(it, res, nlive):
        cand = res + jnp.left_shift(jnp.int32(1), 31 - it)
        cand_f = _key_to_f32(cand)
        tot = jnp.zeros((8, tq), F32)
        for c in range(nlive):
            hit = jnp.where(score_ref[(nlive - 1) % 2, c * tq:(c + 1) * tq, :] >= cand_f, 1.0, 0.0)
            tot = tot + jnp.sum(hit.reshape(tq // 8, 8, tq), axis=0)
        cnt = jnp.sum(tot, axis=0, keepdims=True)
        return jnp.where(cnt >= topk, cand, res)

    for i_static in range(nq):
        @pl.when(i == i_static)
        def _(i_static=i_static):
            key = lax.fori_loop(0, 32, functools.partial(thr_step, nlive=i_static + 1),
                                jnp.full((1, tq), INT_MIN, I32))
            lim_ref[...] = key

    thr_key = lim_ref[...]
    thr = jnp.where(thr_key == INT_MIN, -jnp.inf, _key_to_f32(thr_key))
    n_gt, n_ge = count(lambda sc, ki: sc > thr, lambda sc, ki: sc >= thr)
    need = topk - n_gt

    idx_bits = int(k_ref.shape[0]).bit_length()
    lim_ref[...] = jnp.full((1, tq), 2 ** idx_bits, I32)

    @pl.when(jnp.max(n_ge) > topk)
    def _():
        def lim_step(it, res):
            cand = res + jnp.left_shift(jnp.int32(1), idx_bits - 1 - it)
            (cnt,) = count(lambda sc, ki: (sc == thr) & (ki < cand))
            return jnp.where(cnt <= need, cand, res)
        lim_ref[...] = lax.fori_loop(0, idx_bits, lim_step, jnp.zeros((1, tq), I32))

    lim = lim_ref[...]

    m_ref[...] = jnp.full(m_ref.shape, -jnp.inf, F32)
    acc_ref[...] = jnp.zeros(acc_ref.shape, F32)

    def attn_chunk(c, _, ahead):
        off = chunk_off(c)
        sc = score_ref[slot, pl.ds(off, tq), :]
        ki = off + key0
        sel = ((sc > thr) | ((sc == thr) & (ki < lim))) & (ki <= qry)
        bias = jnp.where(sel, 0.0, NEG)

        def qk(hh):
            sl = slice(hh * HEAD_DIM, (hh + 1) * HEAD_DIM)
            return _dot_nt(k_ref[pl.ds(off, tq), sl], q_ref[:, sl])

        heads = range(DSA_HEADS)
        zs = [bias + qk(hh) for hh in heads]
        if ahead:
            score_chunk(c, i + 1, iqn_ref, 1 - slot)
        ms = [m_ref[hh:hh + 1, :] for hh in heads]
        m_news = [jnp.maximum(ms[hh], jnp.max(zs[hh], axis=0, keepdims=True)) for hh in heads]
        ps = [jnp.exp2(zs[hh] - m_news[hh]).astype(BF16) for hh in heads]
        pvs = [_dot(vt_ref[hh, :, pl.ds(off, tq)], ps[hh]) for hh in heads]
        for hh in heads:
            acc_ref[hh] = jnp.exp2(ms[hh] - m_news[hh]) * acc_ref[hh] + pvs[hh]
            m_ref[hh:hh + 1, :] = m_news[hh]
        return 0

    @pl.when(i + 1 < nq)
    def _():
        lax.fori_loop(0, nchunk, functools.partial(attn_chunk, ahead=True), 0)
        score_chunk(nchunk, i + 1, iqn_ref, 1 - slot)

    @pl.when(i + 1 == nq)
    def _():
        lax.fori_loop(0, nchunk, functools.partial(attn_chunk, ahead=False), 0)

    for hh in range(DSA_HEADS):
        sl = slice(hh * HEAD_DIM, (hh + 1) * HEAD_DIM)
        out = (acc_ref[hh, 0:HEAD_DIM, :] * (1.0 / acc_ref[hh, HEAD_DIM:HEAD_DIM + 1, :])).T
        o_ref[:, sl] = (out * g_ref[:, sl].astype(F32)).astype(o_ref.dtype)


def _dsa_attention(hmain, small, small_t, batch, seq, topk):
    tq = DSA_TQ
    nq = seq // tq
    wide = lambda t: pl.BlockSpec((tq, DSA_W), lambda b, i: (b * nq + i, t))
    full = lambda t: pl.BlockSpec((seq, DSA_W), lambda b, i: (b, t))
    return pl.pallas_call(
        functools.partial(_dsa_kernel, topk=topk, nq=nq),
        grid=(batch, nq),
        in_specs=[wide(EV_IQ),
                  pl.BlockSpec((tq, DSA_W), lambda b, i: (b * nq + jnp.minimum(i + 1, nq - 1), EV_IQ)),
                  pl.BlockSpec((seq, LANES), lambda b, i: (b, 0)),
                  pl.BlockSpec((1, LANES, seq), lambda b, i: (b, 0, 0)),
                  wide(EV_DQ), full(EV_DK), full(EV_DV), wide(EV_DG)],
        out_specs=pl.BlockSpec((tq, DSA_W), lambda b, i: (b * nq + i, 0)),
        out_shape=jax.ShapeDtypeStruct((batch * seq, DSA_W), BF16),
        scratch_shapes=[pltpu.VMEM((2, seq, tq), F32),
                        pltpu.VMEM((DSA_HEADS, HEAD_DIM + ONES_ROWS, seq), BF16),
                        pltpu.VMEM((1, tq), I32),
                        pltpu.VMEM((DSA_HEADS, tq), F32),
                        pltpu.VMEM((DSA_HEADS, HEAD_DIM + ONES_ROWS, tq), F32)],
        compiler_params=_params(("parallel", "arbitrary")),
        name="dsa_attention",
    )(hmain, hmain, small, small_t, hmain, hmain, hmain, hmain)


def _out_proj_kernel(*refs, ny, normed):
    y_refs, w_ref, x_ref = refs[:ny], refs[ny], refs[ny + 1]
    if normed:
        g_ref, o_ref, n_ref, w16_ref = refs[ny + 2:]
    else:
        o_ref, w16_ref = refs[ny + 2:]

    @pl.when(pl.program_id(0) == 0)
    def _():
        w16_ref[...] = w_ref[0].astype(BF16)

    acc = x_ref[...]
    off = 0
    for y_ref in y_refs:
        kdim = y_ref.shape[1]
        acc = acc + _dot(y_ref[...], w16_ref[off:off + kdim, :])
        off += kdim
    o_ref[...] = acc
    if normed:
        y = acc * lax.rsqrt(jnp.mean(acc * acc, axis=-1, keepdims=True) + EPS)
        n_ref[...] = (y * g_ref[...]).astype(n_ref.dtype)


def _out_proj(ys, w_stack, layer, x2, next_gain=None, tm=512):
    m, d = x2.shape
    kdim = w_stack.shape[1]
    row = pl.BlockSpec((tm, d), lambda i: (i, 0))
    normed = next_gain is not None
    extra = ([next_gain.reshape(1, d)], [pl.BlockSpec((1, d), lambda i: (0, 0))]) if normed else ([], [])
    out = pl.pallas_call(
        functools.partial(_out_proj_kernel, ny=len(ys), normed=normed),
        grid=(m // tm,),
        in_specs=[pl.BlockSpec((tm, y.shape[1]), lambda i: (i, 0)) for y in ys]
                 + [pl.BlockSpec((1, kdim, d), lambda i: (layer, 0, 0), pipeline_mode=pl.Buffered(1)), row] + extra[1],
        out_specs=[row, row] if normed else [row],
        out_shape=[jax.ShapeDtypeStruct((m, d), F32)] + ([jax.ShapeDtypeStruct((m, d), BF16)] if normed else []),
        scratch_shapes=[pltpu.VMEM((kdim, d), BF16)],
        compiler_params=_params(("arbitrary",)),
        name="out_proj",
    )(*ys, w_stack, x2, *extra[0])
    return (out[0], out[1]) if normed else (out[0], None)


OD_TN = 512
OD_NQ = SWA_W // OD_TN


def _odd_kinds(j, aux, nslab):
    gain_ref, c64_ref, s64_ref = aux
    qk = lambda idx: (lambda a: _rope(_half_rms(a, gain_ref[idx]), c64_ref[...], s64_ref[...], SWA_HEAD_DIM))
    nk = SWA_KV_W // LANES
    return [
        (j < OD_NQ, lambda: [(qk(0), 0, nslab)]),
        (j == OD_NQ, lambda: [(qk(1), 0, nk), (lambda a: a, nk, nslab)]),
        (j > OD_NQ, lambda: [(lambda a: a * jax.nn.sigmoid(a), 0, nslab)]),
    ]


def _odd_proj(xn, w, layer, gains, c64, s64, seq):
    gspec = pl.BlockSpec((2, 1, LANES), lambda t: (0, 0, 0))
    return _skewed_proj(xn, w, layer, False, lambda j: j, (gains, c64, s64), [gspec, None, None],
                        _odd_kinds, seq, OD_TN, w.shape[2] // OD_TN, "odd_proj")


def _swa_kernel(sink_ref, q_ref, kvp_ref, kvc_ref, *rest):
    *g_refs, o_ref, bias_ref = rest
    n = pl.program_id(1)
    w = SWA_WINDOW
    dh = SWA_HEAD_DIM
    group = SWA_HEADS // SWA_KV_HEADS
    gw = group * w
    @pl.when(n <= 1)
    def _():
        kb = lax.broadcasted_iota(I32, (2 * w, gw), 0)
        qi = lax.broadcasted_iota(I32, (2 * w, gw), 1) & (w - 1)
        valid = (kb > qi) & (kb <= qi + w) & ((n > 0) | (kb >= w))
        bias_ref[...] = jnp.where(valid, 0.0, NEG)

    bias = bias_ref[...]
    head_of_lane = lax.broadcasted_iota(I32, (1, gw), 1) // w
    ones = jnp.ones((ONES_ROWS, 2 * w), BF16)

    def band(col):
        sl = slice(col, col + dh)
        return jnp.concatenate([kvp_ref[:, sl], kvc_ref[:, sl]], axis=0)

    def scores(kv):
        qg = jnp.concatenate([q_ref[:, (kv * group + t) * dh:(kv * group + t + 1) * dh] for t in range(group)],
                             axis=0)
        return _dot_nt(band(kv * dh), qg)

    s_next = scores(0)
    for kv in range(SWA_KV_HEADS):
        h0 = kv * group
        z = bias + s_next
        if kv + 1 < SWA_KV_HEADS:
            s_next = scores(kv + 1)
        sink = jnp.zeros((1, gw), F32)
        for t in range(group):
            sink = jnp.where(head_of_lane == t, sink_ref[h0 + t] * LOG2E, sink)
        m = jnp.maximum(jnp.max(z, axis=0, keepdims=True), sink)
        vt = jnp.concatenate([band(SWA_KV_W + kv * dh).astype(F32).T.astype(BF16), ones], axis=0)
        pv = _dot(vt, jnp.exp2(z - m).astype(BF16))
        den = pv[dh:dh + 1] + jnp.exp2(sink - m)
        ot = (pv[0:dh] * (1.0 / den)).T
        o = jnp.concatenate([ot[t * w:(t + 1) * w] for t in range(group)], axis=1)
        sl = slice(h0 * dh, (h0 + group) * dh)
        o_ref[:, sl] = (o * g_refs[kv][...].astype(F32)).astype(o_ref.dtype)


def _swa_attention(hodd, sinks, batch, seq):
    w = SWA_WINDOW
    nb = seq // w
    kvw = 2 * SWA_KV_W
    kv_col = SWA_W // kvw
    gate = lambda kv: pl.BlockSpec((w, kvw), lambda b, n: (b * nb + n, kv_col + 1 + kv))
    return pl.pallas_call(
        _swa_kernel,
        grid=(batch, nb),
        in_specs=[pl.BlockSpec(memory_space=pltpu.SMEM),
                  pl.BlockSpec((w, SWA_W), lambda b, n: (b * nb + n, 0)),
                  pl.BlockSpec((w, kvw), lambda b, n: (b * nb + jnp.maximum(n - 1, 0), kv_col)),
                  pl.BlockSpec((w, kvw), lambda b, n: (b * nb + n, kv_col))]
                 + [gate(kv) for kv in range(SWA_KV_HEADS)],
        out_specs=pl.BlockSpec((w, SWA_W), lambda b, n: (b * nb + n, 0)),
        out_shape=jax.ShapeDtypeStruct((batch * seq, SWA_W), BF16),
        scratch_shapes=[pltpu.VMEM((2 * w, SWA_HEADS // SWA_KV_HEADS * w), F32)],
        compiler_params=_params(("parallel", "arbitrary")),
        name="swa_attention",
    )(sinks, hodd, hodd, hodd, *([hodd] * SWA_KV_HEADS))


def _rope_tables(seq, dim):
    inv = 1.0 / (ROPE_THETA ** (jnp.arange(0, dim, 2, dtype=F32) / dim))
    ang = jnp.arange(seq, dtype=F32)[:, None] * inv[None, :]
    cos, sin = jnp.cos(ang), jnp.sin(ang)
    reps = LANES // dim
    return (jnp.tile(jnp.concatenate([cos, cos], -1), (1, reps)),
            jnp.tile(jnp.concatenate([-sin, sin], -1), (1, reps)))


def _even_layer(x2, xn, pre_gain, next_gain, batch, seq, w_in_stack, j, b_f, g_fox, g_dsa, g_kidx, w_out_stack,
                tabs, topk):
    c128, s128, c64, s64 = tabs
    ncol = w_in_stack.shape[2]
    row0 = j * ncol
    fgt_row = row0 + 4 * FOX_W
    idx_row = fgt_row + FOX_HEADS + 4 * DSA_W + IDX_W
    pad = LANES - (IDX_DIM + IDX_HEADS + FOX_HEADS)
    wt = jnp.swapaxes(w_in_stack, 1, 2).reshape(-1, w_in_stack.shape[1])
    qscale = HEAD_DIM ** -0.5 * LOG2E
    one = jnp.ones((LANES,), F32)
    gains = jnp.stack([g_fox[0] * qscale, g_fox[1], one, one,
                       g_dsa[0] * qscale, g_dsa[1], one, one, one]).reshape(EV_IQ + 1, 1, LANES)
    gk = jnp.concatenate([g_kidx, jnp.zeros((LANES - IDX_DIM,), F32)]).reshape(1, LANES)
    bf = jnp.concatenate([jnp.zeros((SM_CF,), F32), b_f, jnp.zeros((pad,), F32)]).reshape(1, LANES)

    small, small_t, xn = _even_small(x2 if xn is None else xn, pre_gain if xn is None else None,
                                     wt, idx_row, fgt_row, gk, bf, c64, s64, batch, seq)
    hmain = _even_proj(xn, wt, row0, gains, c128, s128, c64, s64, seq)
    ya = _fox_attention(hmain, small, batch, seq)
    yb = _dsa_attention(hmain, small, small_t, batch, seq, topk)
    return _out_proj([ya, yb], w_out_stack, j, x2, next_gain)


def _odd_layer(x2, xn, next_gain, batch, seq, w_in_stack, j, g_qk, sinks, w_out_stack, tabs):
    _, _, c64, s64 = tabs
    qscale = jnp.array([[SWA_HEAD_DIM ** -0.5 * LOG2E], [1.0]], F32)
    gains = jnp.tile(g_qk * qscale, (1, LANES // SWA_HEAD_DIM)).reshape(2, 1, LANES)
    hodd = _odd_proj(xn, w_in_stack, j, gains, c64, s64, seq)
    y = _swa_attention(hodd, sinks, batch, seq)
    return _out_proj([y], w_out_stack, j, x2, next_gain)


def kernel(x, norm_even, w_in_even, b_f_even, g_qk_fox, g_qk_dsa, g_kidx, w_out_even,
           norm_odd, w_in_odd, g_qk_swa, sinks, w_out_odd):
    batch, seq, d = x.shape
    depth = norm_even.shape[0] + norm_odd.shape[0]
    topk = min(IDX_TOPK_MAX, seq // 4)
    tabs = _rope_tables(seq, HEAD_DIM) + _rope_tables(seq, IDX_DIM)
    x2 = x.reshape(batch * seq, d)
    pre_gain = lambda layer: (norm_even, norm_odd)[layer % 2][layer // 2]
    xn = None
    for layer in range(depth):
        j = layer // 2
        next_gain = pre_gain(layer + 1) if layer + 1 < depth else None
        if layer % 2 == 0:
            x2, xn = _even_layer(x2, xn, pre_gain(layer), next_gain, batch, seq, w_in_even, j, b_f_even[j], g_qk_fox[j],
                                 g_qk_dsa[j], g_kidx[j], w_out_even, tabs, topk)
        else:
            x2, xn = _odd_layer(x2, xn, next_gain, batch, seq, w_in_odd, j, g_qk_swa[j], sinks[j],
                                w_out_odd, tabs)
    return x2.reshape(batch, seq, d)
```
